```python
import math
import jax, jax.numpy as jnp
from jax import lax
import numpy as np

D_MODEL = 1024
BATCH = 8
SEQ = 2048
DEPTH = 1
DEC_BATCH = 128
DEC_SEQ = 1
PAST_LEN = 8192
PAGE_SIZE = 128

SSD_EXPAND = 2
D_INNER = SSD_EXPAND * D_MODEL
SSD_HEAD_DIM = 64
SSD_HEADS = D_INNER // SSD_HEAD_DIM
SSD_GROUPS = 8
SSD_HEADS_PER_GROUP = SSD_HEADS // SSD_GROUPS
D_STATE = 128
CONV_W = 4
CONV_DIM = D_INNER + 2 * SSD_GROUPS * D_STATE
SSD_CHUNK = 128
ATT_HEADS = 16
ATT_HEAD_DIM = 64
ATT_WIDTH = ATT_HEADS * ATT_HEAD_DIM
KV_HEADS = 4
KV_REP = ATT_HEADS // KV_HEADS
IDX_HEADS = 8
IDX_DIM = 64
TOPK_KEYS_MAX = 256
Q_BLOCK = 128
NUM_BUCKETS = 32
MAX_DISTANCE = 128
PEER_HEADS = 8
PEER_N_KEYS = 128
PEER_EXPERTS = PEER_N_KEYS * PEER_N_KEYS
PEER_KEY_DIM = 256
PEER_HALF = PEER_KEY_DIM // 2
PEER_TOPK = 16
PEER_BLOCK = 128
EPS = 1e-6
IN_SPLITS = (D_INNER, CONV_DIM, SSD_HEADS,
             ATT_WIDTH, KV_HEADS * ATT_HEAD_DIM, KV_HEADS * ATT_HEAD_DIM,
             IDX_HEADS * IDX_DIM, IDX_DIM, IDX_HEADS,
             D_MODEL, D_MODEL)
IN_COLS = sum(IN_SPLITS)

kernel_name = 'hybrid_ssd_dsa_peer_step'

F32 = jnp.float32


def rms_norm(x, g):
    xf = x.astype(F32)
    y = xf * lax.rsqrt(jnp.mean(xf * xf, axis=-1, keepdims=True) + EPS)
    return (y * g.astype(F32)).astype(x.dtype)


def split_cols(proj):
    offs = np.cumsum(np.array(IN_SPLITS))[:-1].tolist()
    return jnp.split(proj, offs, axis=-1)


def causal_dwconv(xpad, w, b):
    out = lax.conv_general_dilated(xpad, w[:, None, :], window_strides=(1,), padding='VALID',
                                   dimension_numbers=('NWC', 'WIO', 'NWC'),
                                   feature_group_count=xpad.shape[-1])
    return jax.nn.silu(out + b)


def t5_bucket(dist):
    n = jnp.maximum(dist, 0)
    max_exact = NUM_BUCKETS // 2
    nf = jnp.maximum(n, 1).astype(F32)
    large = max_exact + (jnp.log(nf / max_exact) / math.log(MAX_DISTANCE / max_exact)
                         * (NUM_BUCKETS - max_exact)).astype(jnp.int32)
    large = jnp.minimum(large, NUM_BUCKETS - 1)
    return jnp.where(n < max_exact, n, large)


def indexer_topk(iq, iw, ik, q_pos, n_sel):
    s = jnp.einsum('bthd,bsd->bths', iq, ik, preferred_element_type=F32) * IDX_DIM ** -0.5
    score = jnp.einsum('bth,bths->bts', iw.astype(F32), jax.nn.relu(s))
    k_pos = jnp.arange(ik.shape[1])
    score = jnp.where(k_pos[None, None, :] <= q_pos[None, :, None], score, -jnp.inf)
    _, idx = lax.top_k(score, n_sel)
    return idx


def sparse_attention(q, k_sel, v_sel, q_pos, idx, rel_bias):
    B, T = q.shape[:2]
    K = idx.shape[-1]
    qg = q.reshape(B, T, KV_HEADS, KV_REP, ATT_HEAD_DIM)
    logits = jnp.einsum('btgrd,btkgd->btgrk', qg, k_sel, preferred_element_type=F32) * ATT_HEAD_DIM ** -0.5
    dist = q_pos[None, :, None] - idx
    bias = rel_bias.astype(F32)[t5_bucket(dist)]
    bias = bias.reshape(B, T, K, KV_HEADS, KV_REP).transpose(0, 1, 3, 4, 2)
    valid = (dist >= 0)[:, :, None, None, :]
    p = jax.nn.softmax(jnp.where(valid, logits + bias, -jnp.inf), axis=-1)
    out = jnp.einsum('btgrk,btkgd->btgrd', p.astype(v_sel.dtype), v_sel)
    return out.reshape(B, T, ATT_WIDTH)


def gather_rows(rows, ids):
    return jax.vmap(lambda r, i: r[i])(rows, ids)


def attn_inputs(q_raw, k_raw, v_raw, iq_raw, iw_raw, p):
    B, T = q_raw.shape[:2]
    q = rms_norm(q_raw.reshape(B, T, ATT_HEADS, ATT_HEAD_DIM), p['q_norm_g'])
    k = rms_norm(k_raw.reshape(B, T, KV_HEADS, ATT_HEAD_DIM), p['k_norm_g'])
    v = v_raw.reshape(B, T, KV_HEADS, ATT_HEAD_DIM)
    iq = iq_raw.reshape(B, T, IDX_HEADS, IDX_DIM)
    iw = iw_raw * IDX_HEADS ** -0.5
    return q, k, v, iq, iw


def attention_prompt(q, k, v, iq, ik, iw, rel_bias):
    B, T = q.shape[:2]
    n_sel = min(TOPK_KEYS_MAX, T // 4)
    nb = T // Q_BLOCK

    def blocks(a):
        return jnp.moveaxis(a.reshape(B, nb, Q_BLOCK, *a.shape[2:]), 1, 0)

    pos = jnp.arange(T).reshape(nb, Q_BLOCK)

    def one_block(inp):
        qb, iqb, iwb, posb = inp
        idx = indexer_topk(iqb, iwb, ik, posb, n_sel)
        return sparse_attention(qb, gather_rows(k, idx), gather_rows(v, idx), posb, idx, rel_bias)

    out = lax.map(one_block, (blocks(q), blocks(iq), blocks(iw), pos))
    return jnp.moveaxis(out, 0, 1).reshape(B, T, ATT_WIDTH)


def attention_sample(q, k, v, iq, ik, iw, rel_bias, cache_k, cache_v, cache_ik, page_table):
    DB, DS = q.shape[:2]
    past = page_table.shape[1] * PAGE_SIZE
    n_sel = min(TOPK_KEYS_MAX, (past + DS) // 4)
    rows = (page_table[:, :, None] * PAGE_SIZE
            + jnp.arange(PAGE_SIZE, dtype=jnp.int32)[None, None, :]).reshape(DB, past)
    pool_k = cache_k.reshape(-1, KV_HEADS, ATT_HEAD_DIM)
    pool_v = cache_v.reshape(-1, KV_HEADS, ATT_HEAD_DIM)
    pool_ik = cache_ik.reshape(-1, IDX_DIM)
    ik_all = jnp.concatenate([pool_ik[rows], ik], axis=1)
    q_pos = past + jnp.arange(DS)
    idx = indexer_topk(iq, iw, ik_all, q_pos, n_sel)
    is_new = (idx >= past)[..., None, None]
    prow = jnp.take_along_axis(rows, jnp.minimum(idx, past - 1).reshape(DB, -1), axis=1).reshape(idx.shape)
    nidx = jnp.clip(idx - past, 0, DS - 1)
    k_sel = jnp.where(is_new, gather_rows(k, nidx), pool_k[prow])
    v_sel = jnp.where(is_new, gather_rows(v, nidx), pool_v[prow])
    return sparse_attention(q, k_sel, v_sel, q_pos, idx, rel_bias)


def ssd_inputs(xbc_conv, dt_raw, p):
    B, T = xbc_conv.shape[:2]
    x, bm, cm = jnp.split(xbc_conv, [D_INNER, D_INNER + SSD_GROUPS * D_STATE], axis=-1)
    x = x.reshape(B, T, SSD_GROUPS, SSD_HEADS_PER_GROUP, SSD_HEAD_DIM)
    bm = bm.reshape(B, T, SSD_GROUPS, D_STATE)
    cm = cm.reshape(B, T, SSD_GROUPS, D_STATE)
    dt = jax.nn.softplus((dt_raw + p['dt_bias']).astype(F32)).reshape(B, T, SSD_GROUPS, SSD_HEADS_PER_GROUP)
    A = -jnp.exp(p['a_log'].astype(F32)).reshape(SSD_GROUPS, SSD_HEADS_PER_GROUP)
    return x, bm, cm, dt, A


def ssd_chunked(xs, dt, A, bm, cm):
    B, T = xs.shape[:2]
    nc = T // SSD_CHUNK

    def to_chunks(a):
        return jnp.moveaxis(a.reshape(B, nc, SSD_CHUNK, *a.shape[2:]), 1, 0)

    causal = jnp.tril(jnp.ones((SSD_CHUNK, SSD_CHUNK), dtype=bool))[None, :, :, None, None]

    def step(h, inp):
        xc, dtc, bc, cc = inp
        acum = jnp.cumsum(dtc * A, axis=1)
        seg = acum[:, :, None] - acum[:, None, :]
        lmat = jnp.exp(jnp.where(causal, seg, -jnp.inf))
        cb = jnp.einsum('btgn,bsgn->btsg', cc, bc, preferred_element_type=F32)
        y = jnp.einsum('btsg,btsgr,bsgr,bsgrp->btgrp', cb, lmat, dtc, xc)
        y = y + jnp.einsum('btgn,bgrpn,btgr->btgrp', cc, h, jnp.exp(acum))
        decay = jnp.exp(acum[:, -1:] - acum) * dtc
        h = h * jnp.exp(acum[:, -1])[..., None, None] + jnp.einsum('bsgn,bsgr,bsgrp->bgrpn', bc, decay, xc)
        return h, y

    h0 = jnp.zeros((B, SSD_GROUPS, SSD_HEADS_PER_GROUP, SSD_HEAD_DIM, D_STATE), F32)
    hT, ys = lax.scan(step, h0, (to_chunks(xs), to_chunks(dt), to_chunks(bm), to_chunks(cm)))
    return jnp.moveaxis(ys, 0, 1).reshape(xs.shape), hT


def ssd_recurrent(xs, dt, A, bm, cm, h0):
    def step(h, inp):
        xt, dtt, bt, ct = inp
        h = h * jnp.exp(dtt * A)[..., None, None] + jnp.einsum('bgr,bgrp,bgn->bgrpn', dtt, xt, bt)
        y = jnp.einsum('bgn,bgrpn->bgrp', ct, h)
        return h, y

    seq = (jnp.moveaxis(xs, 1, 0), jnp.moveaxis(dt, 1, 0), jnp.moveaxis(bm, 1, 0), jnp.moveaxis(cm, 1, 0))
    hT, ys = lax.scan(step, h0, seq)
    return jnp.moveaxis(ys, 0, 1), hT


def ssd_output(y, xs, z, p):
    B, T = z.shape[:2]
    d = p['d_skip'].astype(F32).reshape(SSD_GROUPS, SSD_HEADS_PER_GROUP, 1)
    y = (y + d * xs).reshape(B, T, D_INNER) * jax.nn.silu(z.astype(F32))
    yg = y.reshape(B, T, SSD_GROUPS, D_INNER // SSD_GROUPS)
    yg = yg * lax.rsqrt(jnp.mean(yg * yg, axis=-1, keepdims=True) + EPS)
    return (yg.reshape(B, T, D_INNER) * p['ssd_norm_g'].astype(F32)).astype(z.dtype)


def peer(h, p):
    n = h.shape[0]
    q = (h @ p['peer_wq']).reshape(n, PEER_HEADS, 2, PEER_HALF)
    s = jnp.einsum('nhcd,chkd->nhck', q, p['peer_keys'], preferred_element_type=F32)
    s1, i1 = lax.top_k(s[:, :, 0], PEER_TOPK)
    s2, i2 = lax.top_k(s[:, :, 1], PEER_TOPK)
    cand_s = (s1[..., :, None] + s2[..., None, :]).reshape(n, PEER_HEADS, PEER_TOPK * PEER_TOPK)
    cand_i = (i1[..., :, None] * PEER_N_KEYS + i2[..., None, :]).reshape(n, PEER_HEADS, PEER_TOPK * PEER_TOPK)
    top_s, pos = lax.top_k(cand_s, PEER_TOPK)
    eidx = jnp.take_along_axis(cand_i, pos, axis=-1)
    g = jax.nn.softmax(top_s, axis=-1)
    act = jax.nn.gelu(jnp.einsum('nd,nhkd->nhk', h, p['peer_u'][eidx], preferred_element_type=F32))
    return jnp.einsum('nhk,nhkd->nd', (g * act).astype(h.dtype), p['peer_v'][eidx])


def mixer_projection(x, p):
    return split_cols(rms_norm(x, p['norm1_g']) @ p['w_in'])


def merge_and_channel(x, y_ssd, y_att, g_a, g_b, p, blocked):
    merged = (jax.nn.sigmoid(g_a) * (y_ssd @ p['w_branch_a'])
              + jax.nn.sigmoid(g_b) * (y_att @ p['w_branch_b']))
    x = x + merged @ p['w_out']
    h = rms_norm(x, p['norm2_g'])
    B, T, _ = h.shape
    hf = h.reshape(B * T, D_MODEL)
    if blocked:
        out = lax.map(lambda hb: peer(hb, p), hf.reshape(-1, PEER_BLOCK, D_MODEL)).reshape(B * T, D_MODEL)
    else:
        out = peer(hf, p)
    return x + out.reshape(B, T, D_MODEL)


def layer_prompt(x, p, rel_bias):
    z, xbc, dt_raw, q_raw, k_raw, v_raw, iq_raw, ik, iw_raw, g_a, g_b = mixer_projection(x, p)
    B = x.shape[0]
    xpad = jnp.concatenate([jnp.zeros((B, CONV_W - 1, CONV_DIM), xbc.dtype), xbc], axis=1)
    conv_state = xpad[:, -(CONV_W - 1):]
    xs, bm, cm, dt, A = ssd_inputs(causal_dwconv(xpad, p['conv_w'], p['conv_b']), dt_raw, p)
    y, hT = ssd_chunked(xs, dt, A, bm, cm)
    y_ssd = ssd_output(y, xs, z, p)
    q, k, v, iq, iw = attn_inputs(q_raw, k_raw, v_raw, iq_raw, iw_raw, p)
    y_att = attention_prompt(q, k, v, iq, ik, iw, rel_bias)
    out = merge_and_channel(x, y_ssd, y_att, g_a, g_b, p, True)
    ssm_state = hT.reshape(B, SSD_HEADS, SSD_HEAD_DIM, D_STATE).astype(x.dtype)
    return out, (k, v, ik, ssm_state, conv_state)


def layer_sample(x, p, rel_bias, cache_k, cache_v, cache_ik, page_table, state_ssm, state_conv):
    z, xbc, dt_raw, q_raw, k_raw, v_raw, iq_raw, ik, iw_raw, g_a, g_b = mixer_projection(x, p)
    DB = x.shape[0]
    xcat = jnp.concatenate([state_conv.astype(xbc.dtype), xbc], axis=1)
    conv_state = xcat[:, -(CONV_W - 1):]
    xs, bm, cm, dt, A = ssd_inputs(causal_dwconv(xcat, p['conv_w'], p['conv_b']), dt_raw, p)
    h0 = state_ssm.astype(F32).reshape(DB, SSD_GROUPS, SSD_HEADS_PER_GROUP, SSD_HEAD_DIM, D_STATE)
    y, hT = ssd_recurrent(xs, dt, A, bm, cm, h0)
    y_ssd = ssd_output(y, xs, z, p)
    q, k, v, iq, iw = attn_inputs(q_raw, k_raw, v_raw, iq_raw, iw_raw, p)
    y_att = attention_sample(q, k, v, iq, ik, iw, rel_bias, cache_k, cache_v, cache_ik, page_table)
    out = merge_and_channel(x, y_ssd, y_att, g_a, g_b, p, False)
    ssm_state = hT.reshape(DB, SSD_HEADS, SSD_HEAD_DIM, D_STATE).astype(state_ssm.dtype)
    return out, (k, v, ik, ssm_state, conv_state)


def setup_inputs(seed: int = 0) -> dict:
    key = jax.random.key(seed)
    ks = jax.random.split(key, 32)
    n_pages = PAST_LEN // PAGE_SIZE
    n_pool = (DEC_BATCH * n_pages * 5) // 4

    def nrm(k, shape, scale):
        return scale * jax.random.normal(k, shape, F32)

    dt0 = jnp.exp(jax.random.uniform(ks[13], (DEPTH, SSD_HEADS), F32, math.log(1e-3), math.log(1e-1)))
    return {
        'x_prompt': nrm(ks[0], (BATCH, SEQ, D_MODEL), 1.0),
        'x_sample': nrm(ks[1], (DEC_BATCH, DEC_SEQ, D_MODEL), 1.0),
        'cache_k': nrm(ks[2], (DEPTH, n_pool, PAGE_SIZE, KV_HEADS, ATT_HEAD_DIM), 1.0),
        'cache_v': nrm(ks[3], (DEPTH, n_pool, PAGE_SIZE, KV_HEADS, ATT_HEAD_DIM), 1.0),
        'cache_idx_k': nrm(ks[4], (DEPTH, n_pool, PAGE_SIZE, IDX_DIM), 1.0),
        'state_ssm': nrm(ks[5], (DEPTH, DEC_BATCH, SSD_HEADS, SSD_HEAD_DIM, D_STATE), 0.1),
        'state_conv': nrm(ks[6], (DEPTH, DEC_BATCH, CONV_W - 1, CONV_DIM), 1.0),
        'page_table': jax.random.permutation(ks[7], n_pool)[:DEC_BATCH * n_pages]
                      .reshape(DEC_BATCH, n_pages).astype(jnp.int32),
        'rel_bias': nrm(ks[8], (NUM_BUCKETS, ATT_HEADS), 0.5),
        'norm1_g': 1.0 + nrm(ks[9], (DEPTH, D_MODEL), 0.02),
        'w_in': nrm(ks[10], (DEPTH, D_MODEL, IN_COLS), D_MODEL ** -0.5),
        'conv_w': nrm(ks[11], (DEPTH, CONV_W, CONV_DIM), CONV_W ** -0.5),
        'conv_b': nrm(ks[12], (DEPTH, CONV_DIM), 0.02),
        'dt_bias': dt0 + jnp.log(-jnp.expm1(-dt0)),
        'a_log': jnp.log(jax.random.uniform(ks[14], (DEPTH, SSD_HEADS), F32, 1.0, 16.0)),
        'd_skip': 1.0 + nrm(ks[15], (DEPTH, SSD_HEADS), 0.1),
        'ssd_norm_g': 1.0 + nrm(ks[16], (DEPTH, D_INNER), 0.02),
        'q_norm_g': 1.0 + nrm(ks[17], (DEPTH, ATT_HEAD_DIM), 0.02),
        'k_norm_g': 1.0 + nrm(ks[18], (DEPTH, ATT_HEAD_DIM), 0.02),
        'w_branch_a': nrm(ks[19], (DEPTH, D_INNER, D_MODEL), D_INNER ** -0.5),
        'w_branch_b': nrm(ks[20], (DEPTH, ATT_WIDTH, D_MODEL), ATT_WIDTH ** -0.5),
        'w_out': nrm(ks[21], (DEPTH, D_MODEL, D_MODEL), D_MODEL ** -0.5),
        'norm2_g': 1.0 + nrm(ks[22], (DEPTH, D_MODEL), 0.02),
        'peer_wq': nrm(ks[23], (DEPTH, D_MODEL, PEER_HEADS * PEER_KEY_DIM), D_MODEL ** -0.5),
        'peer_keys': nrm(ks[24], (DEPTH, 2, PEER_HEADS, PEER_N_KEYS, PEER_HALF), PEER_HALF ** -0.5),
        'peer_u': nrm(ks[25], (DEPTH, PEER_EXPERTS, D_MODEL), D_MODEL ** -0.5),
        'peer_v': nrm(ks[26], (DEPTH, PEER_EXPERTS, D_MODEL), PEER_HEADS ** -0.5),
    }


def reference(x_prompt, x_sample, cache_k, cache_v, cache_idx_k, state_ssm, state_conv, page_table,
              rel_bias, norm1_g, w_in, conv_w, conv_b, dt_bias, a_log, d_skip, ssd_norm_g,
              q_norm_g, k_norm_g, w_branch_a, w_branch_b, w_out, norm2_g,
              peer_wq, peer_keys, peer_u, peer_v):
    yp, ys = x_prompt, x_sample
    new_p, new_s = [], []
    for l in range(DEPTH):
        p = dict(norm1_g=norm1_g[l], w_in=w_in[l], conv_w=conv_w[l], conv_b=conv_b[l],
                 dt_bias=dt_bias[l], a_log=a_log[l], d_skip=d_skip[l], ssd_norm_g=ssd_norm_g[l],
                 q_norm_g=q_norm_g[l], k_norm_g=k_norm_g[l], w_branch_a=w_branch_a[l],
                 w_branch_b=w_branch_b[l], w_out=w_out[l], norm2_g=norm2_g[l],
                 peer_wq=peer_wq[l], peer_keys=peer_keys[l], peer_u=peer_u[l], peer_v=peer_v[l])
        yp, sp = layer_prompt(yp, p, rel_bias)
        ys, ss = layer_sample(ys, p, rel_bias, cache_k[l], cache_v[l], cache_idx_k[l], page_table,
                              state_ssm[l], state_conv[l])
        new_p.append(sp)
        new_s.append(ss)

    def stk(group, i):
        return jnp.stack([st[i] for st in group], axis=0)

    return (yp, ys,
            stk(new_p, 0), stk(new_p, 1), stk(new_p, 2), stk(new_p, 3), stk(new_p, 4),
            stk(new_s, 0), stk(new_s, 1), stk(new_s, 2), stk(new_s, 3), stk(new_s, 4))
```

```python
import functools
import math

import numpy as np
import jax
import jax.numpy as jnp
from jax import lax
from jax.experimental import pallas as pl
from jax.experimental.pallas import tpu as pltpu

F32 = jnp.float32
BF16 = jnp.bfloat16
I32 = jnp.int32

D_MODEL = 1024
PAGE_SIZE = 128
D_INNER = 2048
SSD_HEAD_DIM = 64
SSD_HEADS = 32
SSD_GROUPS = 8
D_STATE = 128
CONV_W = 4
CONV_DIM = D_INNER + 2 * SSD_GROUPS * D_STATE
SSD_CHUNK = 128
ATT_HEADS = 16
ATT_HEAD_DIM = 64
ATT_WIDTH = ATT_HEADS * ATT_HEAD_DIM
KV_HEADS = 4
KV_WIDTH = KV_HEADS * ATT_HEAD_DIM
IDX_HEADS = 8
IDX_DIM = 64
TOPK_KEYS_MAX = 256
Q_BLOCK = 128
NUM_BUCKETS = 32
MAX_DISTANCE = 128
PEER_HEADS = 8
PEER_N_KEYS = 128
PEER_EXPERTS = PEER_N_KEYS * PEER_N_KEYS
PEER_HALF = 128
PEER_TOPK = 16
EPS = 1e-6

LANES = 128
VMEM_LIMIT_BYTES = 56 * 1024 * 1024

COL_XBC = 0
COL_Z = COL_XBC + CONV_DIM
COL_Q = COL_Z + D_INNER
COL_K = COL_Q + ATT_WIDTH
COL_V = COL_K + KV_WIDTH
COL_IQ = COL_V + KV_WIDTH
COL_GA = COL_IQ + IDX_HEADS * IDX_DIM
COL_GB = COL_GA + D_MODEL
COL_SMALL = COL_GB + D_MODEL
SMALL_DT = 0
SMALL_IK = SSD_HEADS
SMALL_IW = SSD_HEADS + IDX_DIM
PROJ_COLS = COL_SMALL + LANES
PROJ_TN = 1152

INT_MIN = -2 ** 31
NEG_INF = float("-inf")


def _cparams(sem):
    return pltpu.CompilerParams(dimension_semantics=sem, vmem_limit_bytes=VMEM_LIMIT_BYTES)


def _split3(x):
    hi = x.astype(BF16)
    r1 = x - hi.astype(F32)
    mid = r1.astype(BF16)
    lo = (r1 - mid.astype(F32)).astype(BF16)
    return hi, mid, lo


def _dot(a, b):
    return jnp.dot(a, b, preferred_element_type=F32)


def _dot_nt(a, b):
    return lax.dot_general(a, b, (((1,), (1,)), ((), ())), preferred_element_type=F32)


def _dot_tn(a, b):
    return lax.dot_general(a, b, (((0,), (0,)), ((), ())), preferred_element_type=F32)


def _exact_dot(x, onehot_bf16):
    hi, mid, lo = _split3(x)
    return _dot(hi, onehot_bf16) + _dot(mid, onehot_bf16) + _dot(lo, onehot_bf16)


def _in_proj_kernel(x_ref, g_ref, w_ref, o_ref, xn_ref):
    @pl.when(pl.program_id(1) == 0)
    def _():
        x = x_ref[...]
        ms = jnp.mean(x * x, axis=-1, keepdims=True)
        xn_ref[...] = (x * lax.rsqrt(ms + EPS) * g_ref[...]).astype(BF16)

    o_ref[...] = _dot(xn_ref[...], w_ref[...])


def _in_proj(x2d, g, w_perm, tm):
    n = x2d.shape[0]
    return pl.pallas_call(
        _in_proj_kernel,
        grid=(n // tm, PROJ_COLS // PROJ_TN),
        in_specs=[pl.BlockSpec((tm, D_MODEL), lambda i, j: (i, 0)),
                  pl.BlockSpec((1, D_MODEL), lambda i, j: (0, 0)),
                  pl.BlockSpec((D_MODEL, PROJ_TN), lambda i, j: (0, j))],
        out_specs=pl.BlockSpec((tm, PROJ_TN), lambda i, j: (i, j)),
        out_shape=jax.ShapeDtypeStruct((n, PROJ_COLS), F32),
        scratch_shapes=[pltpu.VMEM((tm, D_MODEL), BF16)],
        compiler_params=_cparams(("parallel", "arbitrary")),
        name="in_proj",
    )(x2d, g.reshape(1, D_MODEL), w_perm)


def _permute_w_in(w_in):
    offs = np.cumsum([0, D_INNER, CONV_DIM, SSD_HEADS, ATT_WIDTH, KV_WIDTH, KV_WIDTH,
                      IDX_HEADS * IDX_DIM, IDX_DIM, IDX_HEADS, D_MODEL, D_MODEL])
    z, xbc, dt, q, k, v, iq, ik, iw, ga, gb = [w_in[:, offs[i]:offs[i + 1]] for i in range(11)]
    pad = jnp.zeros((D_MODEL, LANES - SSD_HEADS - IDX_DIM - IDX_HEADS), w_in.dtype)
    return jnp.concatenate([xbc, z, q, k, v, iq, ga, gb, dt, ik, iw, pad], axis=1).astype(BF16)


def _seg_indicator(width, seg):
    m = np.zeros((width, LANES), np.float32)
    m[np.arange(width), np.arange(width) // seg] = 1.0
    return m


def _half_placement():
    lo = np.zeros((KV_WIDTH, KV_HEADS * LANES), np.float32)
    hi = np.zeros((KV_WIDTH, KV_HEADS * LANES), np.float32)
    c = np.arange(KV_WIDTH)
    g, d = c // ATT_HEAD_DIM, c % ATT_HEAD_DIM
    lo[c, g * LANES + d] = 1.0
    hi[c, g * LANES + ATT_HEAD_DIM + d] = 1.0
    return lo, hi


def _head_rms(x, ind, ind_t, gain):
    sq = x * x
    hi = sq.astype(BF16)
    lo = (sq - hi.astype(F32)).astype(BF16)
    ss = _dot(hi, ind) + _dot(lo, ind)
    r = lax.rsqrt(ss * (1.0 / ATT_HEAD_DIM) + EPS)
    rb = _exact_dot(r, ind_t)
    return x * rb * gain


def _attn_prep_kernel(q_ref, kv_ref, sm_ref, gq_ref, gk_ref, indq_ref, indqt_ref, indk_ref, indkt_ref,
                      plo_ref, phi_ref,
                      qn_ref, kn_ref, v_ref, ik_ref, klo_ref, khi_ref, vlo_ref, vhi_ref,
                      ikb_ref, iqb_ref, iw_ref):
    qn = _head_rms(q_ref[...], indq_ref[...], indqt_ref[...], gq_ref[...])
    qn_ref[...] = (qn * (ATT_HEAD_DIM ** -0.5)).astype(BF16)
    kv = kv_ref[...]
    kn = _head_rms(kv[:, :KV_WIDTH], indk_ref[...], indkt_ref[...], gk_ref[...])
    v = kv[:, KV_WIDTH:]
    kn_ref[...] = kn
    v_ref[...] = v
    knb = kn.astype(BF16)
    vb = v.astype(BF16)
    for src, place_ref, dst_ref in ((knb, plo_ref, klo_ref), (knb, phi_ref, khi_ref),
                                    (vb, plo_ref, vlo_ref), (vb, phi_ref, vhi_ref)):
        placed = _dot(src, place_ref[...]).astype(BF16)
        for g in range(KV_HEADS):
            dst_ref[g] = placed[:, g * LANES:(g + 1) * LANES]
    sm = sm_ref[...]
    ik = sm[:, SMALL_IK:SMALL_IK + IDX_DIM]
    ik_ref[...] = ik
    ikb_ref[...] = ik.astype(BF16)
    iw_ref[...] = sm * (IDX_HEADS ** -0.5 * IDX_DIM ** -0.5)


def _attn_prep(proj, q_norm_g, k_norm_g, tm):
    n = proj.shape[0]
    gq = jnp.tile(q_norm_g, ATT_HEADS).reshape(1, ATT_WIDTH)
    gk = jnp.tile(k_norm_g, KV_HEADS).reshape(1, KV_WIDTH)
    indq = _seg_indicator(ATT_WIDTH, ATT_HEAD_DIM)
    indk = _seg_indicator(KV_WIDTH, ATT_HEAD_DIM)
    plo, phi = _half_placement()
    consts = [jnp.asarray(a, BF16) for a in (indq, indq.T, indk, indk.T, plo, phi)]

    def full(a):
        return pl.BlockSpec(a.shape, lambda i: (0,) * a.ndim)

    outs = [
        ((n, ATT_WIDTH), BF16), ((n, KV_WIDTH), F32), ((n, KV_WIDTH), F32), ((n, IDX_DIM), F32),
        ((KV_HEADS, n, LANES), BF16), ((KV_HEADS, n, LANES), BF16),
        ((KV_HEADS, n, LANES), BF16), ((KV_HEADS, n, LANES), BF16),
        ((n, IDX_DIM), BF16), ((n, IDX_HEADS * IDX_DIM), BF16), ((n, LANES), F32),
    ]

    def body(q_ref, kv_ref, sm_ref, iq_ref, *rest):
        (gq_ref, gk_ref, indq_ref, indqt_ref, indk_ref, indkt_ref, plo_ref, phi_ref,
         qn_ref, kn_ref, v_ref, ik_ref, klo_ref, khi_ref, vlo_ref, vhi_ref, ikb_ref, iqb_ref, iw_ref) = rest
        _attn_prep_kernel(q_ref, kv_ref, sm_ref, gq_ref, gk_ref, indq_ref, indqt_ref, indk_ref, indkt_ref,
                          plo_ref, phi_ref, qn_ref, kn_ref, v_ref, ik_ref, klo_ref, khi_ref, vlo_ref, vhi_ref,
                          ikb_ref, iqb_ref, iw_ref)
        iqb_ref[...] = iq_ref[...].astype(BF16)

    return pl.pallas_call(
        body,
        grid=(n // tm,),
        in_specs=[pl.BlockSpec((tm, ATT_WIDTH), lambda i: (i, COL_Q // ATT_WIDTH)),
                  pl.BlockSpec((tm, 2 * KV_WIDTH), lambda i: (i, COL_K // (2 * KV_WIDTH))),
                  pl.BlockSpec((tm, LANES), lambda i: (i, COL_SMALL // LANES)),
                  pl.BlockSpec((tm, IDX_HEADS * IDX_DIM), lambda i: (i, COL_IQ // (IDX_HEADS * IDX_DIM))),
                  full(gq), full(gk)] + [full(c) for c in consts],
        out_specs=[pl.BlockSpec((tm, s[1]), lambda i: (i, 0)) if len(s) == 2
                   else pl.BlockSpec((KV_HEADS, tm, LANES), lambda i: (0, i, 0)) for s, _ in outs],
        out_shape=[jax.ShapeDtypeStruct(s, d) for s, d in outs],
        compiler_params=_cparams(("parallel",)),
        name="attn_prep",
    )(proj, proj, proj, proj, gq, gk, *consts)


def _merge_kernel(ya_ref, yb_ref, ga_ref, gb_ref, x_ref, wa_ref, wb_ref, wo_ref, g2_ref, x1_ref, h_ref):
    a = _dot(ya_ref[...].astype(BF16), wa_ref[...])
    b = _dot(yb_ref[...].astype(BF16), wb_ref[...])
    merged = jax.nn.sigmoid(ga_ref[...]) * a + jax.nn.sigmoid(gb_ref[...]) * b
    x1 = x_ref[...] + _dot(merged.astype(BF16), wo_ref[...])
    x1_ref[...] = x1
    ms = jnp.mean(x1 * x1, axis=-1, keepdims=True)
    h_ref[...] = (x1 * lax.rsqrt(ms + EPS) * g2_ref[...]).astype(BF16)


def _merge(y_ssd, y_att, proj, x2d, wa, wb, wo, g2, tm):
    n = x2d.shape[0]

    def full(a):
        return pl.BlockSpec(a.shape, lambda i: (0,) * a.ndim)

    g2 = g2.reshape(1, D_MODEL)
    return pl.pallas_call(
        _merge_kernel,
        grid=(n // tm,),
        in_specs=[pl.BlockSpec((tm, D_INNER), lambda i: (i, 0)),
                  pl.BlockSpec((tm, ATT_WIDTH), lambda i: (i, 0)),
                  pl.BlockSpec((tm, D_MODEL), lambda i: (i, COL_GA // D_MODEL)),
                  pl.BlockSpec((tm, D_MODEL), lambda i: (i, COL_GB // D_MODEL)),
                  pl.BlockSpec((tm, D_MODEL), lambda i: (i, 0)),
                  full(wa), full(wb), full(wo), full(g2)],
        out_specs=[pl.BlockSpec((tm, D_MODEL), lambda i: (i, 0)),
                   pl.BlockSpec((tm, D_MODEL), lambda i: (i, 0))],
        out_shape=[jax.ShapeDtypeStruct((n, D_MODEL), F32), jax.ShapeDtypeStruct((n, D_MODEL), BF16)],
        compiler_params=_cparams(("parallel",)),
        name="merge",
    )(y_ssd, y_att, proj, proj, x2d, wa, wb, wo, g2)


def _top_values(cur, k):
    vals = []
    for _ in range(k):
        m = jnp.max(cur, axis=0, keepdims=True)
        vals.append(m)
        cur = jnp.where(cur >= m, NEG_INF, cur)
    return vals


def _peer_select_kernel(h_ref, wq_ref, keys_ref, s1_ref, c1_ref, s2_ref, e2_ref, tau_ref):
    q = _dot(h_ref[...], wq_ref[...]).astype(BF16)
    for hd in range(PEER_HEADS):
        base = hd * 2 * PEER_HALF
        s1 = _dot_nt(keys_ref[0, hd], q[:, base:base + PEER_HALF])
        s2 = _dot_nt(keys_ref[1, hd], q[:, base + PEER_HALF:base + 2 * PEER_HALF])
        v1 = _top_values(s1, PEER_TOPK)
        v2 = jnp.concatenate(_top_values(s2, PEER_TOPK), axis=0)
        cand = jnp.concatenate([v1[i] + v2 for i in range(PEER_TOPK)], axis=0)
        top = _top_values(cand, PEER_TOPK)
        zsum = sum(jnp.exp(t - top[0]) for t in top)
        s1_ref[hd] = s1
        s2_ref[hd] = s2
        c1_ref[hd] = jnp.exp(s1 - v1[0]) / zsum
        e2_ref[hd] = jnp.exp(s2 - v2[0:1])
        tau_ref[hd] = top[PEER_TOPK - 1]


def _peer_select(hb, wq, keys, tn):
    n = hb.shape[0]
    big = jax.ShapeDtypeStruct((PEER_HEADS, PEER_N_KEYS, n), F32)
    bspec = pl.BlockSpec((PEER_HEADS, PEER_N_KEYS, tn), lambda i: (0, 0, i))
    return pl.pallas_call(
        _peer_select_kernel,
        grid=(n // tn,),
        in_specs=[pl.BlockSpec((tn, D_MODEL), lambda i: (i, 0)),
                  pl.BlockSpec(wq.shape, lambda i: (0, 0)),
                  pl.BlockSpec(keys.shape, lambda i: (0, 0, 0, 0))],
        out_specs=[bspec, bspec, bspec, bspec, pl.BlockSpec((PEER_HEADS, 1, tn), lambda i: (0, 0, i))],
        out_shape=[big, big, big, big, jax.ShapeDtypeStruct((PEER_HEADS, 1, n), F32)],
        compiler_params=_cparams(("parallel",)),
        name="peer_select",
    )(hb, wq, keys)


def _gelu_tanh(x):
    return 0.5 * x * (1.0 + jnp.tanh(math.sqrt(2.0 / math.pi) * (x + 0.044715 * (x * x * x))))


PEER_EC = 512


def _peer_mix_kernel(h_ref, u_ref, vt_ref, s1_ref, c1_ref, s2_ref, e2_ref, tau_ref, x1_ref, y_ref, acc_ref):
    j = pl.program_id(1)

    @pl.when(j == 0)
    def _():
        acc_ref[...] = jnp.zeros_like(acc_ref)

    act = _gelu_tanh(_dot_nt(u_ref[...], h_ref[...]))
    rows = []
    for ii in range(PEER_EC // PEER_N_KEYS):
        i1 = j * (PEER_EC // PEER_N_KEYS) + ii
        w = jnp.zeros((PEER_N_KEYS, act.shape[1]), F32)
        for hd in range(PEER_HEADS):
            s1r = s1_ref[hd, pl.ds(i1, 1), :]
            c1r = c1_ref[hd, pl.ds(i1, 1), :]
            chosen = (s2_ref[hd] + s1r) >= tau_ref[hd]
            w = w + jnp.where(chosen, e2_ref[hd] * c1r, 0.0)
        rows.append((act[ii * PEER_N_KEYS:(ii + 1) * PEER_N_KEYS] * w).astype(BF16))
    gmat = jnp.concatenate(rows, axis=0)
    acc_ref[...] += _dot(vt_ref[...], gmat)

    @pl.when(j == pl.num_programs(1) - 1)
    def _():
        y_ref[...] = x1_ref[...] + acc_ref[...].T


def _peer_mix(hb, ub, vtb, sel, x1, tn):
    n = hb.shape[0]
    s1, c1, s2, e2, tau = sel
    bspec = pl.BlockSpec((PEER_HEADS, PEER_N_KEYS, tn), lambda i, j: (0, 0, i))
    return pl.pallas_call(
        _peer_mix_kernel,
        grid=(n // tn, PEER_EXPERTS // PEER_EC),
        in_specs=[pl.BlockSpec((tn, D_MODEL), lambda i, j: (i, 0)),
                  pl.BlockSpec((PEER_EC, D_MODEL), lambda i, j: (j, 0)),
                  pl.BlockSpec((D_MODEL, PEER_EC), lambda i, j: (0, j)),
                  bspec, bspec, bspec, bspec,
                  pl.BlockSpec((PEER_HEADS, 1, tn), lambda i, j: (0, 0, i)),
                  pl.BlockSpec((tn, D_MODEL), lambda i, j: (i, 0))],
        out_specs=pl.BlockSpec((tn, D_MODEL), lambda i, j: (i, 0)),
        out_shape=jax.ShapeDtypeStruct((n, D_MODEL), F32),
        scratch_shapes=[pltpu.VMEM((D_MODEL, tn), F32)],
        compiler_params=_cparams(("parallel", "arbitrary")),
        name="peer_mix",
    )(hb, ub, vtb, s1, c1, s2, e2, tau, x1)


HALO = 8
N_PAIRS = SSD_HEADS // 2
GROUP_W = D_INNER // SSD_GROUPS


def _softplus(x):
    return jnp.maximum(x, 0.0) + jnp.log1p(jnp.exp(-jnp.abs(x)))


def _silu(x):
    return x * jax.nn.sigmoid(x)


def _lane_bcast(col):
    return jnp.broadcast_to(col, (col.shape[0], LANES))


def _ssd_gate_norm(y, xs, z, dvec, gain):
    y = (y + dvec * xs) * _silu(z)
    parts = []
    for g in range(SSD_GROUPS):
        yg = y[:, g * GROUP_W:(g + 1) * GROUP_W]
        ms = jnp.mean(yg * yg, axis=-1, keepdims=True)
        parts.append(yg * lax.rsqrt(ms + EPS))
    return jnp.concatenate(parts, axis=1) * gain


def _ssd_prompt_kernel(xbc_ref, z_ref, sm_ref, cw_ref, cb_ref, dtb_ref, alog_ref, dvec_ref, gain_ref, tri_ref,
                       y_ref, hout_ref, ext_ref, h_ref, yacc_ref):
    c = pl.program_id(1)

    @pl.when(c == 0)
    def _():
        ext_ref[0:HALO, :] = jnp.zeros((HALO, CONV_DIM), F32)
        h_ref[...] = jnp.zeros_like(h_ref)

    x = xbc_ref[...]
    ext_ref[HALO:HALO + SSD_CHUNK, :] = x
    conv = cb_ref[...]
    for j in range(CONV_W):
        conv = conv + cw_ref[j:j + 1, :] * ext_ref[pl.ds(HALO - (CONV_W - 1) + j, SSD_CHUNK), :]
    ext_ref[0:HALO, :] = x[SSD_CHUNK - HALO:, :]
    xc = _silu(conv)
    xs = xc[:, :D_INNER]

    lane = lax.broadcasted_iota(I32, (SSD_CHUNK, LANES), 1)
    row = lax.broadcasted_iota(I32, (SSD_CHUNK, LANES), 0)
    dt = jnp.where(lane < SSD_HEADS, _softplus(sm_ref[...] + dtb_ref[...]), 0.0)
    da = dt * (-jnp.exp(alog_ref[...]))
    hi, mid, lo = _split3(da)
    tri = tri_ref[...]
    acum = _dot(tri, hi) + _dot(tri, mid) + _dot(tri, lo)
    acum_t = acum.T
    dt_t = dt.T
    alast = acum[SSD_CHUNK - 1:SSD_CHUNK, :]
    e_in = jnp.exp(acum)
    e_out = jnp.exp(alast - acum) * dt
    e_all = jnp.exp(alast)
    causal = row >= lane
    lo_half = lane < SSD_HEAD_DIM

    for g in range(SSD_GROUPS):
        bg = xc[:, D_INNER + g * D_STATE:D_INNER + (g + 1) * D_STATE].astype(BF16)
        cg = xc[:, D_INNER + SSD_GROUPS * D_STATE + g * D_STATE:
                D_INNER + SSD_GROUPS * D_STATE + (g + 1) * D_STATE].astype(BF16)
        cbm = _dot_nt(cg, bg)
        for pp in range(2):
            pair = 2 * g + pp
            r0 = 2 * pair
            xp = xs[:, r0 * SSD_HEAD_DIM:(r0 + 2) * SSD_HEAD_DIM]
            yp = jnp.zeros((SSD_CHUNK, LANES), F32)
            for hh in range(2):
                r = r0 + hh
                seg = _lane_bcast(acum[:, r:r + 1]) - acum_t[r:r + 1, :]
                lmat = jnp.exp(jnp.where(causal, seg, NEG_INF))
                m = (cbm * lmat * dt_t[r:r + 1, :]).astype(BF16)
                xm = jnp.where(lo_half == (hh == 0), xp, 0.0).astype(BF16)
                yp = yp + _dot(m, xm)
            hp = h_ref[pair]
            scale_in = jnp.where(lo_half, _lane_bcast(e_in[:, r0:r0 + 1]), _lane_bcast(e_in[:, r0 + 1:r0 + 2]))
            yp = yp + _dot_nt(cg, hp.astype(BF16)) * scale_in
            scale_out = jnp.where(lo_half, _lane_bcast(e_out[:, r0:r0 + 1]), _lane_bcast(e_out[:, r0 + 1:r0 + 2]))
            xd = (xp * scale_out).astype(BF16)
            hdec = jnp.where(row < SSD_HEAD_DIM, e_all[:, r0:r0 + 1], e_all[:, r0 + 1:r0 + 2])
            h_ref[pair] = hp * hdec + _dot_tn(xd, bg)
            yacc_ref[:, r0 * SSD_HEAD_DIM:(r0 + 2) * SSD_HEAD_DIM] = yp

    y_ref[...] = _ssd_gate_norm(yacc_ref[...], xs, z_ref[...], dvec_ref[...], gain_ref[...])

    @pl.when(c == pl.num_programs(1) - 1)
    def _():
        hout_ref[0] = h_ref[...]


def _pad_lanes(v):
    return jnp.zeros((1, LANES), F32).at[0, :v.shape[0]].set(v)


def _ssd_prompt(proj, bsz, conv_w, conv_b, dt_bias, a_log, d_skip, ssd_norm_g):
    n = proj.shape[0]
    nc = n // bsz // SSD_CHUNK
    tri = jnp.asarray(np.tril(np.ones((SSD_CHUNK, SSD_CHUNK), np.float32)), BF16)
    dvec = jnp.repeat(d_skip, SSD_HEAD_DIM).reshape(1, D_INNER)
    small = [conv_w, conv_b.reshape(1, CONV_DIM), _pad_lanes(dt_bias), _pad_lanes(a_log), dvec,
             ssd_norm_g.reshape(1, D_INNER), tri]

    def full(a):
        return pl.BlockSpec(a.shape, lambda b, c: (0,) * a.ndim)

    return pl.pallas_call(
        _ssd_prompt_kernel,
        grid=(bsz, nc),
        in_specs=[pl.BlockSpec((SSD_CHUNK, CONV_DIM), lambda b, c: (b * nc + c, 0)),
                  pl.BlockSpec((SSD_CHUNK, D_INNER), lambda b, c: (b * nc + c, COL_Z // D_INNER)),
                  pl.BlockSpec((SSD_CHUNK, LANES), lambda b, c: (b * nc + c, COL_SMALL // LANES))]
                 + [full(a) for a in small],
        out_specs=[pl.BlockSpec((SSD_CHUNK, D_INNER), lambda b, c: (b * nc + c, 0)),
                   pl.BlockSpec((1, N_PAIRS, LANES, D_STATE), lambda b, c: (b, 0, 0, 0))],
        out_shape=[jax.ShapeDtypeStruct((n, D_INNER), F32),
                   jax.ShapeDtypeStruct((bsz, N_PAIRS, LANES, D_STATE), F32)],
        scratch_shapes=[pltpu.VMEM((HALO + SSD_CHUNK, CONV_DIM), F32),
                        pltpu.VMEM((N_PAIRS, LANES, D_STATE), F32),
                        pltpu.VMEM((SSD_CHUNK, D_INNER), F32)],
        compiler_params=_cparams(("parallel", "arbitrary")),
        name="ssd_prompt",
    )(proj, proj, proj, *small)


def _t5_bucket_np(dist):
    n = np.maximum(dist, 0)
    max_exact = NUM_BUCKETS // 2
    nf = np.maximum(n, 1).astype(np.float32)
    ratio = np.log(nf / np.float32(max_exact)) / np.float32(math.log(MAX_DISTANCE / max_exact))
    large = max_exact + (ratio * np.float32(NUM_BUCKETS - max_exact)).astype(np.int32)
    large = np.minimum(large, NUM_BUCKETS - 1)
    return np.where(n < max_exact, n, large).astype(np.int32)


BIAS_TJ = 2048


def _bias_kernel(rel_t_ref, bucket_ref, o_ref):
    ids = lax.broadcasted_iota(I32, (NUM_BUCKETS, BIAS_TJ), 0)
    onehot = jnp.where(ids == bucket_ref[...], 1.0, 0.0).astype(BF16)
    o_ref[...] = _exact_dot(rel_t_ref[...], onehot)


def _bias_lookup(rel_bias, buckets):
    j = buckets.shape[0]
    return pl.pallas_call(
        _bias_kernel,
        grid=(j // BIAS_TJ,),
        in_specs=[pl.BlockSpec((ATT_HEADS, NUM_BUCKETS), lambda i: (0, 0)),
                  pl.BlockSpec((1, BIAS_TJ), lambda i: (0, i))],
        out_specs=pl.BlockSpec((ATT_HEADS, BIAS_TJ), lambda i: (0, i)),
        out_shape=jax.ShapeDtypeStruct((ATT_HEADS, j), F32),
        compiler_params=_cparams(("parallel",)),
        name="bias_lookup",
    )(rel_bias.T, jnp.asarray(buckets.reshape(1, j)))


def _prompt_bias_buckets():
    i = np.arange(Q_BLOCK)[:, None]
    j = np.arange(Q_BLOCK)[None, :]
    far = np.full((Q_BLOCK, Q_BLOCK), 2 * Q_BLOCK)
    return _t5_bucket_np(np.stack([far, i - j + Q_BLOCK, i - j]))


def _sortable_key(x):
    b = pltpu.bitcast(x, I32)
    return jnp.where(b < 0, b ^ 0x7FFFFFFF, b)


def _kth_largest_key(count_ge, k, shape):
    def body(it, cand):
        trial = cand + jnp.left_shift(jnp.int32(1), 31 - it)
        return jnp.where(count_ge(trial) >= k, trial, cand)
    return lax.fori_loop(0, 32, body, jnp.full(shape, INT_MIN, I32))


def _select_keys(n_sel, qb, iq_ref, iw_ref, ik_ref, triu_ref, key_ref, mask_ref):
    ntile = qb + 1
    row = lax.broadcasted_iota(I32, (Q_BLOCK, LANES), 0)
    lane = lax.broadcasted_iota(I32, (Q_BLOCK, LANES), 1)
    causal = lane <= row

    iw = iw_ref[...]
    iq = iq_ref[...]
    wcols = [_lane_bcast(iw[:, SMALL_IW + h:SMALL_IW + h + 1]) for h in range(IDX_HEADS)]
    iqs = [iq[:, h * IDX_DIM:(h + 1) * IDX_DIM] for h in range(IDX_HEADS)]

    def score_tile(j, _):
        ikt = ik_ref[pl.ds(pl.multiple_of(j * Q_BLOCK, Q_BLOCK), Q_BLOCK), :]
        sc = jnp.zeros((Q_BLOCK, LANES), F32)
        for h in range(IDX_HEADS):
            sc = sc + wcols[h] * jnp.maximum(_dot_nt(iqs[h], ikt), 0.0)
        sc = jnp.where(jnp.logical_or(j < qb, causal), sc, NEG_INF)
        key_ref[j] = _sortable_key(sc)
        return 0

    lax.fori_loop(0, ntile, score_tile, 0)

    def count_ge(trial):
        def body(j, acc):
            return acc + jnp.where(key_ref[j] >= trial, 1, 0)
        acc = lax.fori_loop(0, ntile, body, jnp.zeros((Q_BLOCK, LANES), I32))
        return jnp.sum(acc, axis=1, keepdims=True)

    tau = _kth_largest_key(count_ge, n_sel, (Q_BLOCK, 1))

    def count_gt(j, acc):
        return acc + jnp.where(key_ref[j] > tau, 1, 0)
    n_gt = jnp.sum(lax.fori_loop(0, ntile, count_gt, jnp.zeros((Q_BLOCK, LANES), I32)), axis=1, keepdims=True)
    need = (n_sel - n_gt).astype(F32)

    def mask_tile(j, seen):
        k = key_ref[j]
        eq = k == tau
        eqf = jnp.where(eq, 1.0, 0.0)
        pref = _dot(eqf.astype(BF16), triu_ref[...]) + seen
        sel = jnp.logical_or(k > tau, jnp.logical_and(eq, pref <= need))
        sel = jnp.logical_and(sel, jnp.logical_or(j < qb, causal))
        mask_ref[j] = jnp.where(sel, 0.0, NEG_INF)
        return seen + jnp.sum(eqf, axis=1, keepdims=True)

    lax.fori_loop(0, ntile, mask_tile, jnp.zeros((Q_BLOCK, 1), F32))


def _attn_prompt_kernel(n_sel, qp_ref, iq_ref, iw_ref, klo_ref, khi_ref, vlo_ref, vhi_ref, ik_ref, bt_ref, triu_ref,
                        y_ref, key_ref, mask_ref, lbuf_ref):
    qb = pl.program_id(1)
    pair = pl.program_id(2)
    ntile = qb + 1

    @pl.when(pair == 0)
    def _():
        _select_keys(n_sel, qb, iq_ref, iw_ref, ik_ref, triu_ref, key_ref, mask_ref)

    g = pair // (ATT_HEADS // KV_HEADS // 2)
    qp = qp_ref[...]
    out = jnp.zeros((Q_BLOCK, LANES), F32)
    for half, (k_ref, v_ref) in enumerate(((klo_ref, vlo_ref), (khi_ref, vhi_ref))):
        h = 2 * pair + half

        def logits_tile(j, mx, k_ref=k_ref, h=h):
            kt = k_ref[g, pl.ds(pl.multiple_of(j * Q_BLOCK, Q_BLOCK), Q_BLOCK), :]
            kind = jnp.clip(j - qb + 2, 0, 2)
            lg = _dot_nt(qp, kt) + bt_ref[h, kind] + mask_ref[j]
            lbuf_ref[j] = lg
            return jnp.maximum(mx, lg)

        mx = lax.fori_loop(0, ntile, logits_tile, jnp.full((Q_BLOCK, LANES), NEG_INF, F32))
        mrow = jnp.max(mx, axis=1, keepdims=True)

        def pv_tile(j, carry, v_ref=v_ref, mrow=mrow):
            lsum, acc = carry
            p = jnp.exp(lbuf_ref[j] - mrow)
            vt = v_ref[g, pl.ds(pl.multiple_of(j * Q_BLOCK, Q_BLOCK), Q_BLOCK), :]
            return lsum + p, acc + _dot(p.astype(BF16), vt)

        lsum, acc = lax.fori_loop(0, ntile, pv_tile,
                                  (jnp.zeros((Q_BLOCK, LANES), F32), jnp.zeros((Q_BLOCK, LANES), F32)))
        out = out + acc / jnp.sum(lsum, axis=1, keepdims=True)
    y_ref[...] = out


def _attn_prompt(prep, btile, bsz):
    qn, _, _, _, klo, khi, vlo, vhi, ikb, iqb, iw = prep
    n = qn.shape[0]
    t = n // bsz
    nqb = t // Q_BLOCK
    n_sel = min(TOPK_KEYS_MAX, t // 4)
    triu = jnp.asarray(np.triu(np.ones((LANES, LANES), np.float32)), BF16)
    kvspec = pl.BlockSpec((KV_HEADS, t, LANES), lambda b, q, p: (0, b, 0))
    return pl.pallas_call(
        functools.partial(_attn_prompt_kernel, n_sel),
        grid=(bsz, nqb, ATT_HEADS // 2),
        in_specs=[pl.BlockSpec((Q_BLOCK, LANES), lambda b, q, p: (b * nqb + q, p)),
                  pl.BlockSpec((Q_BLOCK, IDX_HEADS * IDX_DIM), lambda b, q, p: (b * nqb + q, 0)),
                  pl.BlockSpec((Q_BLOCK, LANES), lambda b, q, p: (b * nqb + q, 0)),
                  kvspec, kvspec, kvspec, kvspec,
                  pl.BlockSpec((t, IDX_DIM), lambda b, q, p: (b, 0)),
                  pl.BlockSpec(btile.shape, lambda b, q, p: (0, 0, 0, 0)),
                  pl.BlockSpec(triu.shape, lambda b, q, p: (0, 0))],
        out_specs=pl.BlockSpec((Q_BLOCK, LANES), lambda b, q, p: (b * nqb + q, p)),
        out_shape=jax.ShapeDtypeStruct((n, ATT_WIDTH), F32),
        scratch_shapes=[pltpu.VMEM((nqb, Q_BLOCK, LANES), I32),
                        pltpu.VMEM((nqb, Q_BLOCK, LANES), F32),
                        pltpu.VMEM((nqb, Q_BLOCK, LANES), F32)],
        compiler_params=_cparams(("parallel", "arbitrary", "arbitrary")),
        name="attn_prompt",
    )(qn, iqb, iw, klo, khi, vlo, vhi, ikb, btile, triu)


def _head_expand_indicator():
    m = np.zeros((LANES, D_INNER), np.float32)
    m[np.arange(D_INNER) // SSD_HEAD_DIM, np.arange(D_INNER)] = 1.0
    return m


def _ssd_step_kernel(xbc_ref, z_ref, sm_ref, sc_ref, hin_ref, cw_ref, cb_ref, dtb_ref, alog_ref, dvec_ref, gain_ref,
                     exp_ref, y_ref, hout_ref, cs_ref, xs_ref, bm_ref, cm_ref, dtx_t_ref, dec_t_ref, y_t_ref):
    b = pl.program_id(0)
    db = xbc_ref.shape[0]

    @pl.when(b == 0)
    def _():
        x = xbc_ref[...]
        sc = sc_ref[...]
        conv = cb_ref[...] + cw_ref[CONV_W - 1:CONV_W, :] * x
        for j in range(CONV_W - 1):
            conv = conv + cw_ref[j:j + 1, :] * sc[:, j * CONV_DIM:(j + 1) * CONV_DIM]
        cs_ref[:, :(CONV_W - 2) * CONV_DIM] = sc[:, CONV_DIM:]
        cs_ref[:, (CONV_W - 2) * CONV_DIM:] = x
        xc = _silu(conv)
        xs = xc[:, :D_INNER]
        lane = lax.broadcasted_iota(I32, (db, LANES), 1)
        dt = jnp.where(lane < SSD_HEADS, _softplus(sm_ref[...] + dtb_ref[...]), 0.0)
        dec = jnp.exp(dt * (-jnp.exp(alog_ref[...])))
        xs_ref[0:db, :] = xs
        bm_ref[0:db, :] = xc[:, D_INNER:D_INNER + SSD_GROUPS * D_STATE]
        cm_ref[0:db, :] = xc[:, D_INNER + SSD_GROUPS * D_STATE:]
        dtx_t_ref[:, 0:db] = (_exact_dot(dt, exp_ref[...]) * xs).T
        dec_t_ref[:, 0:db] = _exact_dot(dec, exp_ref[...]).T
        y_t_ref[...] = jnp.zeros_like(y_t_ref)

    lane = lax.broadcasted_iota(I32, (LANES, LANES), 1)
    mine = lane == b
    base = pl.multiple_of((b // 8) * 8, 8)
    my_row = lax.broadcasted_iota(I32, (8, SSD_GROUPS * D_STATE), 0) == (b % 8)
    bm_b = jnp.sum(jnp.where(my_row, bm_ref[pl.ds(base, 8), :], 0.0), axis=0, keepdims=True)
    cm_b = jnp.sum(jnp.where(my_row, cm_ref[pl.ds(base, 8), :], 0.0), axis=0, keepdims=True)
    for pair in range(N_PAIRS):
        g = pair // 2
        rows = slice(pair * LANES, (pair + 1) * LANES)
        xcol = jnp.sum(jnp.where(mine, dtx_t_ref[rows, :], 0.0), axis=1, keepdims=True)
        dcol = jnp.sum(jnp.where(mine, dec_t_ref[rows, :], 0.0), axis=1, keepdims=True)
        brow = bm_b[:, g * D_STATE:(g + 1) * D_STATE]
        crow = cm_b[:, g * D_STATE:(g + 1) * D_STATE]
        hnew = hin_ref[0, pair] * dcol + xcol * brow
        hout_ref[0, pair] = hnew
        ycol = jnp.sum(hnew * crow, axis=1, keepdims=True)
        y_t_ref[rows, :] = jnp.where(mine, ycol, y_t_ref[rows, :])

    @pl.when(b == pl.num_programs(0) - 1)
    def _():
        y = y_t_ref[...].T[0:db, :]
        y_ref[...] = _ssd_gate_norm(y, xs_ref[0:db, :], z_ref[...], dvec_ref[...], gain_ref[...])


def _ssd_step(proj, state_conv2d, state_ssm4d, conv_w, conv_b, dt_bias, a_log, d_skip, ssd_norm_g):
    db = proj.shape[0]
    assert db <= LANES and db % 8 == 0
    dvec = jnp.repeat(d_skip, SSD_HEAD_DIM).reshape(1, D_INNER)
    small = [conv_w, conv_b.reshape(1, CONV_DIM), _pad_lanes(dt_bias), _pad_lanes(a_log), dvec,
             ssd_norm_g.reshape(1, D_INNER), jnp.asarray(_head_expand_indicator(), BF16)]

    def full(a):
        return pl.BlockSpec(a.shape, lambda b: (0,) * a.ndim)

    cs_w = (CONV_W - 1) * CONV_DIM
    hspec = pl.BlockSpec((1, N_PAIRS, LANES, D_STATE), lambda b: (b, 0, 0, 0))
    return pl.pallas_call(
        _ssd_step_kernel,
        grid=(db,),
        in_specs=[pl.BlockSpec((db, CONV_DIM), lambda b: (0, 0)),
                  pl.BlockSpec((db, D_INNER), lambda b: (0, COL_Z // D_INNER)),
                  pl.BlockSpec((db, LANES), lambda b: (0, COL_SMALL // LANES)),
                  pl.BlockSpec((db, cs_w), lambda b: (0, 0)),
                  hspec] + [full(a) for a in small],
        out_specs=[pl.BlockSpec((db, D_INNER), lambda b: (0, 0)),
                   hspec,
                   pl.BlockSpec((db, cs_w), lambda b: (0, 0))],
        out_shape=[jax.ShapeDtypeStruct((db, D_INNER), F32),
                   jax.ShapeDtypeStruct(state_ssm4d.shape, F32),
                   jax.ShapeDtypeStruct((db, cs_w), F32)],
        scratch_shapes=[pltpu.VMEM((LANES, D_INNER), F32),
                        pltpu.VMEM((LANES, SSD_GROUPS * D_STATE), F32),
                        pltpu.VMEM((LANES, SSD_GROUPS * D_STATE), F32),
                        pltpu.VMEM((D_INNER, LANES), F32),
                        pltpu.VMEM((D_INNER, LANES), F32),
                        pltpu.VMEM((D_INNER, LANES), F32)],
        compiler_params=_cparams(("arbitrary",)),
        name="ssd_step",
    )(proj, proj, proj, state_conv2d, state_ssm4d, *small)


def _page_copies(pt_ref, seq, cache_ref, buf_ref, slot, sem_ref, n_pages):
    return [pltpu.make_async_copy(cache_ref.at[pt_ref[seq, p]], buf_ref.at[slot, p], sem_ref.at[slot])
            for p in range(n_pages)]


def _paged_fetch(pt_ref, streams, n_pages):
    b = pl.program_id(0)
    slot = b % 2

    @pl.when(b == 0)
    def _():
        for cache_ref, buf_ref, sem_ref in streams:
            for cp in _page_copies(pt_ref, 0, cache_ref, buf_ref, 0, sem_ref, n_pages):
                cp.start()

    @pl.when(b + 1 < pl.num_programs(0))
    def _():
        for cache_ref, buf_ref, sem_ref in streams:
            for cp in _page_copies(pt_ref, b + 1, cache_ref, buf_ref, 1 - slot, sem_ref, n_pages):
                cp.start()

    for cache_ref, buf_ref, sem_ref in streams:
        for cp in _page_copies(pt_ref, b, cache_ref, buf_ref, slot, sem_ref, n_pages):
            cp.wait()
    return slot


def _dec_score_kernel(n_pages, pt_ref, iq_ref, w_ref, iknew_ref, cache_ref, o_ref, buf_ref, sem_ref):
    b = pl.program_id(0)
    slot = _paged_fetch(pt_ref, [(cache_ref, buf_ref, sem_ref)], n_pages)
    past = n_pages * PAGE_SIZE
    ik = buf_ref[slot].reshape(past, IDX_DIM).astype(BF16)
    iq = iq_ref[...].astype(BF16)
    w = w_ref[...]
    sc = jnp.sum(w * jnp.maximum(_dot_nt(iq, ik), 0.0), axis=0, keepdims=True)
    iknew = iknew_ref[pl.ds(b, 1), :].astype(BF16).astype(F32)
    s_new = jnp.sum(iq.astype(F32) * iknew, axis=1, keepdims=True)
    sc_new = jnp.sum(w * jnp.maximum(s_new, 0.0), axis=0, keepdims=True)
    lane = lax.broadcasted_iota(I32, (1, LANES), 1)
    o_ref[0, :, 0:past] = sc
    o_ref[0, :, past:past + LANES] = jnp.where(lane == 0, sc_new, NEG_INF)


def _dec_scores(page_table, cache_ik, iq8, w8, ikb_new):
    db, n_pages = page_table.shape
    s_pad = (n_pages + 1) * PAGE_SIZE
    grid_spec = pltpu.PrefetchScalarGridSpec(
        num_scalar_prefetch=1,
        grid=(db,),
        in_specs=[pl.BlockSpec((IDX_HEADS, IDX_DIM), lambda b, pt: (b, 0)),
                  pl.BlockSpec((IDX_HEADS, 1), lambda b, pt: (b, 0)),
                  pl.BlockSpec(ikb_new.shape, lambda b, pt: (0, 0)),
                  pl.BlockSpec(memory_space=pl.ANY)],
        out_specs=pl.BlockSpec((1, 1, s_pad), lambda b, pt: (b, 0, 0)),
        scratch_shapes=[pltpu.VMEM((2, n_pages, PAGE_SIZE, IDX_DIM), F32),
                        pltpu.SemaphoreType.DMA((2,))],
    )
    return pl.pallas_call(
        functools.partial(_dec_score_kernel, n_pages),
        grid_spec=grid_spec,
        out_shape=jax.ShapeDtypeStruct((db, 1, s_pad), F32),
        compiler_params=_cparams(("arbitrary",)),
        name="dec_scores",
    )(page_table, iq8, w8, ikb_new, cache_ik)


def _dec_select_kernel(n_sel, sc_ref, triu_ref, mask_ref, key_ref):
    db, s_pad = sc_ref.shape
    ntile = s_pad // LANES
    key_ref[...] = _sortable_key(sc_ref[...])

    def count_ge(trial):
        acc = jnp.zeros((db, LANES), I32)
        for j in range(ntile):
            acc = acc + jnp.where(key_ref[:, j * LANES:(j + 1) * LANES] >= trial, 1, 0)
        return jnp.sum(acc, axis=1, keepdims=True)

    tau = _kth_largest_key(count_ge, n_sel, (db, 1))
    acc = jnp.zeros((db, LANES), I32)
    for j in range(ntile):
        acc = acc + jnp.where(key_ref[:, j * LANES:(j + 1) * LANES] > tau, 1, 0)
    need = (n_sel - jnp.sum(acc, axis=1, keepdims=True)).astype(F32)
    seen = jnp.zeros((db, 1), F32)
    for j in range(ntile):
        k = key_ref[:, j * LANES:(j + 1) * LANES]
        eq = k == tau
        pref = _dot(jnp.where(eq, 1.0, 0.0).astype(BF16), triu_ref[...]) + seen
        sel = jnp.logical_or(k > tau, jnp.logical_and(eq, pref <= need))
        mask_ref[:, j * LANES:(j + 1) * LANES] = jnp.where(sel, 0.0, NEG_INF)
        seen = pref[:, LANES - 1:LANES]


def _dec_select(scores2d, n_sel):
    db, s_pad = scores2d.shape
    triu = jnp.asarray(np.triu(np.ones((LANES, LANES), np.float32)), BF16)
    return pl.pallas_call(
        functools.partial(_dec_select_kernel, n_sel),
        grid=(1,),
        in_specs=[pl.BlockSpec((db, s_pad), lambda i: (0, 0)),
                  pl.BlockSpec(triu.shape, lambda i: (0, 0))],
        out_specs=pl.BlockSpec((db, s_pad), lambda i: (0, 0)),
        out_shape=jax.ShapeDtypeStruct((db, s_pad), F32),
        scratch_shapes=[pltpu.VMEM((db, s_pad), I32)],
        compiler_params=_cparams(("arbitrary",)),
        name="dec_select",
    )(scores2d, triu)


def _dec_attn_kernel(n_pages, pt_ref, q_ref, mask_ref, bias_ref, knew_ref, vnew_ref, ck_ref, cv_ref,
                     y_ref, kbuf_ref, vbuf_ref, ksem_ref, vsem_ref):
    b = pl.program_id(0)
    slot = _paged_fetch(pt_ref, [(ck_ref, kbuf_ref, ksem_ref), (cv_ref, vbuf_ref, vsem_ref)], n_pages)
    past = n_pages * PAGE_SIZE
    kb = kbuf_ref[slot].reshape(past, KV_WIDTH).astype(BF16)
    vb = vbuf_ref[slot].reshape(past, KV_WIDTH).astype(BF16)
    q = q_ref[...]
    hrow = lax.broadcasted_iota(I32, (ATT_HEADS, KV_WIDTH), 0)
    lane = lax.broadcasted_iota(I32, (ATT_HEADS, KV_WIDTH), 1)
    own = (hrow // (ATT_HEADS // KV_HEADS)) == (lane // ATT_HEAD_DIM)
    qbd = jnp.where(own, jnp.concatenate([q] * KV_HEADS, axis=1), jnp.zeros_like(q[:, :1]))
    knew = knew_ref[pl.ds(b, 1), :].astype(BF16).astype(F32)
    l_new = jnp.sum(qbd.astype(F32) * knew, axis=1, keepdims=True)
    vnew = vnew_ref[pl.ds(b, 1), :].astype(BF16).astype(F32)
    mask = mask_ref[0]
    lg = _dot_nt(qbd, kb) + bias_ref[:, 0:past] + mask[:, 0:past]
    tl = lax.broadcasted_iota(I32, (ATT_HEADS, LANES), 1)
    lg_new = jnp.where(tl == 0, l_new, 0.0) + bias_ref[:, past:past + LANES] + mask[:, past:past + LANES]
    m = jnp.maximum(jnp.max(lg, axis=1, keepdims=True), jnp.max(lg_new, axis=1, keepdims=True))
    p = jnp.exp(lg - m)
    p_new = jnp.exp(lg_new - m)
    denom = jnp.sum(p, axis=1, keepdims=True) + jnp.sum(p_new, axis=1, keepdims=True)
    pn = p_new[:, 0:1].astype(BF16).astype(F32)
    out = (_dot(p.astype(BF16), vb) + pn * vnew) / denom
    r16 = lax.broadcasted_iota(I32, (ATT_HEADS, ATT_HEAD_DIM), 0) // (ATT_HEADS // KV_HEADS)
    y = jnp.zeros((ATT_HEADS, ATT_HEAD_DIM), F32)
    for g in range(KV_HEADS):
        y = jnp.where(r16 == g, out[:, g * ATT_HEAD_DIM:(g + 1) * ATT_HEAD_DIM], y)
    y_ref[...] = y


def _dec_attn(page_table, q16, mask3d, bias_dec, kn_new, v_new, cache_k3d, cache_v3d):
    db, n_pages = page_table.shape
    s_pad = (n_pages + 1) * PAGE_SIZE
    grid_spec = pltpu.PrefetchScalarGridSpec(
        num_scalar_prefetch=1,
        grid=(db,),
        in_specs=[pl.BlockSpec((ATT_HEADS, ATT_HEAD_DIM), lambda b, pt: (b, 0)),
                  pl.BlockSpec((1, 1, s_pad), lambda b, pt: (b, 0, 0)),
                  pl.BlockSpec(bias_dec.shape, lambda b, pt: (0, 0)),
                  pl.BlockSpec(kn_new.shape, lambda b, pt: (0, 0)),
                  pl.BlockSpec(v_new.shape, lambda b, pt: (0, 0)),
                  pl.BlockSpec(memory_space=pl.ANY),
                  pl.BlockSpec(memory_space=pl.ANY)],
        out_specs=pl.BlockSpec((ATT_HEADS, ATT_HEAD_DIM), lambda b, pt: (b, 0)),
        scratch_shapes=[pltpu.VMEM((2, n_pages, PAGE_SIZE, KV_WIDTH), F32),
                        pltpu.VMEM((2, n_pages, PAGE_SIZE, KV_WIDTH), F32),
                        pltpu.SemaphoreType.DMA((2,)),
                        pltpu.SemaphoreType.DMA((2,))],
    )
    return pl.pallas_call(
        functools.partial(_dec_attn_kernel, n_pages),
        grid_spec=grid_spec,
        out_shape=jax.ShapeDtypeStruct((db * ATT_HEADS, ATT_HEAD_DIM), F32),
        compiler_params=_cparams(("arbitrary",)),
        name="dec_attn",
    )(page_table, q16, mask3d, bias_dec, kn_new, v_new, cache_k3d, cache_v3d)


def _channel_mixer(hb, x1, peer_wq, peer_keys, ub, vtb, tn):
    sel = _peer_select(hb, peer_wq, peer_keys, tn)
    return _peer_mix(hb, ub, vtb, sel, x1, tn)


def kernel(x_prompt, x_sample, cache_k, cache_v, cache_idx_k, state_ssm, state_conv, page_table, rel_bias, norm1_g, w_in, conv_w, conv_b, dt_bias, a_log, d_skip, ssd_norm_g, q_norm_g, k_norm_g, w_branch_a, w_branch_b, w_out, norm2_g, peer_wq, peer_keys, peer_u, peer_v):
    depth = w_in.shape[0]
    bsz, t, _ = x_prompt.shape
    db, ds, _ = x_sample.shape
    assert depth == 1 and ds == 1, "single layer, one new token per sequence"
    n = bsz * t
    n_pages = page_table.shape[1]
    past = n_pages * PAGE_SIZE
    l = 0

    w_perm = _permute_w_in(w_in[l])
    wa, wb, wo = w_branch_a[l].astype(BF16), w_branch_b[l].astype(BF16), w_out[l].astype(BF16)
    wq, keys = peer_wq[l].astype(BF16), peer_keys[l].astype(BF16)
    ub, vtb = peer_u[l].astype(BF16), peer_v[l].T.astype(BF16)
    ssd_w = (conv_w[l], conv_b[l], dt_bias[l], a_log[l], d_skip[l], ssd_norm_g[l])

    dec_dist = np.concatenate([past - np.arange(past), np.zeros(PAGE_SIZE, np.int64)])
    n_prompt_b = 3 * Q_BLOCK * Q_BLOCK
    buckets = np.concatenate([_prompt_bias_buckets().reshape(-1), _t5_bucket_np(dec_dist)])
    pad = (-buckets.shape[0]) % BIAS_TJ
    bias_all = _bias_lookup(rel_bias, np.concatenate([buckets, np.zeros(pad, np.int32)]))
    btile = bias_all[:, :n_prompt_b].reshape(ATT_HEADS, 3, Q_BLOCK, Q_BLOCK)
    bias_dec = bias_all[:, n_prompt_b:n_prompt_b + past + PAGE_SIZE]

    xp = x_prompt.reshape(n, D_MODEL)
    proj = _in_proj(xp, norm1_g[l], w_perm, min(1024, n))
    y_ssd, h_t = _ssd_prompt(proj, bsz, *ssd_w)
    prep = _attn_prep(proj, q_norm_g[l], k_norm_g[l], min(512, n))
    y_att = _attn_prompt(prep, btile, bsz)
    x1, hb = _merge(y_ssd, y_att, proj, xp, wa, wb, wo, norm2_g[l], min(512, n))
    y_prompt = _channel_mixer(hb, x1, wq, keys, ub, vtb, min(512, n))

    k_prompt = prep[1].reshape(1, bsz, t, KV_HEADS, ATT_HEAD_DIM)
    v_prompt = prep[2].reshape(1, bsz, t, KV_HEADS, ATT_HEAD_DIM)
    ik_prompt = prep[3].reshape(1, bsz, t, IDX_DIM)
    ssm_prompt = h_t.reshape(1, bsz, SSD_HEADS, SSD_HEAD_DIM, D_STATE)
    conv_prompt = proj[:, :CONV_DIM].reshape(bsz, t, CONV_DIM)[:, t - (CONV_W - 1):][None]

    xs_ = x_sample.reshape(db, D_MODEL)
    proj_s = _in_proj(xs_, norm1_g[l], w_perm, db)
    y_ssd_s, h_s, conv_s = _ssd_step(proj_s, state_conv[l].reshape(db, (CONV_W - 1) * CONV_DIM),
                                     state_ssm[l].reshape(db, N_PAIRS, LANES, D_STATE), *ssd_w)
    prep_s = _attn_prep(proj_s, q_norm_g[l], k_norm_g[l], db)
    qn_s, kn_s, v_s, ik_s, _, _, _, _, _, _, iw_s = prep_s
    iq8 = proj_s[:, COL_IQ:COL_IQ + IDX_HEADS * IDX_DIM].reshape(db * IDX_HEADS, IDX_DIM)
    w8 = iw_s[:, SMALL_IW:SMALL_IW + IDX_HEADS].reshape(db * IDX_HEADS, 1)
    scores = _dec_scores(page_table, cache_idx_k[l], iq8, w8, ik_s)
    n_sel = min(TOPK_KEYS_MAX, (past + ds) // 4)
    mask = _dec_select(scores.reshape(db, past + PAGE_SIZE), n_sel)
    y_att_s = _dec_attn(page_table, qn_s.reshape(db * ATT_HEADS, ATT_HEAD_DIM),
                        mask.reshape(db, 1, past + PAGE_SIZE), bias_dec, kn_s, v_s,
                        cache_k[l].reshape(-1, PAGE_SIZE, KV_WIDTH), cache_v[l].reshape(-1, PAGE_SIZE, KV_WIDTH))
    x1_s, hb_s = _merge(y_ssd_s, y_att_s.reshape(db, ATT_WIDTH), proj_s, xs_, wa, wb, wo, norm2_g[l], db)
    y_sample = _channel_mixer(hb_s, x1_s, wq, keys, ub, vtb, db)

    return (y_prompt.reshape(bsz, t, D_MODEL), y_sample.reshape(db, ds, D_MODEL),
            k_prompt, v_prompt, ik_prompt, ssm_prompt, conv_prompt,
            kn_s.reshape(1, db, ds, KV_HEADS, ATT_HEAD_DIM), v_s.reshape(1, db, ds, KV_HEADS, ATT_HEAD_DIM),
            ik_s.reshape(1, db, ds, IDX_DIM),
            h_s.reshape(1, db, SSD_HEADS, SSD_HEAD_DIM, D_STATE),
            conv_s.reshape(1, db, CONV_W - 1, CONV_DIM))
```

```python
import functools
import math

import numpy as np
import jax
import jax.numpy as jnp
from jax import lax
from jax.experimental import pallas as pl
from jax.experimental.pallas import tpu as pltpu

F32 = jnp.float32
BF16 = jnp.bfloat16
I32 = jnp.int32

D_MODEL = 1024
PAGE_SIZE = 128
D_INNER = 2048
SSD_HEAD_DIM = 64
SSD_HEADS = 32
SSD_GROUPS = 8
D_STATE = 128
CONV_W = 4
CONV_DIM = D_INNER + 2 * SSD_GROUPS * D_STATE
SSD_CHUNK = 128
ATT_HEADS = 16
ATT_HEAD_DIM = 64
ATT_WIDTH = ATT_HEADS * ATT_HEAD_DIM
KV_HEADS = 4
KV_WIDTH = KV_HEADS * ATT_HEAD_DIM
IDX_HEADS = 8
IDX_DIM = 64
TOPK_KEYS_MAX = 256
Q_BLOCK = 128
NUM_BUCKETS = 32
MAX_DISTANCE = 128
PEER_HEADS = 8
PEER_N_KEYS = 128
PEER_EXPERTS = PEER_N_KEYS * PEER_N_KEYS
PEER_HALF = 128
PEER_TOPK = 16
EPS = 1e-6

LANES = 128
VMEM_LIMIT_BYTES = 56 * 1024 * 1024

COL_XBC = 0
COL_Z = COL_XBC + CONV_DIM
COL_Q = COL_Z + D_INNER
COL_K = COL_Q + ATT_WIDTH
COL_V = COL_K + KV_WIDTH
COL_IQ = COL_V + KV_WIDTH
COL_GA = COL_IQ + IDX_HEADS * IDX_DIM
COL_GB = COL_GA + D_MODEL
COL_SMALL = COL_GB + D_MODEL
SMALL_DT = 0
SMALL_IK = SSD_HEADS
SMALL_IW = SSD_HEADS + IDX_DIM
PROJ_COLS = COL_SMALL + LANES
PROJ_TN = 1152

INT_MIN = -2 ** 31
NEG_INF = float("-inf")


def _cparams(sem):
    return pltpu.CompilerParams(dimension_semantics=sem, vmem_limit_bytes=VMEM_LIMIT_BYTES)


def _split3(x):
    hi = x.astype(BF16)
    r1 = x - hi.astype(F32)
    mid = r1.astype(BF16)
    lo = (r1 - mid.astype(F32)).astype(BF16)
    return hi, mid, lo


def _dot(a, b):
    return jnp.dot(a, b, preferred_element_type=F32)


def _dot_nt(a, b):
    return lax.dot_general(a, b, (((1,), (1,)), ((), ())), preferred_element_type=F32)


def _dot_tn(a, b):
    return lax.dot_general(a, b, (((0,), (0,)), ((), ())), preferred_element_type=F32)


def _exact_dot(x, onehot_bf16):
    hi, mid, lo = _split3(x)
    return _dot(hi, onehot_bf16) + _dot(mid, onehot_bf16) + _dot(lo, onehot_bf16)


def _in_proj_kernel(x_ref, g_ref, w_ref, o_ref, xn_ref):
    @pl.when(pl.program_id(1) == 0)
    def _():
        x = x_ref[...]
        ms = jnp.mean(x * x, axis=-1, keepdims=True)
        xn_ref[...] = (x * lax.rsqrt(ms + EPS) * g_ref[...]).astype(BF16)

    o_ref[...] = _dot(xn_ref[...], w_ref[...])


def _in_proj(x2d, g, w_perm, tm):
    n = x2d.shape[0]
    return pl.pallas_call(
        _in_proj_kernel,
        grid=(n // tm, PROJ_COLS // PROJ_TN),
        in_specs=[pl.BlockSpec((tm, D_MODEL), lambda i, j: (i, 0)),
                  pl.BlockSpec((1, D_MODEL), lambda i, j: (0, 0)),
                  pl.BlockSpec((D_MODEL, PROJ_TN), lambda i, j: (0, j))],
        out_specs=pl.BlockSpec((tm, PROJ_TN), lambda i, j: (i, j)),
        out_shape=jax.ShapeDtypeStruct((n, PROJ_COLS), F32),
        scratch_shapes=[pltpu.VMEM((tm, D_MODEL), BF16)],
        compiler_params=_cparams(("parallel", "arbitrary")),
        name="in_proj",
    )(x2d, g.reshape(1, D_MODEL), w_perm)


def _permute_w_in(w_in):
    offs = np.cumsum([0, D_INNER, CONV_DIM, SSD_HEADS, ATT_WIDTH, KV_WIDTH, KV_WIDTH,
                      IDX_HEADS * IDX_DIM, IDX_DIM, IDX_HEADS, D_MODEL, D_MODEL])
    z, xbc, dt, q, k, v, iq, ik, iw, ga, gb = [w_in[:, offs[i]:offs[i + 1]] for i in range(11)]
    pad = jnp.zeros((D_MODEL, LANES - SSD_HEADS - IDX_DIM - IDX_HEADS), w_in.dtype)
    return jnp.concatenate([xbc, z, q, k, v, iq, ga, gb, dt, ik, iw, pad], axis=1).astype(BF16)


def _seg_indicator(width, seg):
    m = np.zeros((width, LANES), np.float32)
    m[np.arange(width), np.arange(width) // seg] = 1.0
    return m


def _half_placement():
    lo = np.zeros((KV_WIDTH, KV_HEADS * LANES), np.float32)
    hi = np.zeros((KV_WIDTH, KV_HEADS * LANES), np.float32)
    c = np.arange(KV_WIDTH)
    g, d = c // ATT_HEAD_DIM, c % ATT_HEAD_DIM
    lo[c, g * LANES + d] = 1.0
    hi[c, g * LANES + ATT_HEAD_DIM + d] = 1.0
    return lo, hi


def _head_rms(x, ind, ind_t, gain):
    sq = x * x
    hi = sq.astype(BF16)
    lo = (sq - hi.astype(F32)).astype(BF16)
    ss = _dot(hi, ind) + _dot(lo, ind)
    r = lax.rsqrt(ss * (1.0 / ATT_HEAD_DIM) + EPS)
    rb = _exact_dot(r, ind_t)
    return x * rb * gain


def _attn_prep_kernel(q_ref, kv_ref, sm_ref, gq_ref, gk_ref, indq_ref, indqt_ref, indk_ref, indkt_ref,
                      plo_ref, phi_ref,
                      qn_ref, kn_ref, v_ref, ik_ref, klo_ref, khi_ref, vlo_ref, vhi_ref,
                      ikb_ref, iqb_ref, iw_ref):
    qn = _head_rms(q_ref[...], indq_ref[...], indqt_ref[...], gq_ref[...])
    qn_ref[...] = (qn * (ATT_HEAD_DIM ** -0.5)).astype(BF16)
    kv = kv_ref[...]
    kn = _head_rms(kv[:, :KV_WIDTH], indk_ref[...], indkt_ref[...], gk_ref[...])
    v = kv[:, KV_WIDTH:]
    kn_ref[...] = kn
    v_ref[...] = v
    knb = kn.astype(BF16)
    vb = v.astype(BF16)
    for src, place_ref, dst_ref in ((knb, plo_ref, klo_ref), (knb, phi_ref, khi_ref),
                                    (vb, plo_ref, vlo_ref), (vb, phi_ref, vhi_ref)):
        placed = _dot(src, place_ref[...]).astype(BF16)
        for g in range(KV_HEADS):
            dst_ref[g] = placed[:, g * LANES:(g + 1) * LANES]
    sm = sm_ref[...]
    ik = sm[:, SMALL_IK:SMALL_IK + IDX_DIM]
    ik_ref[...] = ik
    ikb_ref[...] = ik.astype(BF16)
    iw_ref[...] = sm * (IDX_HEADS ** -0.5 * IDX_DIM ** -0.5)


def _attn_prep(proj, q_norm_g, k_norm_g, tm):
    n = proj.shape[0]
    gq = jnp.tile(q_norm_g, ATT_HEADS).reshape(1, ATT_WIDTH)
    gk = jnp.tile(k_norm_g, KV_HEADS).reshape(1, KV_WIDTH)
    indq = _seg_indicator(ATT_WIDTH, ATT_HEAD_DIM)
    indk = _seg_indicator(KV_WIDTH, ATT_HEAD_DIM)
    plo, phi = _half_placement()
    consts = [jnp.asarray(a, BF16) for a in (indq, indq.T, indk, indk.T, plo, phi)]

    def full(a):
        return pl.BlockSpec(a.shape, lambda i: (0,) * a.ndim)

    outs = [
        ((n, ATT_WIDTH), BF16), ((n, KV_WIDTH), F32), ((n, KV_WIDTH), F32), ((n, IDX_DIM), F32),
        ((KV_HEADS, n, LANES), BF16), ((KV_HEADS, n, LANES), BF16),
        ((KV_HEADS, n, LANES), BF16), ((KV_HEADS, n, LANES), BF16),
        ((n, IDX_DIM), BF16), ((n, IDX_HEADS * IDX_DIM), BF16), ((n, LANES), F32),
    ]

    def body(q_ref, kv_ref, sm_ref, iq_ref, *rest):
        (gq_ref, gk_ref, indq_ref, indqt_ref, indk_ref, indkt_ref, plo_ref, phi_ref,
         qn_ref, kn_ref, v_ref, ik_ref, klo_ref, khi_ref, vlo_ref, vhi_ref, ikb_ref, iqb_ref, iw_ref) = rest
        _attn_prep_kernel(q_ref, kv_ref, sm_ref, gq_ref, gk_ref, indq_ref, indqt_ref, indk_ref, indkt_ref,
                          plo_ref, phi_ref, qn_ref, kn_ref, v_ref, ik_ref, klo_ref, khi_ref, vlo_ref, vhi_ref,
                          ikb_ref, iqb_ref, iw_ref)
        iqb_ref[...] = iq_ref[...].astype(BF16)

    return pl.pallas_call(
        body,
        grid=(n // tm,),
        in_specs=[pl.BlockSpec((tm, ATT_WIDTH), lambda i: (i, COL_Q // ATT_WIDTH)),
                  pl.BlockSpec((tm, 2 * KV_WIDTH), lambda i: (i, COL_K // (2 * KV_WIDTH))),
                  pl.BlockSpec((tm, LANES), lambda i: (i, COL_SMALL // LANES)),
                  pl.BlockSpec((tm, IDX_HEADS * IDX_DIM), lambda i: (i, COL_IQ // (IDX_HEADS * IDX_DIM))),
                  full(gq), full(gk)] + [full(c) for c in consts],
        out_specs=[pl.BlockSpec((tm, s[1]), lambda i: (i, 0)) if len(s) == 2
                   else pl.BlockSpec((KV_HEADS, tm, LANES), lambda i: (0, i, 0)) for s, _ in outs],
        out_shape=[jax.ShapeDtypeStruct(s, d) for s, d in outs],
        compiler_params=_cparams(("parallel",)),
        name="attn_prep",
    )(proj, proj, proj, proj, gq, gk, *consts)


def _merge_kernel(ya_ref, yb_ref, ga_ref, gb_ref, x_ref, wa_ref, wb_ref, wo_ref, g2_ref, x1_ref, h_ref):
    a = _dot(ya_ref[...].astype(BF16), wa_ref[...])
    b = _dot(yb_ref[...].astype(BF16), wb_ref[...])
    merged = jax.nn.sigmoid(ga_ref[...]) * a + jax.nn.sigmoid(gb_ref[...]) * b
    x1 = x_ref[...] + _dot(merged.astype(BF16), wo_ref[...])
    x1_ref[...] = x1
    ms = jnp.mean(x1 * x1, axis=-1, keepdims=True)
    h_ref[...] = (x1 * lax.rsqrt(ms + EPS) * g2_ref[...]).astype(BF16)


def _merge(y_ssd, y_att, proj, x2d, wa, wb, wo, g2, tm):
    n = x2d.shape[0]

    def full(a):
        return pl.BlockSpec(a.shape, lambda i: (0,) * a.ndim)

    g2 = g2.reshape(1, D_MODEL)
    return pl.pallas_call(
        _merge_kernel,
        grid=(n // tm,),
        in_specs=[pl.BlockSpec((tm, D_INNER), lambda i: (i, 0)),
                  pl.BlockSpec((tm, ATT_WIDTH), lambda i: (i, 0)),
                  pl.BlockSpec((tm, D_MODEL), lambda i: (i, COL_GA // D_MODEL)),
                  pl.BlockSpec((tm, D_MODEL), lambda i: (i, COL_GB // D_MODEL)),
                  pl.BlockSpec((tm, D_MODEL), lambda i: (i, 0)),
                  full(wa), full(wb), full(wo), full(g2)],
        out_specs=[pl.BlockSpec((tm, D_MODEL), lambda i: (i, 0)),
                   pl.BlockSpec((tm, D_MODEL), lambda i: (i, 0))],
        out_shape=[jax.ShapeDtypeStruct((n, D_MODEL), F32), jax.ShapeDtypeStruct((n, D_MODEL), BF16)],
        compiler_params=_cparams(("parallel",)),
        name="merge",
    )(y_ssd, y_att, proj, proj, x2d, wa, wb, wo, g2)


def _top_values(cur, k):
    vals = []
    for _ in range(k):
        m = jnp.max(cur, axis=0, keepdims=True)
        vals.append(m)
        cur = jnp.where(cur >= m, NEG_INF, cur)
    return vals


def _peer_select_kernel(h_ref, wq_ref, keys_ref, s1_ref, c1_ref, s2_ref, e2_ref, tau_ref):
    q = _dot(h_ref[...], wq_ref[...]).astype(BF16)
    for hd in range(PEER_HEADS):
        base = hd * 2 * PEER_HALF
        s1 = _dot_nt(keys_ref[0, hd], q[:, base:base + PEER_HALF])
        s2 = _dot_nt(keys_ref[1, hd], q[:, base + PEER_HALF:base + 2 * PEER_HALF])
        v1 = _top_values(s1, PEER_TOPK)
        v2 = jnp.concatenate(_top_values(s2, PEER_TOPK), axis=0)
        cand = jnp.concatenate([v1[i] + v2 for i in range(PEER_TOPK)], axis=0)
        top = _top_values(cand, PEER_TOPK)
        zsum = sum(jnp.exp(t - top[0]) for t in top)
        s1_ref[hd] = s1
        s2_ref[hd] = s2
        c1_ref[hd] = jnp.exp(s1 - v1[0]) / zsum
        e2_ref[hd] = jnp.exp(s2 - v2[0:1])
        tau_ref[hd] = top[PEER_TOPK - 1]


def _peer_select(hb, wq, keys, tn):
    n = hb.shape[0]
    big = jax.ShapeDtypeStruct((PEER_HEADS, PEER_N_KEYS, n), F32)
    bspec = pl.BlockSpec((PEER_HEADS, PEER_N_KEYS, tn), lambda i: (0, 0, i))
    return pl.pallas_call(
        _peer_select_kernel,
        grid=(n // tn,),
        in_specs=[pl.BlockSpec((tn, D_MODEL), lambda i: (i, 0)),
                  pl.BlockSpec(wq.shape, lambda i: (0, 0)),
                  pl.BlockSpec(keys.shape, lambda i: (0, 0, 0, 0))],
        out_specs=[bspec, bspec, bspec, bspec, pl.BlockSpec((PEER_HEADS, 1, tn), lambda i: (0, 0, i))],
        out_shape=[big, big, big, big, jax.ShapeDtypeStruct((PEER_HEADS, 1, n), F32)],
        compiler_params=_cparams(("parallel",)),
        name="peer_select",
    )(hb, wq, keys)


def _gelu_tanh(x):
    return 0.5 * x * (1.0 + jnp.tanh(math.sqrt(2.0 / math.pi) * (x + 0.044715 * (x * x * x))))


PEER_EC = 512


def _peer_mix_kernel(h_ref, u_ref, vt_ref, s1_ref, c1_ref, s2_ref, e2_ref, tau_ref, x1_ref, y_ref, acc_ref):
    j = pl.program_id(1)

    @pl.when(j == 0)
    def _():
        acc_ref[...] = jnp.zeros_like(acc_ref)

    act = _gelu_tanh(_dot_nt(u_ref[...], h_ref[...]))
    rows = []
    for ii in range(PEER_EC // PEER_N_KEYS):
        i1 = j * (PEER_EC // PEER_N_KEYS) + ii
        w = jnp.zeros((PEER_N_KEYS, act.shape[1]), F32)
        for hd in range(PEER_HEADS):
            s1r = s1_ref[hd, pl.ds(i1, 1), :]
            c1r = c1_ref[hd, pl.ds(i1, 1), :]
            chosen = (s2_ref[hd] + s1r) >= tau_ref[hd]
            w = w + jnp.where(chosen, e2_ref[hd] * c1r, 0.0)
        rows.append((act[ii * PEER_N_KEYS:(ii + 1) * PEER_N_KEYS] * w).astype(BF16))
    gmat = jnp.concatenate(rows, axis=0)
    acc_ref[...] += _dot(vt_ref[...], gmat)

    @pl.when(j == pl.num_programs(1) - 1)
    def _():
        y_ref[...] = x1_ref[...] + acc_ref[...].T


def _peer_mix(hb, ub, vtb, sel, x1, tn):
    n = hb.shape[0]
    s1, c1, s2, e2, tau = sel
    bspec = pl.BlockSpec((PEER_HEADS, PEER_N_KEYS, tn), lambda i, j: (0, 0, i))
    return pl.pallas_call(
        _peer_mix_kernel,
        grid=(n // tn, PEER_EXPERTS // PEER_EC),
        in_specs=[pl.BlockSpec((tn, D_MODEL), lambda i, j: (i, 0)),
                  pl.BlockSpec((PEER_EC, D_MODEL), lambda i, j: (j, 0)),
                  pl.BlockSpec((D_MODEL, PEER_EC), lambda i, j: (0, j)),
                  bspec, bspec, bspec, bspec,
                  pl.BlockSpec((PEER_HEADS, 1, tn), lambda i, j: (0, 0, i)),
                  pl.BlockSpec((tn, D_MODEL), lambda i, j: (i, 0))],
        out_specs=pl.BlockSpec((tn, D_MODEL), lambda i, j: (i, 0)),
        out_shape=jax.ShapeDtypeStruct((n, D_MODEL), F32),
        scratch_shapes=[pltpu.VMEM((D_MODEL, tn), F32)],
        compiler_params=_cparams(("parallel", "arbitrary")),
        name="peer_mix",
    )(hb, ub, vtb, s1, c1, s2, e2, tau, x1)


HALO = 8
N_PAIRS = SSD_HEADS // 2
GROUP_W = D_INNER // SSD_GROUPS


def _softplus(x):
    return jnp.maximum(x, 0.0) + jnp.log1p(jnp.exp(-jnp.abs(x)))


def _silu(x):
    return x * jax.nn.sigmoid(x)


def _lane_bcast(col):
    return jnp.broadcast_to(col, (col.shape[0], LANES))


def _ssd_gate_norm(y, xs, z, dvec, gain):
    y = (y + dvec * xs) * _silu(z)
    parts = []
    for g in range(SSD_GROUPS):
        yg = y[:, g * GROUP_W:(g + 1) * GROUP_W]
        ms = jnp.mean(yg * yg, axis=-1, keepdims=True)
        parts.append(yg * lax.rsqrt(ms + EPS))
    return jnp.concatenate(parts, axis=1) * gain


def _ssd_prompt_kernel(xbc_ref, z_ref, sm_ref, cw_ref, cb_ref, dtb_ref, alog_ref, dvec_ref, gain_ref, tri_ref,
                       y_ref, hout_ref, ext_ref, h_ref, yacc_ref):
    c = pl.program_id(1)

    @pl.when(c == 0)
    def _():
        ext_ref[0:HALO, :] = jnp.zeros((HALO, CONV_DIM), F32)
        h_ref[...] = jnp.zeros_like(h_ref)

    x = xbc_ref[...]
    ext_ref[HALO:HALO + SSD_CHUNK, :] = x
    conv = cb_ref[...]
    for j in range(CONV_W):
        conv = conv + cw_ref[j:j + 1, :] * ext_ref[pl.ds(HALO - (CONV_W - 1) + j, SSD_CHUNK), :]
    ext_ref[0:HALO, :] = x[SSD_CHUNK - HALO:, :]
    xc = _silu(conv)
    xs = xc[:, :D_INNER]

    lane = lax.broadcasted_iota(I32, (SSD_CHUNK, LANES), 1)
    row = lax.broadcasted_iota(I32, (SSD_CHUNK, LANES), 0)
    dt = jnp.where(lane < SSD_HEADS, _softplus(sm_ref[...] + dtb_ref[...]), 0.0)
    da = dt * (-jnp.exp(alog_ref[...]))
    hi, mid, lo = _split3(da)
    tri = tri_ref[...]
    acum = _dot(tri, hi) + _dot(tri, mid) + _dot(tri, lo)
    acum_t = acum.T
    dt_t = dt.T
    alast = acum[SSD_CHUNK - 1:SSD_CHUNK, :]
    e_in = jnp.exp(acum)
    e_out = jnp.exp(alast - acum) * dt
    e_all = jnp.exp(alast)
    causal = row >= lane
    lo_half = lane < SSD_HEAD_DIM

    for g in range(SSD_GROUPS):
        bg = xc[:, D_INNER + g * D_STATE:D_INNER + (g + 1) * D_STATE].astype(BF16)
        cg = xc[:, D_INNER + SSD_GROUPS * D_STATE + g * D_STATE:
                D_INNER + SSD_GROUPS * D_STATE + (g + 1) * D_STATE].astype(BF16)
        cbm = _dot_nt(cg, bg)
        for pp in range(2):
            pair = 2 * g + pp
            r0 = 2 * pair
            xp = xs[:, r0 * SSD_HEAD_DIM:(r0 + 2) * SSD_HEAD_DIM]
            yp = jnp.zeros((SSD_CHUNK, LANES), F32)
            for hh in range(2):
                r = r0 + hh
                seg = _lane_bcast(acum[:, r:r + 1]) - acum_t[r:r + 1, :]
                lmat = jnp.exp(jnp.where(causal, seg, NEG_INF))
                m = (cbm * lmat * dt_t[r:r + 1, :]).astype(BF16)
                xm = jnp.where(lo_half == (hh == 0), xp, 0.0).astype(BF16)
                yp = yp + _dot(m, xm)
            hp = h_ref[pair]
            scale_in = jnp.where(lo_half, _lane_bcast(e_in[:, r0:r0 + 1]), _lane_bcast(e_in[:, r0 + 1:r0 + 2]))
            yp = yp + _dot_nt(cg, hp.astype(BF16)) * scale_in
            scale_out = jnp.where(lo_half, _lane_bcast(e_out[:, r0:r0 + 1]), _lane_bcast(e_out[:, r0 + 1:r0 + 2]))
            xd = (xp * scale_out).astype(BF16)
            hdec = jnp.where(row < SSD_HEAD_DIM, e_all[:, r0:r0 + 1], e_all[:, r0 + 1:r0 + 2])
            h_ref[pair] = hp * hdec + _dot_tn(xd, bg)
            yacc_ref[:, r0 * SSD_HEAD_DIM:(r0 + 2) * SSD_HEAD_DIM] = yp

    y_ref[...] = _ssd_gate_norm(yacc_ref[...], xs, z_ref[...], dvec_ref[...], gain_ref[...])

    @pl.when(c == pl.num_programs(1) - 1)
    def _():
        hout_ref[0] = h_ref[...]


def _pad_lanes(v):
    return jnp.zeros((1, LANES), F32).at[0, :v.shape[0]].set(v)


def _ssd_prompt(proj, bsz, conv_w, conv_b, dt_bias, a_log, d_skip, ssd_norm_g):
    n = proj.shape[0]
    nc = n // bsz // SSD_CHUNK
    tri = jnp.asarray(np.tril(np.ones((SSD_CHUNK, SSD_CHUNK), np.float32)), BF16)
    dvec = jnp.repeat(d_skip, SSD_HEAD_DIM).reshape(1, D_INNER)
    small = [conv_w, conv_b.reshape(1, CONV_DIM), _pad_lanes(dt_bias), _pad_lanes(a_log), dvec,
             ssd_norm_g.reshape(1, D_INNER), tri]

    def full(a):
        return pl.BlockSpec(a.shape, lambda b, c: (0,) * a.ndim)

    return pl.pallas_call(
        _ssd_prompt_kernel,
        grid=(bsz, nc),
        in_specs=[pl.BlockSpec((SSD_CHUNK, CONV_DIM), lambda b, c: (b * nc + c, 0)),
                  pl.BlockSpec((SSD_CHUNK, D_INNER), lambda b, c: (b * nc + c, COL_Z // D_INNER)),
                  pl.BlockSpec((SSD_CHUNK, LANES), lambda b, c: (b * nc + c, COL_SMALL // LANES))]
                 + [full(a) for a in small],
        out_specs=[pl.BlockSpec((SSD_CHUNK, D_INNER), lambda b, c: (b * nc + c, 0)),
                   pl.BlockSpec((1, N_PAIRS, LANES, D_STATE), lambda b, c: (b, 0, 0, 0))],
        out_shape=[jax.ShapeDtypeStruct((n, D_INNER), F32),
                   jax.ShapeDtypeStruct((bsz, N_PAIRS, LANES, D_STATE), F32)],
        scratch_shapes=[pltpu.VMEM((HALO + SSD_CHUNK, CONV_DIM), F32),
                        pltpu.VMEM((N_PAIRS, LANES, D_STATE), F32),
                        pltpu.VMEM((SSD_CHUNK, D_INNER), F32)],
        compiler_params=_cparams(("parallel", "arbitrary")),
        name="ssd_prompt",
    )(proj, proj, proj, *small)


def _t5_bucket_np(dist):
    n = np.maximum(dist, 0)
    max_exact = NUM_BUCKETS // 2
    nf = np.maximum(n, 1).astype(np.float32)
    ratio = np.log(nf / np.float32(max_exact)) / np.float32(math.log(MAX_DISTANCE / max_exact))
    large = max_exact + (ratio * np.float32(NUM_BUCKETS - max_exact)).astype(np.int32)
    large = np.minimum(large, NUM_BUCKETS - 1)
    return np.where(n < max_exact, n, large).astype(np.int32)


BIAS_TJ = 2048


def _bias_kernel(rel_t_ref, bucket_ref, o_ref):
    ids = lax.broadcasted_iota(I32, (NUM_BUCKETS, BIAS_TJ), 0)
    onehot = jnp.where(ids == bucket_ref[...], 1.0, 0.0).astype(BF16)
    o_ref[...] = _exact_dot(rel_t_ref[...], onehot)


def _bias_lookup(rel_bias, buckets):
    j = buckets.shape[0]
    return pl.pallas_call(
        _bias_kernel,
        grid=(j // BIAS_TJ,),
        in_specs=[pl.BlockSpec((ATT_HEADS, NUM_BUCKETS), lambda i: (0, 0)),
                  pl.BlockSpec((1, BIAS_TJ), lambda i: (0, i))],
        out_specs=pl.BlockSpec((ATT_HEADS, BIAS_TJ), lambda i: (0, i)),
        out_shape=jax.ShapeDtypeStruct((ATT_HEADS, j), F32),
        compiler_params=_cparams(("parallel",)),
        name="bias_lookup",
    )(rel_bias.T, jnp.asarray(buckets.reshape(1, j)))


def _prompt_bias_buckets():
    i = np.arange(Q_BLOCK)[:, None]
    j = np.arange(Q_BLOCK)[None, :]
    far = np.full((Q_BLOCK, Q_BLOCK), 2 * Q_BLOCK)
    return _t5_bucket_np(np.stack([far, i - j + Q_BLOCK, i - j]))


KEY_TILES = 4
KEY_CHUNK = KEY_TILES * Q_BLOCK


def _sortable_key(x):
    b = pltpu.bitcast(x, I32)
    return jnp.where(b < 0, b ^ 0x7FFFFFFF, b)


def _kth_largest_key(count_ge, k, shape):
    def body(it, cand):
        trial = cand + jnp.left_shift(jnp.int32(1), 31 - it)
        return jnp.where(count_ge(trial) >= k, trial, cand)
    return lax.fori_loop(0, 32, body, jnp.full(shape, INT_MIN, I32))


def _select_keys(n_sel, qb, iq_ref, iw_ref, ik_ref, triu_ref, key_ref, mask_ref):
    nchunk = qb // KEY_TILES + 1
    row = lax.broadcasted_iota(I32, (Q_BLOCK, KEY_CHUNK), 0)
    lane = lax.broadcasted_iota(I32, (Q_BLOCK, KEY_CHUNK), 1)

    def admissible(c):
        return lane + (c * KEY_CHUNK - qb * Q_BLOCK) <= row

    iw = iw_ref[...]
    iq = iq_ref[...]
    wcols = [jnp.broadcast_to(iw[:, SMALL_IW + h:SMALL_IW + h + 1], (Q_BLOCK, KEY_CHUNK)) for h in range(IDX_HEADS)]
    iqs = [iq[:, h * IDX_DIM:(h + 1) * IDX_DIM] for h in range(IDX_HEADS)]

    def score_chunk(c, _):
        ikc = ik_ref[pl.ds(pl.multiple_of(c * KEY_CHUNK, KEY_CHUNK), KEY_CHUNK), :]
        sc = jnp.zeros((Q_BLOCK, KEY_CHUNK), F32)
        for h in range(IDX_HEADS):
            sc = sc + wcols[h] * jnp.maximum(_dot_nt(iqs[h], ikc), 0.0)
        key_ref[c] = _sortable_key(jnp.where(admissible(c), sc, NEG_INF))
        return 0

    lax.fori_loop(0, nchunk, score_chunk, 0)

    def count_ge(trial):
        def body(c, acc):
            return acc + jnp.where(key_ref[c] >= trial, 1, 0)
        acc = lax.fori_loop(0, nchunk, body, jnp.zeros((Q_BLOCK, KEY_CHUNK), I32))
        return jnp.sum(acc, axis=1, keepdims=True)

    tau = _kth_largest_key(count_ge, n_sel, (Q_BLOCK, 1))

    def count_gt(c, acc):
        return acc + jnp.where(key_ref[c] > tau, 1, 0)
    n_gt = jnp.sum(lax.fori_loop(0, nchunk, count_gt, jnp.zeros((Q_BLOCK, KEY_CHUNK), I32)), axis=1, keepdims=True)
    need = (n_sel - n_gt).astype(F32)

    def mask_chunk(c, seen):
        k = key_ref[c]
        eq = k == tau
        eqf = jnp.where(eq, 1.0, 0.0)
        pref = _dot(eqf.astype(BF16), triu_ref[...]) + seen
        sel = jnp.logical_or(k > tau, jnp.logical_and(eq, pref <= need))
        mask_ref[c] = jnp.where(jnp.logical_and(sel, admissible(c)), 0.0, NEG_INF)
        return seen + jnp.sum(eqf, axis=1, keepdims=True)

    lax.fori_loop(0, nchunk, mask_chunk, jnp.zeros((Q_BLOCK, 1), F32))


def _attn_prompt_kernel(n_sel, qp_ref, iq_ref, iw_ref, klo_ref, khi_ref, vlo_ref, vhi_ref, ik_ref, bt_ref, triu_ref,
                        y_ref, key_ref, mask_ref, lbuf_ref):
    qb = pl.program_id(1)
    pair = pl.program_id(2)
    nchunk = qb // KEY_TILES + 1

    @pl.when(pair == 0)
    def _():
        _select_keys(n_sel, qb, iq_ref, iw_ref, ik_ref, triu_ref, key_ref, mask_ref)

    g = pair // (ATT_HEADS // KV_HEADS // 2)
    qp = qp_ref[...]
    out = jnp.zeros((Q_BLOCK, LANES), F32)
    for half, (k_ref, v_ref) in enumerate(((klo_ref, vlo_ref), (khi_ref, vhi_ref))):
        h = 2 * pair + half

        def logits_chunk(c, mx, k_ref=k_ref, h=h):
            kt = k_ref[g, pl.ds(pl.multiple_of(c * KEY_CHUNK, KEY_CHUNK), KEY_CHUNK), :]
            bias = jnp.concatenate([bt_ref[h, jnp.clip(c * KEY_TILES + u - qb + 2, 0, 2)] for u in range(KEY_TILES)],
                                   axis=1)
            lg = _dot_nt(qp, kt) + bias + mask_ref[c]
            lbuf_ref[c] = lg
            return jnp.maximum(mx, lg)

        mx = lax.fori_loop(0, nchunk, logits_chunk, jnp.full((Q_BLOCK, KEY_CHUNK), NEG_INF, F32))
        mrow = jnp.max(mx, axis=1, keepdims=True)

        def pv_chunk(c, carry, v_ref=v_ref, mrow=mrow):
            lsum, acc = carry
            p = jnp.exp(lbuf_ref[c] - mrow)
            vt = v_ref[g, pl.ds(pl.multiple_of(c * KEY_CHUNK, KEY_CHUNK), KEY_CHUNK), :]
            return lsum + p, acc + _dot(p.astype(BF16), vt)

        lsum, acc = lax.fori_loop(0, nchunk, pv_chunk,
                                  (jnp.zeros((Q_BLOCK, KEY_CHUNK), F32), jnp.zeros((Q_BLOCK, LANES), F32)))
        out = out + acc / jnp.sum(lsum, axis=1, keepdims=True)
    y_ref[...] = out


def _attn_prompt(prep, btile, bsz):
    qn, _, _, _, klo, khi, vlo, vhi, ikb, iqb, iw = prep
    n = qn.shape[0]
    t = n // bsz
    nqb = t // Q_BLOCK
    assert t % KEY_CHUNK == 0
    nch = t // KEY_CHUNK
    n_sel = min(TOPK_KEYS_MAX, t // 4)
    triu = jnp.asarray(np.triu(np.ones((KEY_CHUNK, KEY_CHUNK), np.float32)), BF16)
    kvspec = pl.BlockSpec((KV_HEADS, t, LANES), lambda b, q, p: (0, b, 0))
    return pl.pallas_call(
        functools.partial(_attn_prompt_kernel, n_sel),
        grid=(bsz, nqb, ATT_HEADS // 2),
        in_specs=[pl.BlockSpec((Q_BLOCK, LANES), lambda b, q, p: (b * nqb + q, p)),
                  pl.BlockSpec((Q_BLOCK, IDX_HEADS * IDX_DIM), lambda b, q, p: (b * nqb + q, 0)),
                  pl.BlockSpec((Q_BLOCK, LANES), lambda b, q, p: (b * nqb + q, 0)),
                  kvspec, kvspec, kvspec, kvspec,
                  pl.BlockSpec((t, IDX_DIM), lambda b, q, p: (b, 0)),
                  pl.BlockSpec(btile.shape, lambda b, q, p: (0, 0, 0, 0)),
                  pl.BlockSpec(triu.shape, lambda b, q, p: (0, 0))],
        out_specs=pl.BlockSpec((Q_BLOCK, LANES), lambda b, q, p: (b * nqb + q, p)),
        out_shape=jax.ShapeDtypeStruct((n, ATT_WIDTH), F32),
        scratch_shapes=[pltpu.VMEM((nch, Q_BLOCK, KEY_CHUNK), I32),
                        pltpu.VMEM((nch, Q_BLOCK, KEY_CHUNK), F32),
                        pltpu.VMEM((nch, Q_BLOCK, KEY_CHUNK), F32)],
        compiler_params=_cparams(("parallel", "arbitrary", "arbitrary")),
        name="attn_prompt",
    )(qn, iqb, iw, klo, khi, vlo, vhi, ikb, btile, triu)


def _head_expand_indicator():
    m = np.zeros((LANES, D_INNER), np.float32)
    m[np.arange(D_INNER) // SSD_HEAD_DIM, np.arange(D_INNER)] = 1.0
    return m


def _ssd_step_kernel(xbc_ref, z_ref, sm_ref, sc_ref, hin_ref, cw_ref, cb_ref, dtb_ref, alog_ref, dvec_ref, gain_ref,
                     exp_ref, y_ref, hout_ref, cs_ref, xs_ref, bm_ref, cm_ref, dtx_t_ref, dec_t_ref, y_t_ref):
    b = pl.program_id(0)
    db = xbc_ref.shape[0]

    @pl.when(b == 0)
    def _():
        x = xbc_ref[...]
        sc = sc_ref[...]
        conv = cb_ref[...] + cw_ref[CONV_W - 1:CONV_W, :] * x
        for j in range(CONV_W - 1):
            conv = conv + cw_ref[j:j + 1, :] * sc[:, j * CONV_DIM:(j + 1) * CONV_DIM]
        cs_ref[:, :(CONV_W - 2) * CONV_DIM] = sc[:, CONV_DIM:]
        cs_ref[:, (CONV_W - 2) * CONV_DIM:] = x
        xc = _silu(conv)
        xs = xc[:, :D_INNER]
        lane = lax.broadcasted_iota(I32, (db, LANES), 1)
        dt = jnp.where(lane < SSD_HEADS, _softplus(sm_ref[...] + dtb_ref[...]), 0.0)
        dec = jnp.exp(dt * (-jnp.exp(alog_ref[...])))
        xs_ref[0:db, :] = xs
        bm_ref[0:db, :] = xc[:, D_INNER:D_INNER + SSD_GROUPS * D_STATE]
        cm_ref[0:db, :] = xc[:, D_INNER + SSD_GROUPS * D_STATE:]
        dtx_t_ref[:, 0:db] = (_exact_dot(dt, exp_ref[...]) * xs).T
        dec_t_ref[:, 0:db] = _exact_dot(dec, exp_ref[...]).T
        y_t_ref[...] = jnp.zeros_like(y_t_ref)

    lane = lax.broadcasted_iota(I32, (LANES, LANES), 1)
    mine = lane == b
    base = pl.multiple_of((b // 8) * 8, 8)
    my_row = lax.broadcasted_iota(I32, (8, SSD_GROUPS * D_STATE), 0) == (b % 8)
    bm_b = jnp.sum(jnp.where(my_row, bm_ref[pl.ds(base, 8), :], 0.0), axis=0, keepdims=True)
    cm_b = jnp.sum(jnp.where(my_row, cm_ref[pl.ds(base, 8), :], 0.0), axis=0, keepdims=True)
    for pair in range(N_PAIRS):
        g = pair // 2
        rows = slice(pair * LANES, (pair + 1) * LANES)
        xcol = jnp.sum(jnp.where(mine, dtx_t_ref[rows, :], 0.0), axis=1, keepdims=True)
        dcol = jnp.sum(jnp.where(mine, dec_t_ref[rows, :], 0.0), axis=1, keepdims=True)
        brow = bm_b[:, g * D_STATE:(g + 1) * D_STATE]
        crow = cm_b[:, g * D_STATE:(g + 1) * D_STATE]
        hnew = hin_ref[0, pair] * dcol + xcol * brow
        hout_ref[0, pair] = hnew
        ycol = jnp.sum(hnew * crow, axis=1, keepdims=True)
        y_t_ref[rows, :] = jnp.where(mine, ycol, y_t_ref[rows, :])

    @pl.when(b == pl.num_programs(0) - 1)
    def _():
        y = y_t_ref[...].T[0:db, :]
        y_ref[...] = _ssd_gate_norm(y, xs_ref[0:db, :], z_ref[...], dvec_ref[...], gain_ref[...])


def _ssd_step(proj, state_conv2d, state_ssm4d, conv_w, conv_b, dt_bias, a_log, d_skip, ssd_norm_g):
    db = proj.shape[0]
    assert db <= LANES and db % 8 == 0
    dvec = jnp.repeat(d_skip, SSD_HEAD_DIM).reshape(1, D_INNER)
    small = [conv_w, conv_b.reshape(1, CONV_DIM), _pad_lanes(dt_bias), _pad_lanes(a_log), dvec,
             ssd_norm_g.reshape(1, D_INNER), jnp.asarray(_head_expand_indicator(), BF16)]

    def full(a):
        return pl.BlockSpec(a.shape, lambda b: (0,) * a.ndim)

    cs_w = (CONV_W - 1) * CONV_DIM
    hspec = pl.BlockSpec((1, N_PAIRS, LANES, D_STATE), lambda b: (b, 0, 0, 0))
    return pl.pallas_call(
        _ssd_step_kernel,
        grid=(db,),
        in_specs=[pl.BlockSpec((db, CONV_DIM), lambda b: (0, 0)),
                  pl.BlockSpec((db, D_INNER), lambda b: (0, COL_Z // D_INNER)),
                  pl.BlockSpec((db, LANES), lambda b: (0, COL_SMALL // LANES)),
                  pl.BlockSpec((db, cs_w), lambda b: (0, 0)),
                  hspec] + [full(a) for a in small],
        out_specs=[pl.BlockSpec((db, D_INNER), lambda b: (0, 0)),
                   hspec,
                   pl.BlockSpec((db, cs_w), lambda b: (0, 0))],
        out_shape=[jax.ShapeDtypeStruct((db, D_INNER), F32),
                   jax.ShapeDtypeStruct(state_ssm4d.shape, F32),
                   jax.ShapeDtypeStruct((db, cs_w), F32)],
        scratch_shapes=[pltpu.VMEM((LANES, D_INNER), F32),
                        pltpu.VMEM((LANES, SSD_GROUPS * D_STATE), F32),
                        pltpu.VMEM((LANES, SSD_GROUPS * D_STATE), F32),
                        pltpu.VMEM((D_INNER, LANES), F32),
                        pltpu.VMEM((D_INNER, LANES), F32),
                        pltpu.VMEM((D_INNER, LANES), F32)],
        compiler_params=_cparams(("arbitrary",)),
        name="ssd_step",
    )(proj, proj, proj, state_conv2d, state_ssm4d, *small)


def _page_copies(pt_ref, seq, cache_ref, buf_ref, slot, sem_ref, n_pages):
    return [pltpu.make_async_copy(cache_ref.at[pt_ref[seq, p]], buf_ref.at[slot, p], sem_ref.at[slot])
            for p in range(n_pages)]


def _paged_fetch(pt_ref, streams, n_pages):
    b = pl.program_id(0)
    slot = b % 2

    @pl.when(b == 0)
    def _():
        for cache_ref, buf_ref, sem_ref in streams:
            for cp in _page_copies(pt_ref, 0, cache_ref, buf_ref, 0, sem_ref, n_pages):
                cp.start()

    @pl.when(b + 1 < pl.num_programs(0))
    def _():
        for cache_ref, buf_ref, sem_ref in streams:
            for cp in _page_copies(pt_ref, b + 1, cache_ref, buf_ref, 1 - slot, sem_ref, n_pages):
                cp.start()

    for cache_ref, buf_ref, sem_ref in streams:
        for cp in _page_copies(pt_ref, b, cache_ref, buf_ref, slot, sem_ref, n_pages):
            cp.wait()
    return slot


def _dec_score_kernel(n_pages, pt_ref, iq_ref, w_ref, iknew_ref, cache_ref, o_ref, buf_ref, sem_ref):
    b = pl.program_id(0)
    slot = _paged_fetch(pt_ref, [(cache_ref, buf_ref, sem_ref)], n_pages)
    past = n_pages * PAGE_SIZE
    iq = iq_ref[...].astype(BF16)
    w = w_ref[...]
    for p in range(n_pages):
        s = _dot(iq, buf_ref[slot, p].astype(BF16))
        o_ref[0, :, p * PAGE_SIZE:(p + 1) * PAGE_SIZE] = jnp.sum(w * jnp.maximum(s, 0.0), axis=0, keepdims=True)
    iknew = iknew_ref[pl.ds(b, 1), :].astype(BF16).astype(F32)
    s_new = jnp.sum(iq.astype(F32) * iknew, axis=1, keepdims=True)
    sc_new = jnp.sum(w * jnp.maximum(s_new, 0.0), axis=0, keepdims=True)
    lane = lax.broadcasted_iota(I32, (1, LANES), 1)
    o_ref[0, :, past:past + LANES] = jnp.where(lane == 0, sc_new, NEG_INF)


def _dec_scores(page_table, cache_ik, iq8, w8, ikb_new):
    db, n_pages = page_table.shape
    s_pad = (n_pages + 1) * PAGE_SIZE
    grid_spec = pltpu.PrefetchScalarGridSpec(
        num_scalar_prefetch=1,
        grid=(db,),
        in_specs=[pl.BlockSpec((IDX_HEADS, IDX_DIM), lambda b, pt: (b, 0)),
                  pl.BlockSpec((IDX_HEADS, 1), lambda b, pt: (b, 0)),
                  pl.BlockSpec(ikb_new.shape, lambda b, pt: (0, 0)),
                  pl.BlockSpec(memory_space=pl.ANY)],
        out_specs=pl.BlockSpec((1, 1, s_pad), lambda b, pt: (b, 0, 0)),
        scratch_shapes=[pltpu.VMEM((2, n_pages, IDX_DIM, PAGE_SIZE), F32),
                        pltpu.SemaphoreType.DMA((2,))],
    )
    return pl.pallas_call(
        functools.partial(_dec_score_kernel, n_pages),
        grid_spec=grid_spec,
        out_shape=jax.ShapeDtypeStruct((db, 1, s_pad), F32),
        compiler_params=_cparams(("arbitrary",)),
        name="dec_scores",
    )(page_table, iq8, w8, ikb_new, cache_ik)


def _dec_select_kernel(n_sel, sc_ref, triu_ref, mask_ref, key_ref):
    db, s_pad = sc_ref.shape
    ntile = s_pad // LANES
    key_ref[...] = _sortable_key(sc_ref[...])

    def count_ge(trial):
        acc = jnp.zeros((db, LANES), I32)
        for j in range(ntile):
            acc = acc + jnp.where(key_ref[:, j * LANES:(j + 1) * LANES] >= trial, 1, 0)
        return jnp.sum(acc, axis=1, keepdims=True)

    tau = _kth_largest_key(count_ge, n_sel, (db, 1))
    acc = jnp.zeros((db, LANES), I32)
    for j in range(ntile):
        acc = acc + jnp.where(key_ref[:, j * LANES:(j + 1) * LANES] > tau, 1, 0)
    need = (n_sel - jnp.sum(acc, axis=1, keepdims=True)).astype(F32)
    seen = jnp.zeros((db, 1), F32)
    for j in range(ntile):
        k = key_ref[:, j * LANES:(j + 1) * LANES]
        eq = k == tau
        pref = _dot(jnp.where(eq, 1.0, 0.0).astype(BF16), triu_ref[...]) + seen
        sel = jnp.logical_or(k > tau, jnp.logical_and(eq, pref <= need))
        mask_ref[:, j * LANES:(j + 1) * LANES] = jnp.where(sel, 0.0, NEG_INF)
        seen = pref[:, LANES - 1:LANES]


def _dec_select(scores2d, n_sel):
    db, s_pad = scores2d.shape
    triu = jnp.asarray(np.triu(np.ones((LANES, LANES), np.float32)), BF16)
    return pl.pallas_call(
        functools.partial(_dec_select_kernel, n_sel),
        grid=(1,),
        in_specs=[pl.BlockSpec((db, s_pad), lambda i: (0, 0)),
                  pl.BlockSpec(triu.shape, lambda i: (0, 0))],
        out_specs=pl.BlockSpec((db, s_pad), lambda i: (0, 0)),
        out_shape=jax.ShapeDtypeStruct((db, s_pad), F32),
        scratch_shapes=[pltpu.VMEM((db, s_pad), I32)],
        compiler_params=_cparams(("arbitrary",)),
        name="dec_select",
    )(scores2d, triu)


def _dec_attn_kernel(n_pages, pt_ref, q_ref, mask_ref, bias_ref, knew_ref, vnew_ref, ck_ref, cv_ref,
                     y_ref, kbuf_ref, vbuf_ref, ksem_ref, vsem_ref):
    b = pl.program_id(0)
    slot = _paged_fetch(pt_ref, [(ck_ref, kbuf_ref, ksem_ref), (cv_ref, vbuf_ref, vsem_ref)], n_pages)
    past = n_pages * PAGE_SIZE
    q = q_ref[...]
    hrow = lax.broadcasted_iota(I32, (ATT_HEADS, KV_WIDTH), 0)
    lane = lax.broadcasted_iota(I32, (ATT_HEADS, KV_WIDTH), 1)
    own = (hrow // (ATT_HEADS // KV_HEADS)) == (lane // ATT_HEAD_DIM)
    qbd = jnp.where(own, jnp.concatenate([q] * KV_HEADS, axis=1), jnp.zeros_like(q[:, :1]))
    knew = knew_ref[pl.ds(b, 1), :].astype(BF16).astype(F32)
    l_new = jnp.sum(qbd.astype(F32) * knew, axis=1, keepdims=True)
    vnew = vnew_ref[pl.ds(b, 1), :].astype(BF16).astype(F32)
    mask = mask_ref[0]
    lg = jnp.concatenate([_dot(qbd, kbuf_ref[slot, p].astype(BF16)) for p in range(n_pages)], axis=1)
    lg = lg + bias_ref[:, 0:past] + mask[:, 0:past]
    tl = lax.broadcasted_iota(I32, (ATT_HEADS, LANES), 1)
    lg_new = jnp.where(tl == 0, l_new, 0.0) + bias_ref[:, past:past + LANES] + mask[:, past:past + LANES]
    m = jnp.maximum(jnp.max(lg, axis=1, keepdims=True), jnp.max(lg_new, axis=1, keepdims=True))
    p = jnp.exp(lg - m)
    p_new = jnp.exp(lg_new - m)
    denom = jnp.sum(p, axis=1, keepdims=True) + jnp.sum(p_new, axis=1, keepdims=True)
    pn = p_new[:, 0:1].astype(BF16).astype(F32)
    pb = p.astype(BF16)
    out = pn * vnew
    for pg in range(n_pages):
        out = out + _dot_nt(pb[:, pg * PAGE_SIZE:(pg + 1) * PAGE_SIZE], vbuf_ref[slot, pg].astype(BF16))
    out = out / denom
    r16 = lax.broadcasted_iota(I32, (ATT_HEADS, ATT_HEAD_DIM), 0) // (ATT_HEADS // KV_HEADS)
    y = jnp.zeros((ATT_HEADS, ATT_HEAD_DIM), F32)
    for g in range(KV_HEADS):
        y = jnp.where(r16 == g, out[:, g * ATT_HEAD_DIM:(g + 1) * ATT_HEAD_DIM], y)
    y_ref[...] = y


def _dec_attn(page_table, q16, mask3d, bias_dec, kn_new, v_new, cache_k3d, cache_v3d):
    db, n_pages = page_table.shape
    s_pad = (n_pages + 1) * PAGE_SIZE
    grid_spec = pltpu.PrefetchScalarGridSpec(
        num_scalar_prefetch=1,
        grid=(db,),
        in_specs=[pl.BlockSpec((ATT_HEADS, ATT_HEAD_DIM), lambda b, pt: (b, 0)),
                  pl.BlockSpec((1, 1, s_pad), lambda b, pt: (b, 0, 0)),
                  pl.BlockSpec(bias_dec.shape, lambda b, pt: (0, 0)),
                  pl.BlockSpec(kn_new.shape, lambda b, pt: (0, 0)),
                  pl.BlockSpec(v_new.shape, lambda b, pt: (0, 0)),
                  pl.BlockSpec(memory_space=pl.ANY),
                  pl.BlockSpec(memory_space=pl.ANY)],
        out_specs=pl.BlockSpec((ATT_HEADS, ATT_HEAD_DIM), lambda b, pt: (b, 0)),
        scratch_shapes=[pltpu.VMEM((2, n_pages, KV_WIDTH, PAGE_SIZE), F32),
                        pltpu.VMEM((2, n_pages, KV_WIDTH, PAGE_SIZE), F32),
                        pltpu.SemaphoreType.DMA((2,)),
                        pltpu.SemaphoreType.DMA((2,))],
    )
    return pl.pallas_call(
        functools.partial(_dec_attn_kernel, n_pages),
        grid_spec=grid_spec,
        out_shape=jax.ShapeDtypeStruct((db * ATT_HEADS, ATT_HEAD_DIM), F32),
        compiler_params=_cparams(("arbitrary",)),
        name="dec_attn",
    )(page_table, q16, mask3d, bias_dec, kn_new, v_new, cache_k3d, cache_v3d)


def _channel_mixer(hb, x1, peer_wq, peer_keys, ub, vtb, tn):
    sel = _peer_select(hb, peer_wq, peer_keys, tn)
    return _peer_mix(hb, ub, vtb, sel, x1, tn)


def kernel(x_prompt, x_sample, cache_k, cache_v, cache_idx_k, state_ssm, state_conv, page_table, rel_bias, norm1_g, w_in, conv_w, conv_b, dt_bias, a_log, d_skip, ssd_norm_g, q_norm_g, k_norm_g, w_branch_a, w_branch_b, w_out, norm2_g, peer_wq, peer_keys, peer_u, peer_v):
    depth = w_in.shape[0]
    bsz, t, _ = x_prompt.shape
    db, ds, _ = x_sample.shape
    assert depth == 1 and ds == 1, "single layer, one new token per sequence"
    n = bsz * t
    n_pages = page_table.shape[1]
    past = n_pages * PAGE_SIZE
    l = 0

    w_perm = _permute_w_in(w_in[l])
    wa, wb, wo = w_branch_a[l].astype(BF16), w_branch_b[l].astype(BF16), w_out[l].astype(BF16)
    wq, keys = peer_wq[l].astype(BF16), peer_keys[l].astype(BF16)
    ub, vtb = peer_u[l].astype(BF16), peer_v[l].T.astype(BF16)
    ssd_w = (conv_w[l], conv_b[l], dt_bias[l], a_log[l], d_skip[l], ssd_norm_g[l])

    dec_dist = np.concatenate([past - np.arange(past), np.zeros(PAGE_SIZE, np.int64)])
    n_prompt_b = 3 * Q_BLOCK * Q_BLOCK
    buckets = np.concatenate([_prompt_bias_buckets().reshape(-1), _t5_bucket_np(dec_dist)])
    pad = (-buckets.shape[0]) % BIAS_TJ
    bias_all = _bias_lookup(rel_bias, np.concatenate([buckets, np.zeros(pad, np.int32)]))
    btile = bias_all[:, :n_prompt_b].reshape(ATT_HEADS, 3, Q_BLOCK, Q_BLOCK)
    bias_dec = bias_all[:, n_prompt_b:n_prompt_b + past + PAGE_SIZE]

    xp = x_prompt.reshape(n, D_MODEL)
    proj = _in_proj(xp, norm1_g[l], w_perm, min(1024, n))
    y_ssd, h_t = _ssd_prompt(proj, bsz, *ssd_w)
    prep = _attn_prep(proj, q_norm_g[l], k_norm_g[l], min(512, n))
    y_att = _attn_prompt(prep, btile, bsz)
    x1, hb = _merge(y_ssd, y_att, proj, xp, wa, wb, wo, norm2_g[l], min(512, n))
    y_prompt = _channel_mixer(hb, x1, wq, keys, ub, vtb, min(512, n))

    k_prompt = prep[1].reshape(1, bsz, t, KV_HEADS, ATT_HEAD_DIM)
    v_prompt = prep[2].reshape(1, bsz, t, KV_HEADS, ATT_HEAD_DIM)
    ik_prompt = prep[3].reshape(1, bsz, t, IDX_DIM)
    ssm_prompt = h_t.reshape(1, bsz, SSD_HEADS, SSD_HEAD_DIM, D_STATE)
    conv_prompt = proj.reshape(bsz, t, PROJ_COLS)[:, t - (CONV_W - 1):, :CONV_DIM][None]

    xs_ = x_sample.reshape(db, D_MODEL)
    proj_s = _in_proj(xs_, norm1_g[l], w_perm, db)
    y_ssd_s, h_s, conv_s = _ssd_step(proj_s, state_conv[l].reshape(db, (CONV_W - 1) * CONV_DIM),
                                     state_ssm[l].reshape(db, N_PAIRS, LANES, D_STATE), *ssd_w)
    prep_s = _attn_prep(proj_s, q_norm_g[l], k_norm_g[l], db)
    qn_s, kn_s, v_s, ik_s, _, _, _, _, _, _, iw_s = prep_s
    iq8 = proj_s[:, COL_IQ:COL_IQ + IDX_HEADS * IDX_DIM].reshape(db * IDX_HEADS, IDX_DIM)
    w8 = iw_s[:, SMALL_IW:SMALL_IW + IDX_HEADS].reshape(db * IDX_HEADS, 1)
    ikt_pages = jnp.transpose(cache_idx_k[l], (0, 2, 1))
    kt_pages = jnp.transpose(cache_k[l], (0, 2, 3, 1)).reshape(-1, KV_WIDTH, PAGE_SIZE)
    vt_pages = jnp.transpose(cache_v[l], (0, 2, 3, 1)).reshape(-1, KV_WIDTH, PAGE_SIZE)
    scores = _dec_scores(page_table, ikt_pages, iq8, w8, ik_s)
    n_sel = min(TOPK_KEYS_MAX, (past + ds) // 4)
    mask = _dec_select(scores.reshape(db, past + PAGE_SIZE), n_sel)
    y_att_s = _dec_attn(page_table, qn_s.reshape(db * ATT_HEADS, ATT_HEAD_DIM),
                        mask.reshape(db, 1, past + PAGE_SIZE), bias_dec, kn_s, v_s,
                        kt_pages, vt_pages)
    x1_s, hb_s = _merge(y_ssd_s, y_att_s.reshape(db, ATT_WIDTH), proj_s, xs_, wa, wb, wo, norm2_g[l], db)
    y_sample = _channel_mixer(hb_s, x1_s, wq, keys, ub, vtb, db)

    return (y_prompt.reshape(bsz, t, D_MODEL), y_sample.reshape(db, ds, D_MODEL),
            k_prompt, v_prompt, ik_prompt, ssm_prompt, conv_prompt,
            kn_s.reshape(1, db, ds, KV_HEADS, ATT_HEAD_DIM), v_s.reshape(1, db, ds, KV_HEADS, ATT_HEAD_DIM),
            ik_s.reshape(1, db, ds, IDX_DIM),
            h_s.reshape(1, db, SSD_HEADS, SSD_HEAD_DIM, D_STATE),
            conv_s.reshape(1, db, CONV_W - 1, CONV_DIM))
```

```python
import functools
import math

import numpy as np
import jax
import jax.numpy as jnp
from jax import lax
from jax.experimental import pallas as pl
from jax.experimental.pallas import tpu as pltpu

F32 = jnp.float32
BF16 = jnp.bfloat16
I32 = jnp.int32

D_MODEL = 1024
PAGE_SIZE = 128
D_INNER = 2048
SSD_HEAD_DIM = 64
SSD_HEADS = 32
SSD_GROUPS = 8
D_STATE = 128
CONV_W = 4
CONV_DIM = D_INNER + 2 * SSD_GROUPS * D_STATE
SSD_CHUNK = 128
ATT_HEADS = 16
ATT_HEAD_DIM = 64
ATT_WIDTH = ATT_HEADS * ATT_HEAD_DIM
KV_HEADS = 4
KV_WIDTH = KV_HEADS * ATT_HEAD_DIM
IDX_HEADS = 8
IDX_DIM = 64
TOPK_KEYS_MAX = 256
Q_BLOCK = 128
NUM_BUCKETS = 32
MAX_DISTANCE = 128
PEER_HEADS = 8
PEER_N_KEYS = 128
PEER_EXPERTS = PEER_N_KEYS * PEER_N_KEYS
PEER_HALF = 128
PEER_TOPK = 16
EPS = 1e-6

LANES = 128
SUBLANES = 8
VMEM_LIMIT_BYTES = 56 * 1024 * 1024

COL_XBC = 0
COL_Z = COL_XBC + CONV_DIM
COL_Q = COL_Z + D_INNER
COL_K = COL_Q + ATT_WIDTH
COL_V = COL_K + KV_WIDTH
COL_IQ = COL_V + KV_WIDTH
COL_GA = COL_IQ + IDX_HEADS * IDX_DIM
COL_GB = COL_GA + D_MODEL
COL_SMALL = COL_GB + D_MODEL
SMALL_DT = 0
SMALL_IK = SSD_HEADS
SMALL_IW = SSD_HEADS + IDX_DIM
PROJ_COLS = COL_SMALL + LANES
PROJ_TN = 1152

INT_MIN = -2 ** 31
NEG_INF = float("-inf")


def _cparams(sem):
    return pltpu.CompilerParams(dimension_semantics=sem, vmem_limit_bytes=VMEM_LIMIT_BYTES)


def _split3(x):
    hi = x.astype(BF16)
    r1 = x - hi.astype(F32)
    mid = r1.astype(BF16)
    lo = (r1 - mid.astype(F32)).astype(BF16)
    return hi, mid, lo


def _dot(a, b):
    return jnp.dot(a, b, preferred_element_type=F32)


def _dot_nt(a, b):
    return lax.dot_general(a, b, (((1,), (1,)), ((), ())), preferred_element_type=F32)


def _dot_tn(a, b):
    return lax.dot_general(a, b, (((0,), (0,)), ((), ())), preferred_element_type=F32)


def _exact_dot(x, onehot_bf16):
    hi, mid, lo = _split3(x)
    return _dot(hi, onehot_bf16) + _dot(mid, onehot_bf16) + _dot(lo, onehot_bf16)


def _in_proj_kernel(x_ref, g_ref, w_ref, o_ref, xn_ref):
    @pl.when(pl.program_id(1) == 0)
    def _():
        x = x_ref[...]
        ms = jnp.mean(x * x, axis=-1, keepdims=True)
        xn_ref[...] = (x * lax.rsqrt(ms + EPS) * g_ref[...]).astype(BF16)

    o_ref[...] = _dot(xn_ref[...], w_ref[...])


def _in_proj(x2d, g, w_perm, tm):
    n = x2d.shape[0]
    return pl.pallas_call(
        _in_proj_kernel,
        grid=(n // tm, PROJ_COLS // PROJ_TN),
        in_specs=[pl.BlockSpec((tm, D_MODEL), lambda i, j: (i, 0)),
                  pl.BlockSpec((1, D_MODEL), lambda i, j: (0, 0)),
                  pl.BlockSpec((D_MODEL, PROJ_TN), lambda i, j: (0, j))],
        out_specs=pl.BlockSpec((tm, PROJ_TN), lambda i, j: (i, j)),
        out_shape=jax.ShapeDtypeStruct((n, PROJ_COLS), F32),
        scratch_shapes=[pltpu.VMEM((tm, D_MODEL), BF16)],
        compiler_params=_cparams(("parallel", "arbitrary")),
        name="in_proj",
    )(x2d, g.reshape(1, D_MODEL), w_perm)


def _permute_w_in(w_in):
    offs = np.cumsum([0, D_INNER, CONV_DIM, SSD_HEADS, ATT_WIDTH, KV_WIDTH, KV_WIDTH,
                      IDX_HEADS * IDX_DIM, IDX_DIM, IDX_HEADS, D_MODEL, D_MODEL])
    z, xbc, dt, q, k, v, iq, ik, iw, ga, gb = [w_in[:, offs[i]:offs[i + 1]] for i in range(11)]
    pad = jnp.zeros((D_MODEL, LANES - SSD_HEADS - IDX_DIM - IDX_HEADS), w_in.dtype)
    return jnp.concatenate([xbc, z, q, k, v, iq, ga, gb, dt, ik, iw, pad], axis=1).astype(BF16)


def _seg_indicator(width, seg):
    m = np.zeros((width, LANES), np.float32)
    m[np.arange(width), np.arange(width) // seg] = 1.0
    return m


def _half_placement():
    lo = np.zeros((KV_WIDTH, KV_HEADS * LANES), np.float32)
    hi = np.zeros((KV_WIDTH, KV_HEADS * LANES), np.float32)
    c = np.arange(KV_WIDTH)
    g, d = c // ATT_HEAD_DIM, c % ATT_HEAD_DIM
    lo[c, g * LANES + d] = 1.0
    hi[c, g * LANES + ATT_HEAD_DIM + d] = 1.0
    return lo, hi


def _head_rms(x, ind, ind_t, gain):
    sq = x * x
    hi = sq.astype(BF16)
    lo = (sq - hi.astype(F32)).astype(BF16)
    ss = _dot(hi, ind) + _dot(lo, ind)
    r = lax.rsqrt(ss * (1.0 / ATT_HEAD_DIM) + EPS)
    rb = _exact_dot(r, ind_t)
    return x * rb * gain


def _attn_prep_kernel(q_ref, kv_ref, sm_ref, gq_ref, gk_ref, indq_ref, indqt_ref, indk_ref, indkt_ref,
                      plo_ref, phi_ref,
                      qn_ref, kn_ref, v_ref, ik_ref, klo_ref, khi_ref, vlo_ref, vhi_ref,
                      ikb_ref, iqb_ref, iw_ref):
    qn = _head_rms(q_ref[...], indq_ref[...], indqt_ref[...], gq_ref[...])
    qn_ref[...] = (qn * (ATT_HEAD_DIM ** -0.5)).astype(BF16)
    kv = kv_ref[...]
    kn = _head_rms(kv[:, :KV_WIDTH], indk_ref[...], indkt_ref[...], gk_ref[...])
    v = kv[:, KV_WIDTH:]
    kn_ref[...] = kn
    v_ref[...] = v
    knb = kn.astype(BF16)
    vb = v.astype(BF16)
    for src, place_ref, dst_ref in ((knb, plo_ref, klo_ref), (knb, phi_ref, khi_ref),
                                    (vb, plo_ref, vlo_ref), (vb, phi_ref, vhi_ref)):
        placed = _dot(src, place_ref[...]).astype(BF16)
        for g in range(KV_HEADS):
            dst_ref[g] = placed[:, g * LANES:(g + 1) * LANES]
    sm = sm_ref[...]
    ik = sm[:, SMALL_IK:SMALL_IK + IDX_DIM]
    ik_ref[...] = ik
    ikb_ref[...] = ik.astype(BF16)
    iw_ref[...] = sm * (IDX_HEADS ** -0.5 * IDX_DIM ** -0.5)


def _attn_prep(proj, q_norm_g, k_norm_g, tm):
    n = proj.shape[0]
    gq = jnp.tile(q_norm_g, ATT_HEADS).reshape(1, ATT_WIDTH)
    gk = jnp.tile(k_norm_g, KV_HEADS).reshape(1, KV_WIDTH)
    indq = _seg_indicator(ATT_WIDTH, ATT_HEAD_DIM)
    indk = _seg_indicator(KV_WIDTH, ATT_HEAD_DIM)
    plo, phi = _half_placement()
    consts = [jnp.asarray(a, BF16) for a in (indq, indq.T, indk, indk.T, plo, phi)]

    def full(a):
        return pl.BlockSpec(a.shape, lambda i: (0,) * a.ndim)

    outs = [
        ((n, ATT_WIDTH), BF16), ((n, KV_WIDTH), F32), ((n, KV_WIDTH), F32), ((n, IDX_DIM), F32),
        ((KV_HEADS, n, LANES), BF16), ((KV_HEADS, n, LANES), BF16),
        ((KV_HEADS, n, LANES), BF16), ((KV_HEADS, n, LANES), BF16),
        ((n, IDX_DIM), BF16), ((n, IDX_HEADS * IDX_DIM), BF16), ((n, LANES), F32),
    ]

    def body(q_ref, kv_ref, sm_ref, iq_ref, *rest):
        (gq_ref, gk_ref, indq_ref, indqt_ref, indk_ref, indkt_ref, plo_ref, phi_ref,
         qn_ref, kn_ref, v_ref, ik_ref, klo_ref, khi_ref, vlo_ref, vhi_ref, ikb_ref, iqb_ref, iw_ref) = rest
        _attn_prep_kernel(q_ref, kv_ref, sm_ref, gq_ref, gk_ref, indq_ref, indqt_ref, indk_ref, indkt_ref,
                          plo_ref, phi_ref, qn_ref, kn_ref, v_ref, ik_ref, klo_ref, khi_ref, vlo_ref, vhi_ref,
                          ikb_ref, iqb_ref, iw_ref)
        iqb_ref[...] = iq_ref[...].astype(BF16)

    return pl.pallas_call(
        body,
        grid=(n // tm,),
        in_specs=[pl.BlockSpec((tm, ATT_WIDTH), lambda i: (i, COL_Q // ATT_WIDTH)),
                  pl.BlockSpec((tm, 2 * KV_WIDTH), lambda i: (i, COL_K // (2 * KV_WIDTH))),
                  pl.BlockSpec((tm, LANES), lambda i: (i, COL_SMALL // LANES)),
                  pl.BlockSpec((tm, IDX_HEADS * IDX_DIM), lambda i: (i, COL_IQ // (IDX_HEADS * IDX_DIM))),
                  full(gq), full(gk)] + [full(c) for c in consts],
        out_specs=[pl.BlockSpec((tm, s[1]), lambda i: (i, 0)) if len(s) == 2
                   else pl.BlockSpec((KV_HEADS, tm, LANES), lambda i: (0, i, 0)) for s, _ in outs],
        out_shape=[jax.ShapeDtypeStruct(s, d) for s, d in outs],
        compiler_params=_cparams(("parallel",)),
        name="attn_prep",
    )(proj, proj, proj, proj, gq, gk, *consts)


def _merge_kernel(ya_ref, yb_ref, ga_ref, gb_ref, x_ref, wa_ref, wb_ref, wo_ref, g2_ref, x1_ref, h_ref):
    a = _dot(ya_ref[...].astype(BF16), wa_ref[...])
    b = _dot(yb_ref[...].astype(BF16), wb_ref[...])
    merged = jax.nn.sigmoid(ga_ref[...]) * a + jax.nn.sigmoid(gb_ref[...]) * b
    x1 = x_ref[...] + _dot(merged.astype(BF16), wo_ref[...])
    x1_ref[...] = x1
    ms = jnp.mean(x1 * x1, axis=-1, keepdims=True)
    h_ref[...] = (x1 * lax.rsqrt(ms + EPS) * g2_ref[...]).astype(BF16)


def _merge(y_ssd, y_att, proj, x2d, wa, wb, wo, g2, tm):
    n = x2d.shape[0]

    def full(a):
        return pl.BlockSpec(a.shape, lambda i: (0,) * a.ndim)

    g2 = g2.reshape(1, D_MODEL)
    return pl.pallas_call(
        _merge_kernel,
        grid=(n // tm,),
        in_specs=[pl.BlockSpec((tm, D_INNER), lambda i: (i, 0)),
                  pl.BlockSpec((tm, ATT_WIDTH), lambda i: (i, 0)),
                  pl.BlockSpec((tm, D_MODEL), lambda i: (i, COL_GA // D_MODEL)),
                  pl.BlockSpec((tm, D_MODEL), lambda i: (i, COL_GB // D_MODEL)),
                  pl.BlockSpec((tm, D_MODEL), lambda i: (i, 0)),
                  full(wa), full(wb), full(wo), full(g2)],
        out_specs=[pl.BlockSpec((tm, D_MODEL), lambda i: (i, 0)),
                   pl.BlockSpec((tm, D_MODEL), lambda i: (i, 0))],
        out_shape=[jax.ShapeDtypeStruct((n, D_MODEL), F32), jax.ShapeDtypeStruct((n, D_MODEL), BF16)],
        compiler_params=_cparams(("parallel",)),
        name="merge",
    )(y_ssd, y_att, proj, proj, x2d, wa, wb, wo, g2)


def _top_values(cur, k):
    vals = []
    for _ in range(k):
        m = jnp.max(cur, axis=0, keepdims=True)
        vals.append(m)
        cur = jnp.where(cur >= m, NEG_INF, cur)
    return vals


def _peer_select_kernel(h_ref, wq_ref, keys_ref, s1_ref, c1_ref, s2_ref, e2_ref, tau_ref):
    q = _dot(h_ref[...], wq_ref[...]).astype(BF16)
    for hd in range(PEER_HEADS):
        base = hd * 2 * PEER_HALF
        s1 = _dot_nt(keys_ref[0, hd], q[:, base:base + PEER_HALF])
        s2 = _dot_nt(keys_ref[1, hd], q[:, base + PEER_HALF:base + 2 * PEER_HALF])
        v1 = _top_values(s1, PEER_TOPK)
        v2 = jnp.concatenate(_top_values(s2, PEER_TOPK), axis=0)
        riota = lax.broadcasted_iota(I32, (SUBLANES, s1.shape[1]), 0)
        blocks = [v1[0] + v2, v1[1] + v2[:SUBLANES]]
        for i in range(2, SUBLANES):
            blocks.append(jnp.where(riota < PEER_TOPK // (i + 1), v1[i] + v2[:SUBLANES], NEG_INF))
        blocks.append(jnp.concatenate(v1[SUBLANES:], axis=0) + v2[0:1])
        cand = jnp.concatenate(blocks, axis=0)
        top = _top_values(cand, PEER_TOPK)
        zsum = sum(jnp.exp(t - top[0]) for t in top)
        c1 = jnp.exp(s1 - v1[0]) / zsum
        cw = s1_ref.shape[-1]
        for ch in range(s1_ref.shape[1]):
            s1_ref[hd, ch] = s1[:, ch * cw:(ch + 1) * cw]
            c1_ref[hd, ch] = c1[:, ch * cw:(ch + 1) * cw]
        s2_ref[hd] = s2
        e2_ref[hd] = jnp.exp(s2 - v2[0:1])
        tau_ref[hd] = top[PEER_TOPK - 1]


def _peer_select(hb, wq, keys, tn):
    n = hb.shape[0]
    big = jax.ShapeDtypeStruct((PEER_HEADS, PEER_N_KEYS, n), F32)
    bspec = pl.BlockSpec((PEER_HEADS, PEER_N_KEYS, tn), lambda i: (0, 0, i))
    cw = min(LANES, tn)
    chunked = jax.ShapeDtypeStruct((PEER_HEADS, n // cw, PEER_N_KEYS, cw), F32)
    cspec = pl.BlockSpec((PEER_HEADS, tn // cw, PEER_N_KEYS, cw), lambda i: (0, i, 0, 0))
    return pl.pallas_call(
        _peer_select_kernel,
        grid=(n // tn,),
        in_specs=[pl.BlockSpec((tn, D_MODEL), lambda i: (i, 0)),
                  pl.BlockSpec(wq.shape, lambda i: (0, 0)),
                  pl.BlockSpec(keys.shape, lambda i: (0, 0, 0, 0))],
        out_specs=[cspec, cspec, bspec, bspec, pl.BlockSpec((PEER_HEADS, 1, tn), lambda i: (0, 0, i))],
        out_shape=[chunked, chunked, big, big, jax.ShapeDtypeStruct((PEER_HEADS, 1, n), F32)],
        compiler_params=_cparams(("parallel",)),
        name="peer_select",
    )(hb, wq, keys)


def _gelu_tanh(x):
    return 0.5 * x * (1.0 + jnp.tanh(math.sqrt(2.0 / math.pi) * (x + 0.044715 * (x * x * x))))


PEER_EC = 1024
PEER_SUB = 512


def _peer_mix_kernel(h_ref, u_ref, vt_ref, s1_ref, c1_ref, s2_ref, e2_ref, tau_ref, x1_ref, y_ref,
                     acc_ref, act_ref, g_ref, ht_ref):
    j = pl.program_id(1)

    @pl.when(j == 0)
    def _():
        acc_ref[...] = jnp.zeros_like(acc_ref)
        ht_ref[...] = h_ref[...].T

    tn = act_ref.shape[1]
    cw = min(LANES, tn)

    def gate_chunk(sc):
        start = sc * PEER_SUB
        erows = pl.ds(start if isinstance(sc, int) else pl.multiple_of(start, PEER_SUB), PEER_SUB)
        act_ref[...] = _gelu_tanh(_dot(u_ref[erows, :], ht_ref[...]))
        slot = sc % 2
        for ii in range(PEER_SUB // PEER_N_KEYS):
            i1 = j * (PEER_EC // PEER_N_KEYS) + sc * (PEER_SUB // PEER_N_KEYS) + ii
            rows = slice(ii * PEER_N_KEYS, (ii + 1) * PEER_N_KEYS)
            for ch in range(tn // cw):
                cols = slice(ch * cw, (ch + 1) * cw)
                grp = (PEER_N_KEYS // SUBLANES, SUBLANES, cw)
                w = jnp.zeros(grp, F32)
                for hd in range(PEER_HEADS):
                    s1b = jnp.broadcast_to(s1_ref[hd, ch, pl.ds(i1, 1), :], (SUBLANES, cw))
                    c1b = jnp.broadcast_to(c1_ref[hd, ch, pl.ds(i1, 1), :], (SUBLANES, cw))
                    taub = jnp.broadcast_to(tau_ref[hd, :, cols], (SUBLANES, cw))
                    chosen = (s2_ref[hd, :, cols].reshape(grp) + s1b) >= taub
                    w = w + jnp.where(chosen, e2_ref[hd, :, cols].reshape(grp) * c1b, 0.0)
                g = act_ref[rows, cols].reshape(grp) * w
                g_ref[slot, rows, cols] = g.reshape(PEER_N_KEYS, cw).astype(BF16)

    def project_chunk(sc):
        acc_ref[...] += _dot(vt_ref[sc], g_ref[sc % 2])

    gate_chunk(0)

    def pipelined(sc, _):
        project_chunk(sc - 1)
        gate_chunk(sc)
        return 0

    nsub = PEER_EC // PEER_SUB
    lax.fori_loop(1, nsub, pipelined, 0)
    project_chunk(nsub - 1)

    @pl.when(j == pl.num_programs(1) - 1)
    def _():
        y_ref[...] = x1_ref[...] + acc_ref[...].T


def _peer_mix(hb, ub, vtb, sel, x1, tn):
    n = hb.shape[0]
    s1, c1, s2, e2, tau = sel
    bspec = pl.BlockSpec((PEER_HEADS, PEER_N_KEYS, tn), lambda i, j: (0, 0, i))
    cw = min(LANES, tn)
    cspec = pl.BlockSpec((PEER_HEADS, tn // cw, PEER_N_KEYS, cw), lambda i, j: (0, i, 0, 0))
    return pl.pallas_call(
        _peer_mix_kernel,
        grid=(n // tn, PEER_EXPERTS // PEER_EC),
        in_specs=[pl.BlockSpec((tn, D_MODEL), lambda i, j: (i, 0)),
                  pl.BlockSpec((PEER_EC, D_MODEL), lambda i, j: (j, 0)),
                  pl.BlockSpec((PEER_EC // PEER_SUB, D_MODEL, PEER_SUB), lambda i, j: (j, 0, 0)),
                  cspec, cspec, bspec, bspec,
                  pl.BlockSpec((PEER_HEADS, 1, tn), lambda i, j: (0, 0, i)),
                  pl.BlockSpec((tn, D_MODEL), lambda i, j: (i, 0))],
        out_specs=pl.BlockSpec((tn, D_MODEL), lambda i, j: (i, 0)),
        out_shape=jax.ShapeDtypeStruct((n, D_MODEL), F32),
        scratch_shapes=[pltpu.VMEM((D_MODEL, tn), F32),
                        pltpu.VMEM((PEER_SUB, tn), F32),
                        pltpu.VMEM((2, PEER_SUB, tn), BF16),
                        pltpu.VMEM((D_MODEL, tn), BF16)],
        compiler_params=_cparams(("parallel", "arbitrary")),
        name="peer_mix",
    )(hb, ub, vtb, s1, c1, s2, e2, tau, x1)


HALO = 8
N_PAIRS = SSD_HEADS // 2
GROUP_W = D_INNER // SSD_GROUPS


def _softplus(x):
    return jnp.maximum(x, 0.0) + jnp.log1p(jnp.exp(-jnp.abs(x)))


def _silu(x):
    return x * jax.nn.sigmoid(x)


def _lane_bcast(col):
    return jnp.broadcast_to(col, (col.shape[0], LANES))


def _ssd_gate_norm(y, xs, z, dvec, gain):
    y = (y + dvec * xs) * _silu(z)
    parts = []
    for g in range(SSD_GROUPS):
        yg = y[:, g * GROUP_W:(g + 1) * GROUP_W]
        ms = jnp.mean(yg * yg, axis=-1, keepdims=True)
        parts.append(yg * lax.rsqrt(ms + EPS))
    return jnp.concatenate(parts, axis=1) * gain


def _ssd_prompt_kernel(xbc_ref, z_ref, sm_ref, cw_ref, cb_ref, dtb_ref, alog_ref, dvec_ref, gain_ref, tri_ref,
                       y_ref, hout_ref, ext_ref, h_ref, yacc_ref):
    c = pl.program_id(1)

    @pl.when(c == 0)
    def _():
        ext_ref[0:HALO, :] = jnp.zeros((HALO, CONV_DIM), F32)
        h_ref[...] = jnp.zeros_like(h_ref)

    x = xbc_ref[...]
    ext_ref[HALO:HALO + SSD_CHUNK, :] = x
    conv = cb_ref[...]
    for j in range(CONV_W):
        conv = conv + cw_ref[j:j + 1, :] * ext_ref[pl.ds(HALO - (CONV_W - 1) + j, SSD_CHUNK), :]
    ext_ref[0:HALO, :] = x[SSD_CHUNK - HALO:, :]
    xc = _silu(conv)
    xs = xc[:, :D_INNER]

    lane = lax.broadcasted_iota(I32, (SSD_CHUNK, LANES), 1)
    row = lax.broadcasted_iota(I32, (SSD_CHUNK, LANES), 0)
    dt = jnp.where(lane < SSD_HEADS, _softplus(sm_ref[...] + dtb_ref[...]), 0.0)
    da = dt * (-jnp.exp(alog_ref[...]))
    hi, mid, lo = _split3(da)
    tri = tri_ref[...]
    acum = _dot(tri, hi) + _dot(tri, mid) + _dot(tri, lo)
    acum_t = acum.T
    dt_t = dt.T
    alast = acum[SSD_CHUNK - 1:SSD_CHUNK, :]
    e_in = jnp.exp(acum)
    e_out = jnp.exp(alast - acum) * dt
    e_all = jnp.exp(alast)
    causal = row >= lane
    lo_half = lane < SSD_HEAD_DIM

    for g in range(SSD_GROUPS):
        bg = xc[:, D_INNER + g * D_STATE:D_INNER + (g + 1) * D_STATE].astype(BF16)
        cg = xc[:, D_INNER + SSD_GROUPS * D_STATE + g * D_STATE:
                D_INNER + SSD_GROUPS * D_STATE + (g + 1) * D_STATE].astype(BF16)
        cbm = _dot_nt(cg, bg)
        for pp in range(2):
            pair = 2 * g + pp
            r0 = 2 * pair
            xp = xs[:, r0 * SSD_HEAD_DIM:(r0 + 2) * SSD_HEAD_DIM]
            yp = jnp.zeros((SSD_CHUNK, LANES), F32)
            for hh in range(2):
                r = r0 + hh
                seg = _lane_bcast(acum[:, r:r + 1]) - acum_t[r:r + 1, :]
                lmat = jnp.exp(jnp.where(causal, seg, NEG_INF))
                m = (cbm * lmat * dt_t[r:r + 1, :]).astype(BF16)
                xm = jnp.where(lo_half == (hh == 0), xp, 0.0).astype(BF16)
                yp = yp + _dot(m, xm)
            hp = h_ref[pair]
            scale_in = jnp.where(lo_half, _lane_bcast(e_in[:, r0:r0 + 1]), _lane_bcast(e_in[:, r0 + 1:r0 + 2]))
            yp = yp + _dot_nt(cg, hp.astype(BF16)) * scale_in
            scale_out = jnp.where(lo_half, _lane_bcast(e_out[:, r0:r0 + 1]), _lane_bcast(e_out[:, r0 + 1:r0 + 2]))
            xd = (xp * scale_out).astype(BF16)
            hdec = jnp.where(row < SSD_HEAD_DIM, e_all[:, r0:r0 + 1], e_all[:, r0 + 1:r0 + 2])
            h_ref[pair] = hp * hdec + _dot_tn(xd, bg)
            yacc_ref[:, r0 * SSD_HEAD_DIM:(r0 + 2) * SSD_HEAD_DIM] = yp

    y_ref[...] = _ssd_gate_norm(yacc_ref[...], xs, z_ref[...], dvec_ref[...], gain_ref[...])

    @pl.when(c == pl.num_programs(1) - 1)
    def _():
        hout_ref[0] = h_ref[...]


def _pad_lanes(v):
    return jnp.zeros((1, LANES), F32).at[0, :v.shape[0]].set(v)


def _ssd_prompt(proj, bsz, conv_w, conv_b, dt_bias, a_log, d_skip, ssd_norm_g):
    n = proj.shape[0]
    nc = n // bsz // SSD_CHUNK
    tri = jnp.asarray(np.tril(np.ones((SSD_CHUNK, SSD_CHUNK), np.float32)), BF16)
    dvec = jnp.repeat(d_skip, SSD_HEAD_DIM).reshape(1, D_INNER)
    small = [conv_w, conv_b.reshape(1, CONV_DIM), _pad_lanes(dt_bias), _pad_lanes(a_log), dvec,
             ssd_norm_g.reshape(1, D_INNER), tri]

    def full(a):
        return pl.BlockSpec(a.shape, lambda b, c: (0,) * a.ndim)

    return pl.pallas_call(
        _ssd_prompt_kernel,
        grid=(bsz, nc),
        in_specs=[pl.BlockSpec((SSD_CHUNK, CONV_DIM), lambda b, c: (b * nc + c, 0)),
                  pl.BlockSpec((SSD_CHUNK, D_INNER), lambda b, c: (b * nc + c, COL_Z // D_INNER)),
                  pl.BlockSpec((SSD_CHUNK, LANES), lambda b, c: (b * nc + c, COL_SMALL // LANES))]
                 + [full(a) for a in small],
        out_specs=[pl.BlockSpec((SSD_CHUNK, D_INNER), lambda b, c: (b * nc + c, 0)),
                   pl.BlockSpec((1, N_PAIRS, LANES, D_STATE), lambda b, c: (b, 0, 0, 0))],
        out_shape=[jax.ShapeDtypeStruct((n, D_INNER), F32),
                   jax.ShapeDtypeStruct((bsz, N_PAIRS, LANES, D_STATE), F32)],
        scratch_shapes=[pltpu.VMEM((HALO + SSD_CHUNK, CONV_DIM), F32),
                        pltpu.VMEM((N_PAIRS, LANES, D_STATE), F32),
                        pltpu.VMEM((SSD_CHUNK, D_INNER), F32)],
        compiler_params=_cparams(("parallel", "arbitrary")),
        name="ssd_prompt",
    )(proj, proj, proj, *small)


def _t5_bucket_np(dist):
    n = np.maximum(dist, 0)
    max_exact = NUM_BUCKETS // 2
    nf = np.maximum(n, 1).astype(np.float32)
    ratio = np.log(nf / np.float32(max_exact)) / np.float32(math.log(MAX_DISTANCE / max_exact))
    large = max_exact + (ratio * np.float32(NUM_BUCKETS - max_exact)).astype(np.int32)
    large = np.minimum(large, NUM_BUCKETS - 1)
    return np.where(n < max_exact, n, large).astype(np.int32)


BIAS_TJ = 2048


def _bias_kernel(rel_t_ref, bucket_ref, o_ref):
    ids = lax.broadcasted_iota(I32, (NUM_BUCKETS, BIAS_TJ), 0)
    onehot = jnp.where(ids == bucket_ref[...], 1.0, 0.0).astype(BF16)
    o_ref[...] = _exact_dot(rel_t_ref[...], onehot)


def _bias_lookup(rel_bias, buckets):
    j = buckets.shape[0]
    return pl.pallas_call(
        _bias_kernel,
        grid=(j // BIAS_TJ,),
        in_specs=[pl.BlockSpec((ATT_HEADS, NUM_BUCKETS), lambda i: (0, 0)),
                  pl.BlockSpec((1, BIAS_TJ), lambda i: (0, i))],
        out_specs=pl.BlockSpec((ATT_HEADS, BIAS_TJ), lambda i: (0, i)),
        out_shape=jax.ShapeDtypeStruct((ATT_HEADS, j), F32),
        compiler_params=_cparams(("parallel",)),
        name="bias_lookup",
    )(rel_bias.T, jnp.asarray(buckets.reshape(1, j)))


def _prompt_bias_buckets():
    i = np.arange(Q_BLOCK)[:, None]
    j = np.arange(Q_BLOCK)[None, :]
    far = np.full((Q_BLOCK, Q_BLOCK), 2 * Q_BLOCK)
    return _t5_bucket_np(np.stack([far, i - j + Q_BLOCK, i - j]))


KEY_TILES = 4
KEY_CHUNK = KEY_TILES * Q_BLOCK


def _fold_tiles(x, op):
    out = x[:, 0:LANES]
    for u in range(1, x.shape[1] // LANES):
        out = op(out, x[:, u * LANES:(u + 1) * LANES])
    return out


def _sortable_key(x):
    b = pltpu.bitcast(x, I32)
    return jnp.where(b < 0, b ^ 0x7FFFFFFF, b)


def _kth_largest_key(count_ge, k, shape):
    def body(it, cand):
        trial = cand + jnp.left_shift(jnp.int32(1), 31 - it)
        return jnp.where(count_ge(trial) >= k, trial, cand)
    return lax.fori_loop(0, 32, body, jnp.full(shape, INT_MIN, I32))


def _select_keys(n_sel, qb, iq_ref, iw_ref, ik_ref, triu_ref, key_ref, mask_ref):
    nchunk = qb // KEY_TILES + 1
    row = lax.broadcasted_iota(I32, (Q_BLOCK, KEY_CHUNK), 0)
    lane = lax.broadcasted_iota(I32, (Q_BLOCK, KEY_CHUNK), 1)

    def admissible(c):
        return lane + (c * KEY_CHUNK - qb * Q_BLOCK) <= row

    iw = iw_ref[...]
    iq = iq_ref[...]
    wcols = [jnp.broadcast_to(iw[:, SMALL_IW + h:SMALL_IW + h + 1], (Q_BLOCK, KEY_CHUNK)) for h in range(IDX_HEADS)]
    iqs = [iq[:, h * IDX_DIM:(h + 1) * IDX_DIM] for h in range(IDX_HEADS)]

    def score_chunk(c, _):
        ikc = ik_ref[pl.ds(pl.multiple_of(c * KEY_CHUNK, KEY_CHUNK), KEY_CHUNK), :]
        sc = jnp.zeros((Q_BLOCK, KEY_CHUNK), F32)
        for h in range(IDX_HEADS):
            sc = sc + wcols[h] * jnp.maximum(_dot_nt(iqs[h], ikc), 0.0)
        key_ref[c] = _sortable_key(jnp.where(admissible(c), sc, NEG_INF))
        return 0

    lax.fori_loop(0, nchunk, score_chunk, 0)

    def count_ge(trial):
        def body(c, acc):
            return acc + _fold_tiles(jnp.where(key_ref[c] >= trial, 1, 0), jnp.add)
        acc = lax.fori_loop(0, nchunk, body, jnp.zeros((Q_BLOCK, LANES), I32))
        return jnp.sum(acc, axis=1, keepdims=True)

    tau = _kth_largest_key(count_ge, n_sel, (Q_BLOCK, 1))

    def count_gt(c, acc):
        return acc + _fold_tiles(jnp.where(key_ref[c] > tau, 1, 0), jnp.add)
    n_gt = jnp.sum(lax.fori_loop(0, nchunk, count_gt, jnp.zeros((Q_BLOCK, LANES), I32)), axis=1, keepdims=True)
    need = (n_sel - n_gt).astype(F32)

    def mask_chunk(c, seen):
        k = key_ref[c]
        eq = k == tau
        eqf = jnp.where(eq, 1.0, 0.0)
        pref = _dot(eqf.astype(BF16), triu_ref[...]) + seen
        sel = jnp.logical_or(k > tau, jnp.logical_and(eq, pref <= need))
        mask_ref[c] = jnp.where(jnp.logical_and(sel, admissible(c)), 0.0, NEG_INF)
        return seen + jnp.sum(eqf, axis=1, keepdims=True)

    lax.fori_loop(0, nchunk, mask_chunk, jnp.zeros((Q_BLOCK, 1), F32))


def _attn_prompt_kernel(n_sel, qp_ref, iq_ref, iw_ref, klo_ref, khi_ref, vlo_ref, vhi_ref, ik_ref, bt_ref, triu_ref,
                        y_ref, key_ref, mask_ref, lbuf_ref):
    qb = pl.program_id(1)
    pair = pl.program_id(2)
    nchunk = qb // KEY_TILES + 1

    @pl.when(pair == 0)
    def _():
        _select_keys(n_sel, qb, iq_ref, iw_ref, ik_ref, triu_ref, key_ref, mask_ref)

    g = pair // (ATT_HEADS // KV_HEADS // 2)
    qp = qp_ref[...]
    k_refs = (klo_ref, khi_ref)
    v_refs = (vlo_ref, vhi_ref)

    def logits_chunk(c, mxs):
        rows = pl.ds(pl.multiple_of(c * KEY_CHUNK, KEY_CHUNK), KEY_CHUNK)
        kinds = [jnp.clip(c * KEY_TILES + u - qb + 2, 0, 2) for u in range(KEY_TILES)]
        mask = mask_ref[c]
        new = []
        for half in range(2):
            bias = jnp.concatenate([bt_ref[2 * pair + half, kd] for kd in kinds], axis=1)
            lg = _dot_nt(qp, k_refs[half][g, rows, :]) + bias + mask
            lbuf_ref[half, c] = lg
            new.append(jnp.maximum(mxs[half], _fold_tiles(lg, jnp.maximum)))
        return tuple(new)

    neg = jnp.full((Q_BLOCK, LANES), NEG_INF, F32)
    mxs = lax.fori_loop(0, nchunk, logits_chunk, (neg, neg))
    mrows = [jnp.max(m, axis=1, keepdims=True) for m in mxs]

    def pv_chunk(c, carry):
        rows = pl.ds(pl.multiple_of(c * KEY_CHUNK, KEY_CHUNK), KEY_CHUNK)
        new = []
        for half in range(2):
            lsum, acc = carry[half]
            p = jnp.exp(lbuf_ref[half, c] - mrows[half])
            new.append((lsum + _fold_tiles(p, jnp.add), acc + _dot(p.astype(BF16), v_refs[half][g, rows, :])))
        return tuple(new)

    zero = jnp.zeros((Q_BLOCK, LANES), F32)
    res = lax.fori_loop(0, nchunk, pv_chunk, ((zero, zero), (zero, zero)))
    y_ref[...] = sum(acc / jnp.sum(lsum, axis=1, keepdims=True) for lsum, acc in res)


def _attn_prompt(prep, btile, bsz):
    qn, _, _, _, klo, khi, vlo, vhi, ikb, iqb, iw = prep
    n = qn.shape[0]
    t = n // bsz
    nqb = t // Q_BLOCK
    assert t % KEY_CHUNK == 0
    nch = t // KEY_CHUNK
    n_sel = min(TOPK_KEYS_MAX, t // 4)
    triu = jnp.asarray(np.triu(np.ones((KEY_CHUNK, KEY_CHUNK), np.float32)), BF16)
    kvspec = pl.BlockSpec((KV_HEADS, t, LANES), lambda b, q, p: (0, b, 0))
    return pl.pallas_call(
        functools.partial(_attn_prompt_kernel, n_sel),
        grid=(bsz, nqb, ATT_HEADS // 2),
        in_specs=[pl.BlockSpec((Q_BLOCK, LANES), lambda b, q, p: (b * nqb + q, p)),
                  pl.BlockSpec((Q_BLOCK, IDX_HEADS * IDX_DIM), lambda b, q, p: (b * nqb + q, 0)),
                  pl.BlockSpec((Q_BLOCK, LANES), lambda b, q, p: (b * nqb + q, 0)),
                  kvspec, kvspec, kvspec, kvspec,
                  pl.BlockSpec((t, IDX_DIM), lambda b, q, p: (b, 0)),
                  pl.BlockSpec(btile.shape, lambda b, q, p: (0, 0, 0, 0)),
                  pl.BlockSpec(triu.shape, lambda b, q, p: (0, 0))],
        out_specs=pl.BlockSpec((Q_BLOCK, LANES), lambda b, q, p: (b * nqb + q, p)),
        out_shape=jax.ShapeDtypeStruct((n, ATT_WIDTH), F32),
        scratch_shapes=[pltpu.VMEM((nch, Q_BLOCK, KEY_CHUNK), I32),
                        pltpu.VMEM((nch, Q_BLOCK, KEY_CHUNK), F32),
                        pltpu.VMEM((2, nch, Q_BLOCK, KEY_CHUNK), F32)],
        compiler_params=_cparams(("parallel", "arbitrary", "arbitrary")),
        name="attn_prompt",
    )(qn, iqb, iw, klo, khi, vlo, vhi, ikb, btile, triu)


def _head_expand_indicator():
    m = np.zeros((LANES, D_INNER), np.float32)
    m[np.arange(D_INNER) // SSD_HEAD_DIM, np.arange(D_INNER)] = 1.0
    return m


def _ssd_step_kernel(xbc_ref, z_ref, sm_ref, sc_ref, hin_ref, cw_ref, cb_ref, dtb_ref, alog_ref, dvec_ref, gain_ref,
                     exp_ref, y_ref, hout_ref, cs_ref, xs_ref, bm_ref, cm_ref, dtx_t_ref, dec_t_ref, y_t_ref):
    b = pl.program_id(0)
    db = xbc_ref.shape[0]

    @pl.when(b == 0)
    def _():
        x = xbc_ref[...]
        sc = sc_ref[...]
        conv = cb_ref[...] + cw_ref[CONV_W - 1:CONV_W, :] * x
        for j in range(CONV_W - 1):
            conv = conv + cw_ref[j:j + 1, :] * sc[:, j * CONV_DIM:(j + 1) * CONV_DIM]
        cs_ref[:, :(CONV_W - 2) * CONV_DIM] = sc[:, CONV_DIM:]
        cs_ref[:, (CONV_W - 2) * CONV_DIM:] = x
        xc = _silu(conv)
        xs = xc[:, :D_INNER]
        lane = lax.broadcasted_iota(I32, (db, LANES), 1)
        dt = jnp.where(lane < SSD_HEADS, _softplus(sm_ref[...] + dtb_ref[...]), 0.0)
        dec = jnp.exp(dt * (-jnp.exp(alog_ref[...])))
        xs_ref[0:db, :] = xs
        bm_ref[0:db, :] = xc[:, D_INNER:D_INNER + SSD_GROUPS * D_STATE]
        cm_ref[0:db, :] = xc[:, D_INNER + SSD_GROUPS * D_STATE:]
        dtx_t_ref[:, 0:db] = (_exact_dot(dt, exp_ref[...]) * xs).T
        dec_t_ref[:, 0:db] = _exact_dot(dec, exp_ref[...]).T
        y_t_ref[...] = jnp.zeros_like(y_t_ref)

    lane = lax.broadcasted_iota(I32, (LANES, LANES), 1)
    mine = lane == b
    base = pl.multiple_of((b // 8) * 8, 8)
    my_row = lax.broadcasted_iota(I32, (8, SSD_GROUPS * D_STATE), 0) == (b % 8)
    bm_b = jnp.sum(jnp.where(my_row, bm_ref[pl.ds(base, 8), :], 0.0), axis=0, keepdims=True)
    cm_b = jnp.sum(jnp.where(my_row, cm_ref[pl.ds(base, 8), :], 0.0), axis=0, keepdims=True)
    for pair in range(N_PAIRS):
        g = pair // 2
        rows = slice(pair * LANES, (pair + 1) * LANES)
        xcol = jnp.sum(jnp.where(mine, dtx_t_ref[rows, :], 0.0), axis=1, keepdims=True)
        dcol = jnp.sum(jnp.where(mine, dec_t_ref[rows, :], 0.0), axis=1, keepdims=True)
        brow = bm_b[:, g * D_STATE:(g + 1) * D_STATE]
        crow = cm_b[:, g * D_STATE:(g + 1) * D_STATE]
        hnew = hin_ref[0, pair] * dcol + xcol * brow
        hout_ref[0, pair] = hnew
        ycol = jnp.sum(hnew * crow, axis=1, keepdims=True)
        y_t_ref[rows, :] = jnp.where(mine, ycol, y_t_ref[rows, :])

    @pl.when(b == pl.num_programs(0) - 1)
    def _():
        y = y_t_ref[...].T[0:db, :]
        y_ref[...] = _ssd_gate_norm(y, xs_ref[0:db, :], z_ref[...], dvec_ref[...], gain_ref[...])


def _ssd_step(proj, state_conv2d, state_ssm4d, conv_w, conv_b, dt_bias, a_log, d_skip, ssd_norm_g):
    db = proj.shape[0]
    assert db <= LANES and db % 8 == 0
    dvec = jnp.repeat(d_skip, SSD_HEAD_DIM).reshape(1, D_INNER)
    small = [conv_w, conv_b.reshape(1, CONV_DIM), _pad_lanes(dt_bias), _pad_lanes(a_log), dvec,
             ssd_norm_g.reshape(1, D_INNER), jnp.asarray(_head_expand_indicator(), BF16)]

    def full(a):
        return pl.BlockSpec(a.shape, lambda b: (0,) * a.ndim)

    cs_w = (CONV_W - 1) * CONV_DIM
    hspec = pl.BlockSpec((1, N_PAIRS, LANES, D_STATE), lambda b: (b, 0, 0, 0))
    return pl.pallas_call(
        _ssd_step_kernel,
        grid=(db,),
        in_specs=[pl.BlockSpec((db, CONV_DIM), lambda b: (0, 0)),
                  pl.BlockSpec((db, D_INNER), lambda b: (0, COL_Z // D_INNER)),
                  pl.BlockSpec((db, LANES), lambda b: (0, COL_SMALL // LANES)),
                  pl.BlockSpec((db, cs_w), lambda b: (0, 0)),
                  hspec] + [full(a) for a in small],
        out_specs=[pl.BlockSpec((db, D_INNER), lambda b: (0, 0)),
                   hspec,
                   pl.BlockSpec((db, cs_w), lambda b: (0, 0))],
        out_shape=[jax.ShapeDtypeStruct((db, D_INNER), F32),
                   jax.ShapeDtypeStruct(state_ssm4d.shape, F32),
                   jax.ShapeDtypeStruct((db, cs_w), F32)],
        scratch_shapes=[pltpu.VMEM((LANES, D_INNER), F32),
                        pltpu.VMEM((LANES, SSD_GROUPS * D_STATE), F32),
                        pltpu.VMEM((LANES, SSD_GROUPS * D_STATE), F32),
                        pltpu.VMEM((D_INNER, LANES), F32),
                        pltpu.VMEM((D_INNER, LANES), F32),
                        pltpu.VMEM((D_INNER, LANES), F32)],
        compiler_params=_cparams(("arbitrary",)),
        name="ssd_step",
    )(proj, proj, proj, state_conv2d, state_ssm4d, *small)


def _page_copies(pt_ref, seq, cache_ref, buf_ref, slot, sem_ref, n_pages):
    return [pltpu.make_async_copy(cache_ref.at[pt_ref[seq, p]], buf_ref.at[slot, p], sem_ref.at[slot])
            for p in range(n_pages)]


def _paged_fetch(pt_ref, streams, n_pages):
    b = pl.program_id(0)
    slot = b % 2

    @pl.when(b == 0)
    def _():
        for cache_ref, buf_ref, sem_ref in streams:
            for cp in _page_copies(pt_ref, 0, cache_ref, buf_ref, 0, sem_ref, n_pages):
                cp.start()

    @pl.when(b + 1 < pl.num_programs(0))
    def _():
        for cache_ref, buf_ref, sem_ref in streams:
            for cp in _page_copies(pt_ref, b + 1, cache_ref, buf_ref, 1 - slot, sem_ref, n_pages):
                cp.start()

    for cache_ref, buf_ref, sem_ref in streams:
        for cp in _page_copies(pt_ref, b, cache_ref, buf_ref, slot, sem_ref, n_pages):
            cp.wait()
    return slot


def _dec_score_kernel(n_pages, pt_ref, iq_ref, w_ref, iknew_ref, cache_ref, o_ref, buf_ref, sem_ref):
    b = pl.program_id(0)
    slot = _paged_fetch(pt_ref, [(cache_ref, buf_ref, sem_ref)], n_pages)
    past = n_pages * PAGE_SIZE
    iq = iq_ref[...].astype(BF16)
    w = w_ref[...]
    for p in range(n_pages):
        s = _dot(iq, buf_ref[slot, p].astype(BF16))
        o_ref[0, :, p * PAGE_SIZE:(p + 1) * PAGE_SIZE] = jnp.sum(w * jnp.maximum(s, 0.0), axis=0, keepdims=True)
    iknew = iknew_ref[pl.ds(b, 1), :].astype(BF16).astype(F32)
    s_new = jnp.sum(iq.astype(F32) * iknew, axis=1, keepdims=True)
    sc_new = jnp.sum(w * jnp.maximum(s_new, 0.0), axis=0, keepdims=True)
    lane = lax.broadcasted_iota(I32, (1, LANES), 1)
    o_ref[0, :, past:past + LANES] = jnp.where(lane == 0, sc_new, NEG_INF)


def _dec_scores(page_table, cache_ik, iq8, w8, ikb_new):
    db, n_pages = page_table.shape
    s_pad = (n_pages + 1) * PAGE_SIZE
    grid_spec = pltpu.PrefetchScalarGridSpec(
        num_scalar_prefetch=1,
        grid=(db,),
        in_specs=[pl.BlockSpec((IDX_HEADS, IDX_DIM), lambda b, pt: (b, 0)),
                  pl.BlockSpec((IDX_HEADS, 1), lambda b, pt: (b, 0)),
                  pl.BlockSpec(ikb_new.shape, lambda b, pt: (0, 0)),
                  pl.BlockSpec(memory_space=pl.ANY)],
        out_specs=pl.BlockSpec((1, 1, s_pad), lambda b, pt: (b, 0, 0)),
        scratch_shapes=[pltpu.VMEM((2, n_pages, IDX_DIM, PAGE_SIZE), F32),
                        pltpu.SemaphoreType.DMA((2,))],
    )
    return pl.pallas_call(
        functools.partial(_dec_score_kernel, n_pages),
        grid_spec=grid_spec,
        out_shape=jax.ShapeDtypeStruct((db, 1, s_pad), F32),
        compiler_params=_cparams(("arbitrary",)),
        name="dec_scores",
    )(page_table, iq8, w8, ikb_new, cache_ik)


def _dec_select_kernel(n_sel, sc_ref, triu_ref, mask_ref, key_ref):
    db, s_pad = sc_ref.shape
    ntile = s_pad // LANES
    key_ref[...] = _sortable_key(sc_ref[...])

    def count_ge(trial):
        acc = jnp.zeros((db, LANES), I32)
        for j in range(ntile):
            acc = acc + jnp.where(key_ref[:, j * LANES:(j + 1) * LANES] >= trial, 1, 0)
        return jnp.sum(acc, axis=1, keepdims=True)

    tau = _kth_largest_key(count_ge, n_sel, (db, 1))
    acc = jnp.zeros((db, LANES), I32)
    for j in range(ntile):
        acc = acc + jnp.where(key_ref[:, j * LANES:(j + 1) * LANES] > tau, 1, 0)
    need = (n_sel - jnp.sum(acc, axis=1, keepdims=True)).astype(F32)
    seen = jnp.zeros((db, 1), F32)
    for j in range(ntile):
        k = key_ref[:, j * LANES:(j + 1) * LANES]
        eq = k == tau
        pref = _dot(jnp.where(eq, 1.0, 0.0).astype(BF16), triu_ref[...]) + seen
        sel = jnp.logical_or(k > tau, jnp.logical_and(eq, pref <= need))
        mask_ref[:, j * LANES:(j + 1) * LANES] = jnp.where(sel, 0.0, NEG_INF)
        seen = pref[:, LANES - 1:LANES]


def _dec_select(scores2d, n_sel):
    db, s_pad = scores2d.shape
    triu = jnp.asarray(np.triu(np.ones((LANES, LANES), np.float32)), BF16)
    return pl.pallas_call(
        functools.partial(_dec_select_kernel, n_sel),
        grid=(1,),
        in_specs=[pl.BlockSpec((db, s_pad), lambda i: (0, 0)),
                  pl.BlockSpec(triu.shape, lambda i: (0, 0))],
        out_specs=pl.BlockSpec((db, s_pad), lambda i: (0, 0)),
        out_shape=jax.ShapeDtypeStruct((db, s_pad), F32),
        scratch_shapes=[pltpu.VMEM((db, s_pad), I32)],
        compiler_params=_cparams(("arbitrary",)),
        name="dec_select",
    )(scores2d, triu)


def _dec_attn_kernel(n_pages, pt_ref, q_ref, mask_ref, bias_ref, knew_ref, vnew_ref, ck_ref, cv_ref,
                     y_ref, kbuf_ref, vbuf_ref, ksem_ref, vsem_ref):
    b = pl.program_id(0)
    slot = _paged_fetch(pt_ref, [(ck_ref, kbuf_ref, ksem_ref), (cv_ref, vbuf_ref, vsem_ref)], n_pages)
    past = n_pages * PAGE_SIZE
    q = q_ref[...]
    hrow = lax.broadcasted_iota(I32, (ATT_HEADS, KV_WIDTH), 0)
    lane = lax.broadcasted_iota(I32, (ATT_HEADS, KV_WIDTH), 1)
    own = (hrow // (ATT_HEADS // KV_HEADS)) == (lane // ATT_HEAD_DIM)
    qbd = jnp.where(own, jnp.concatenate([q] * KV_HEADS, axis=1), jnp.zeros_like(q[:, :1]))
    knew = knew_ref[pl.ds(b, 1), :].astype(BF16).astype(F32)
    l_new = jnp.sum(qbd.astype(F32) * knew, axis=1, keepdims=True)
    vnew = vnew_ref[pl.ds(b, 1), :].astype(BF16).astype(F32)
    mask = mask_ref[0]
    lg = jnp.concatenate([_dot(qbd, kbuf_ref[slot, p].astype(BF16)) for p in range(n_pages)], axis=1)
    lg = lg + bias_ref[:, 0:past] + mask[:, 0:past]
    tl = lax.broadcasted_iota(I32, (ATT_HEADS, LANES), 1)
    lg_new = jnp.where(tl == 0, l_new, 0.0) + bias_ref[:, past:past + LANES] + mask[:, past:past + LANES]
    m = jnp.maximum(jnp.max(lg, axis=1, keepdims=True), jnp.max(lg_new, axis=1, keepdims=True))
    p = jnp.exp(lg - m)
    p_new = jnp.exp(lg_new - m)
    denom = jnp.sum(p, axis=1, keepdims=True) + jnp.sum(p_new, axis=1, keepdims=True)
    pn = p_new[:, 0:1].astype(BF16).astype(F32)
    pb = p.astype(BF16)
    out = pn * vnew
    for pg in range(n_pages):
        out = out + _dot_nt(pb[:, pg * PAGE_SIZE:(pg + 1) * PAGE_SIZE], vbuf_ref[slot, pg].astype(BF16))
    out = out / denom
    r16 = lax.broadcasted_iota(I32, (ATT_HEADS, ATT_HEAD_DIM), 0) // (ATT_HEADS // KV_HEADS)
    y = jnp.zeros((ATT_HEADS, ATT_HEAD_DIM), F32)
    for g in range(KV_HEADS):
        y = jnp.where(r16 == g, out[:, g * ATT_HEAD_DIM:(g + 1) * ATT_HEAD_DIM], y)
    y_ref[...] = y


def _dec_attn(page_table, q16, mask3d, bias_dec, kn_new, v_new, cache_k3d, cache_v3d):
    db, n_pages = page_table.shape
    s_pad = (n_pages + 1) * PAGE_SIZE
    grid_spec = pltpu.PrefetchScalarGridSpec(
        num_scalar_prefetch=1,
        grid=(db,),
        in_specs=[pl.BlockSpec((ATT_HEADS, ATT_HEAD_DIM), lambda b, pt: (b, 0)),
                  pl.BlockSpec((1, 1, s_pad), lambda b, pt: (b, 0, 0)),
                  pl.BlockSpec(bias_dec.shape, lambda b, pt: (0, 0)),
                  pl.BlockSpec(kn_new.shape, lambda b, pt: (0, 0)),
                  pl.BlockSpec(v_new.shape, lambda b, pt: (0, 0)),
                  pl.BlockSpec(memory_space=pl.ANY),
                  pl.BlockSpec(memory_space=pl.ANY)],
        out_specs=pl.BlockSpec((ATT_HEADS, ATT_HEAD_DIM), lambda b, pt: (b, 0)),
        scratch_shapes=[pltpu.VMEM((2, n_pages, KV_WIDTH, PAGE_SIZE), F32),
                        pltpu.VMEM((2, n_pages, KV_WIDTH, PAGE_SIZE), F32),
                        pltpu.SemaphoreType.DMA((2,)),
                        pltpu.SemaphoreType.DMA((2,))],
    )
    return pl.pallas_call(
        functools.partial(_dec_attn_kernel, n_pages),
        grid_spec=grid_spec,
        out_shape=jax.ShapeDtypeStruct((db * ATT_HEADS, ATT_HEAD_DIM), F32),
        compiler_params=_cparams(("arbitrary",)),
        name="dec_attn",
    )(page_table, q16, mask3d, bias_dec, kn_new, v_new, cache_k3d, cache_v3d)


def _channel_mixer(hb, x1, peer_wq, peer_keys, ub, vtb, tn):
    sel = _peer_select(hb, peer_wq, peer_keys, tn)
    return _peer_mix(hb, ub, vtb, sel, x1, tn)


def kernel(x_prompt, x_sample, cache_k, cache_v, cache_idx_k, state_ssm, state_conv, page_table, rel_bias, norm1_g, w_in, conv_w, conv_b, dt_bias, a_log, d_skip, ssd_norm_g, q_norm_g, k_norm_g, w_branch_a, w_branch_b, w_out, norm2_g, peer_wq, peer_keys, peer_u, peer_v):
    depth = w_in.shape[0]
    bsz, t, _ = x_prompt.shape
    db, ds, _ = x_sample.shape
    assert depth == 1 and ds == 1, "single layer, one new token per sequence"
    n = bsz * t
    n_pages = page_table.shape[1]
    past = n_pages * PAGE_SIZE
    l = 0

    w_perm = _permute_w_in(w_in[l])
    wa, wb, wo = w_branch_a[l].astype(BF16), w_branch_b[l].astype(BF16), w_out[l].astype(BF16)
    wq, keys = peer_wq[l].astype(BF16), peer_keys[l].astype(BF16)
    ub = peer_u[l].astype(BF16)
    vtb = jnp.transpose(peer_v[l].reshape(PEER_EXPERTS // PEER_SUB, PEER_SUB, D_MODEL), (0, 2, 1)).astype(BF16)
    ssd_w = (conv_w[l], conv_b[l], dt_bias[l], a_log[l], d_skip[l], ssd_norm_g[l])

    dec_dist = np.concatenate([past - np.arange(past), np.zeros(PAGE_SIZE, np.int64)])
    n_prompt_b = 3 * Q_BLOCK * Q_BLOCK
    buckets = np.concatenate([_prompt_bias_buckets().reshape(-1), _t5_bucket_np(dec_dist)])
    pad = (-buckets.shape[0]) % BIAS_TJ
    bias_all = _bias_lookup(rel_bias, np.concatenate([buckets, np.zeros(pad, np.int32)]))
    btile = bias_all[:, :n_prompt_b].reshape(ATT_HEADS, 3, Q_BLOCK, Q_BLOCK)
    bias_dec = bias_all[:, n_prompt_b:n_prompt_b + past + PAGE_SIZE]

    xp = x_prompt.reshape(n, D_MODEL)
    proj = _in_proj(xp, norm1_g[l], w_perm, min(1024, n))
    y_ssd, h_t = _ssd_prompt(proj, bsz, *ssd_w)
    prep = _attn_prep(proj, q_norm_g[l], k_norm_g[l], min(512, n))
    y_att = _attn_prompt(prep, btile, bsz)
    x1, hb = _merge(y_ssd, y_att, proj, xp, wa, wb, wo, norm2_g[l], min(512, n))
    y_prompt = _channel_mixer(hb, x1, wq, keys, ub, vtb, min(512, n))

    k_prompt = prep[1].reshape(1, bsz, t, KV_HEADS, ATT_HEAD_DIM)
    v_prompt = prep[2].reshape(1, bsz, t, KV_HEADS, ATT_HEAD_DIM)
    ik_prompt = prep[3].reshape(1, bsz, t, IDX_DIM)
    ssm_prompt = h_t.reshape(1, bsz, SSD_HEADS, SSD_HEAD_DIM, D_STATE)
    conv_prompt = proj.reshape(bsz, t, PROJ_COLS)[:, t - (CONV_W - 1):, :CONV_DIM][None]

    xs_ = x_sample.reshape(db, D_MODEL)
    proj_s = _in_proj(xs_, norm1_g[l], w_perm, db)
    y_ssd_s, h_s, conv_s = _ssd_step(proj_s, state_conv[l].reshape(db, (CONV_W - 1) * CONV_DIM),
                                     state_ssm[l].reshape(db, N_PAIRS, LANES, D_STATE), *ssd_w)
    prep_s = _attn_prep(proj_s, q_norm_g[l], k_norm_g[l], db)
    qn_s, kn_s, v_s, ik_s, _, _, _, _, _, _, iw_s = prep_s
    iq8 = proj_s[:, COL_IQ:COL_IQ + IDX_HEADS * IDX_DIM].reshape(db * IDX_HEADS, IDX_DIM)
    w8 = iw_s[:, SMALL_IW:SMALL_IW + IDX_HEADS].reshape(db * IDX_HEADS, 1)
    ikt_pages = jnp.transpose(cache_idx_k[l], (0, 2, 1))
    kt_pages = jnp.transpose(cache_k[l], (0, 2, 3, 1)).reshape(-1, KV_WIDTH, PAGE_SIZE)
    vt_pages = jnp.transpose(cache_v[l], (0, 2, 3, 1)).reshape(-1, KV_WIDTH, PAGE_SIZE)
    scores = _dec_scores(page_table, ikt_pages, iq8, w8, ik_s)
    n_sel = min(TOPK_KEYS_MAX, (past + ds) // 4)
    mask = _dec_select(scores.reshape(db, past + PAGE_SIZE), n_sel)
    y_att_s = _dec_attn(page_table, qn_s.reshape(db * ATT_HEADS, ATT_HEAD_DIM),
                        mask.reshape(db, 1, past + PAGE_SIZE), bias_dec, kn_s, v_s,
                        kt_pages, vt_pages)
    x1_s, hb_s = _merge(y_ssd_s, y_att_s.reshape(db, ATT_WIDTH), proj_s, xs_, wa, wb, wo, norm2_g[l], db)
    y_sample = _channel_mixer(hb_s, x1_s, wq, keys, ub, vtb, db)

    return (y_prompt.reshape(bsz, t, D_MODEL), y_sample.reshape(db, ds, D_MODEL),
            k_prompt, v_prompt, ik_prompt, ssm_prompt, conv_prompt,
            kn_s.reshape(1, db, ds, KV_HEADS, ATT_HEAD_DIM), v_s.reshape(1, db, ds, KV_HEADS, ATT_HEAD_DIM),
            ik_s.reshape(1, db, ds, IDX_DIM),
            h_s.reshape(1, db, SSD_HEADS, SSD_HEAD_DIM, D_STATE),
            conv_s.reshape(1, db, CONV_W - 1, CONV_DIM))
```

```python
import functools
import math

import numpy as np
import jax
import jax.numpy as jnp
from jax import lax
from jax.experimental import pallas as pl
from jax.experimental.pallas import tpu as pltpu

F32 = jnp.float32
BF16 = jnp.bfloat16
I32 = jnp.int32

D_MODEL = 1024
PAGE_SIZE = 128
D_INNER = 2048
SSD_HEAD_DIM = 64
SSD_HEADS = 32
SSD_GROUPS = 8
D_STATE = 128
CONV_W = 4
CONV_DIM = D_INNER + 2 * SSD_GROUPS * D_STATE
SSD_CHUNK = 128
ATT_HEADS = 16
ATT_HEAD_DIM = 64
ATT_WIDTH = ATT_HEADS * ATT_HEAD_DIM
KV_HEADS = 4
KV_WIDTH = KV_HEADS * ATT_HEAD_DIM
IDX_HEADS = 8
IDX_DIM = 64
TOPK_KEYS_MAX = 256
Q_BLOCK = 128
NUM_BUCKETS = 32
MAX_DISTANCE = 128
PEER_HEADS = 8
PEER_N_KEYS = 128
PEER_EXPERTS = PEER_N_KEYS * PEER_N_KEYS
PEER_HALF = 128
PEER_TOPK = 16
EPS = 1e-6

LANES = 128
SUBLANES = 8
VMEM_LIMIT_BYTES = 56 * 1024 * 1024

COL_XBC = 0
COL_Z = COL_XBC + CONV_DIM
COL_Q = COL_Z + D_INNER
COL_K = COL_Q + ATT_WIDTH
COL_V = COL_K + KV_WIDTH
COL_IQ = COL_V + KV_WIDTH
COL_GA = COL_IQ + IDX_HEADS * IDX_DIM
COL_GB = COL_GA + D_MODEL
COL_SMALL = COL_GB + D_MODEL
SMALL_DT = 0
SMALL_IK = SSD_HEADS
SMALL_IW = SSD_HEADS + IDX_DIM
PROJ_COLS = COL_SMALL + LANES
PROJ_TN = 1152

INT_MIN = -2 ** 31
NEG_INF = float("-inf")


def _cparams(sem):
    return pltpu.CompilerParams(dimension_semantics=sem, vmem_limit_bytes=VMEM_LIMIT_BYTES)


def _split3(x):
    hi = x.astype(BF16)
    r1 = x - hi.astype(F32)
    mid = r1.astype(BF16)
    lo = (r1 - mid.astype(F32)).astype(BF16)
    return hi, mid, lo


def _dot(a, b):
    return jnp.dot(a, b, preferred_element_type=F32)


def _dot_nt(a, b):
    return lax.dot_general(a, b, (((1,), (1,)), ((), ())), preferred_element_type=F32)


def _dot_tn(a, b):
    return lax.dot_general(a, b, (((0,), (0,)), ((), ())), preferred_element_type=F32)


def _exact_dot(x, onehot_bf16):
    hi, mid, lo = _split3(x)
    return _dot(hi, onehot_bf16) + _dot(mid, onehot_bf16) + _dot(lo, onehot_bf16)


def _in_proj_kernel(x_ref, g_ref, w_ref, o_ref, xn_ref):
    @pl.when(pl.program_id(1) == 0)
    def _():
        x = x_ref[...]
        ms = jnp.mean(x * x, axis=-1, keepdims=True)
        xn_ref[...] = (x * lax.rsqrt(ms + EPS) * g_ref[...]).astype(BF16)

    o_ref[...] = _dot(xn_ref[...], w_ref[...])


def _in_proj(x2d, g, w_perm, tm):
    n = x2d.shape[0]
    return pl.pallas_call(
        _in_proj_kernel,
        grid=(n // tm, PROJ_COLS // PROJ_TN),
        in_specs=[pl.BlockSpec((tm, D_MODEL), lambda i, j: (i, 0)),
                  pl.BlockSpec((1, D_MODEL), lambda i, j: (0, 0)),
                  pl.BlockSpec((D_MODEL, PROJ_TN), lambda i, j: (0, j))],
        out_specs=pl.BlockSpec((tm, PROJ_TN), lambda i, j: (i, j)),
        out_shape=jax.ShapeDtypeStruct((n, PROJ_COLS), F32),
        scratch_shapes=[pltpu.VMEM((tm, D_MODEL), BF16)],
        compiler_params=_cparams(("parallel", "arbitrary")),
        name="in_proj",
    )(x2d, g.reshape(1, D_MODEL), w_perm)


def _permute_w_in(w_in):
    offs = np.cumsum([0, D_INNER, CONV_DIM, SSD_HEADS, ATT_WIDTH, KV_WIDTH, KV_WIDTH,
                      IDX_HEADS * IDX_DIM, IDX_DIM, IDX_HEADS, D_MODEL, D_MODEL])
    z, xbc, dt, q, k, v, iq, ik, iw, ga, gb = [w_in[:, offs[i]:offs[i + 1]] for i in range(11)]
    pad = jnp.zeros((D_MODEL, LANES - SSD_HEADS - IDX_DIM - IDX_HEADS), w_in.dtype)
    return jnp.concatenate([xbc, z, q, k, v, iq, ga, gb, dt, ik, iw, pad], axis=1).astype(BF16)


def _seg_indicator(width, seg):
    m = np.zeros((width, LANES), np.float32)
    m[np.arange(width), np.arange(width) // seg] = 1.0
    return m


def _half_placement():
    lo = np.zeros((KV_WIDTH, KV_HEADS * LANES), np.float32)
    hi = np.zeros((KV_WIDTH, KV_HEADS * LANES), np.float32)
    c = np.arange(KV_WIDTH)
    g, d = c // ATT_HEAD_DIM, c % ATT_HEAD_DIM
    lo[c, g * LANES + d] = 1.0
    hi[c, g * LANES + ATT_HEAD_DIM + d] = 1.0
    return lo, hi


def _head_rms(x, ind, ind_t, gain):
    sq = x * x
    hi = sq.astype(BF16)
    lo = (sq - hi.astype(F32)).astype(BF16)
    ss = _dot(hi, ind) + _dot(lo, ind)
    r = lax.rsqrt(ss * (1.0 / ATT_HEAD_DIM) + EPS)
    rb = _exact_dot(r, ind_t)
    return x * rb * gain


def _attn_prep_kernel(q_ref, kv_ref, sm_ref, gq_ref, gk_ref, indq_ref, indqt_ref, indk_ref, indkt_ref,
                      plo_ref, phi_ref,
                      qn_ref, kn_ref, v_ref, ik_ref, klo_ref, khi_ref, vlo_ref, vhi_ref,
                      ikb_ref, iqb_ref, iw_ref):
    qn = _head_rms(q_ref[...], indq_ref[...], indqt_ref[...], gq_ref[...])
    qn_ref[...] = (qn * (ATT_HEAD_DIM ** -0.5)).astype(BF16)
    kv = kv_ref[...]
    kn = _head_rms(kv[:, :KV_WIDTH], indk_ref[...], indkt_ref[...], gk_ref[...])
    v = kv[:, KV_WIDTH:]
    kn_ref[...] = kn
    v_ref[...] = v
    knb = kn.astype(BF16)
    vb = v.astype(BF16)
    for src, place_ref, dst_ref in ((knb, plo_ref, klo_ref), (knb, phi_ref, khi_ref),
                                    (vb, plo_ref, vlo_ref), (vb, phi_ref, vhi_ref)):
        placed = _dot(src, place_ref[...]).astype(BF16)
        for g in range(KV_HEADS):
            dst_ref[g] = placed[:, g * LANES:(g + 1) * LANES]
    sm = sm_ref[...]
    ik = sm[:, SMALL_IK:SMALL_IK + IDX_DIM]
    ik_ref[...] = ik
    ikb_ref[...] = ik.astype(BF16)
    iw_ref[...] = sm * (IDX_HEADS ** -0.5 * IDX_DIM ** -0.5)


def _attn_prep(proj, q_norm_g, k_norm_g, tm):
    n = proj.shape[0]
    gq = jnp.tile(q_norm_g, ATT_HEADS).reshape(1, ATT_WIDTH)
    gk = jnp.tile(k_norm_g, KV_HEADS).reshape(1, KV_WIDTH)
    indq = _seg_indicator(ATT_WIDTH, ATT_HEAD_DIM)
    indk = _seg_indicator(KV_WIDTH, ATT_HEAD_DIM)
    plo, phi = _half_placement()
    consts = [jnp.asarray(a, BF16) for a in (indq, indq.T, indk, indk.T, plo, phi)]

    def full(a):
        return pl.BlockSpec(a.shape, lambda i: (0,) * a.ndim)

    outs = [
        ((n, ATT_WIDTH), BF16), ((n, KV_WIDTH), F32), ((n, KV_WIDTH), F32), ((n, IDX_DIM), F32),
        ((KV_HEADS, n, LANES), BF16), ((KV_HEADS, n, LANES), BF16),
        ((KV_HEADS, n, LANES), BF16), ((KV_HEADS, n, LANES), BF16),
        ((n, IDX_DIM), BF16), ((n, IDX_HEADS * IDX_DIM), BF16), ((n, LANES), F32),
    ]

    def body(q_ref, kv_ref, sm_ref, iq_ref, *rest):
        (gq_ref, gk_ref, indq_ref, indqt_ref, indk_ref, indkt_ref, plo_ref, phi_ref,
         qn_ref, kn_ref, v_ref, ik_ref, klo_ref, khi_ref, vlo_ref, vhi_ref, ikb_ref, iqb_ref, iw_ref) = rest
        _attn_prep_kernel(q_ref, kv_ref, sm_ref, gq_ref, gk_ref, indq_ref, indqt_ref, indk_ref, indkt_ref,
                          plo_ref, phi_ref, qn_ref, kn_ref, v_ref, ik_ref, klo_ref, khi_ref, vlo_ref, vhi_ref,
                          ikb_ref, iqb_ref, iw_ref)
        iqb_ref[...] = iq_ref[...].astype(BF16)

    return pl.pallas_call(
        body,
        grid=(n // tm,),
        in_specs=[pl.BlockSpec((tm, ATT_WIDTH), lambda i: (i, COL_Q // ATT_WIDTH)),
                  pl.BlockSpec((tm, 2 * KV_WIDTH), lambda i: (i, COL_K // (2 * KV_WIDTH))),
                  pl.BlockSpec((tm, LANES), lambda i: (i, COL_SMALL // LANES)),
                  pl.BlockSpec((tm, IDX_HEADS * IDX_DIM), lambda i: (i, COL_IQ // (IDX_HEADS * IDX_DIM))),
                  full(gq), full(gk)] + [full(c) for c in consts],
        out_specs=[pl.BlockSpec((tm, s[1]), lambda i: (i, 0)) if len(s) == 2
                   else pl.BlockSpec((KV_HEADS, tm, LANES), lambda i: (0, i, 0)) for s, _ in outs],
        out_shape=[jax.ShapeDtypeStruct(s, d) for s, d in outs],
        compiler_params=_cparams(("parallel",)),
        name="attn_prep",
    )(proj, proj, proj, proj, gq, gk, *consts)


def _merge_kernel(ya_ref, yb_ref, ga_ref, gb_ref, x_ref, wa_ref, wb_ref, wo_ref, g2_ref, x1_ref, h_ref):
    a = _dot(ya_ref[...].astype(BF16), wa_ref[...])
    b = _dot(yb_ref[...].astype(BF16), wb_ref[...])
    merged = jax.nn.sigmoid(ga_ref[...]) * a + jax.nn.sigmoid(gb_ref[...]) * b
    x1 = x_ref[...] + _dot(merged.astype(BF16), wo_ref[...])
    x1_ref[...] = x1
    ms = jnp.mean(x1 * x1, axis=-1, keepdims=True)
    h_ref[...] = (x1 * lax.rsqrt(ms + EPS) * g2_ref[...]).astype(BF16)


def _merge(y_ssd, y_att, proj, x2d, wa, wb, wo, g2, tm):
    n = x2d.shape[0]

    def full(a):
        return pl.BlockSpec(a.shape, lambda i: (0,) * a.ndim)

    g2 = g2.reshape(1, D_MODEL)
    return pl.pallas_call(
        _merge_kernel,
        grid=(n // tm,),
        in_specs=[pl.BlockSpec((tm, D_INNER), lambda i: (i, 0)),
                  pl.BlockSpec((tm, ATT_WIDTH), lambda i: (i, 0)),
                  pl.BlockSpec((tm, D_MODEL), lambda i: (i, COL_GA // D_MODEL)),
                  pl.BlockSpec((tm, D_MODEL), lambda i: (i, COL_GB // D_MODEL)),
                  pl.BlockSpec((tm, D_MODEL), lambda i: (i, 0)),
                  full(wa), full(wb), full(wo), full(g2)],
        out_specs=[pl.BlockSpec((tm, D_MODEL), lambda i: (i, 0)),
                   pl.BlockSpec((tm, D_MODEL), lambda i: (i, 0))],
        out_shape=[jax.ShapeDtypeStruct((n, D_MODEL), F32), jax.ShapeDtypeStruct((n, D_MODEL), BF16)],
        compiler_params=_cparams(("parallel",)),
        name="merge",
    )(y_ssd, y_att, proj, proj, x2d, wa, wb, wo, g2)


def _top_values(cur, k):
    vals = []
    for _ in range(k):
        m = jnp.max(cur, axis=0, keepdims=True)
        vals.append(m)
        cur = jnp.where(cur >= m, NEG_INF, cur)
    return vals


PEER_RANKS = PEER_TOPK + 1


def _peer_select_kernel(h_ref, wq_ref, keys_ref, th_ref, c1_ref, s2_ref, e2_ref):
    q = _dot(h_ref[...], wq_ref[...]).astype(BF16)
    t = q.shape[0]
    riota = lax.broadcasted_iota(I32, (SUBLANES, t), 0)
    for hd in range(PEER_HEADS):
        base = hd * 2 * PEER_HALF
        s1 = _dot_nt(keys_ref[0, hd], q[:, base:base + PEER_HALF])
        s2 = _dot_nt(keys_ref[1, hd], q[:, base + PEER_HALF:base + 2 * PEER_HALF])
        v1 = _top_values(s1, PEER_RANKS)
        v2 = _top_values(s2, PEER_RANKS)
        pad = jnp.full((SUBLANES - 1, t), NEG_INF, F32)
        v2a = jnp.concatenate(v2 + [pad], axis=0)
        blocks = [v1[0] + v2a, v1[1] + v2a[:SUBLANES]]
        for i in range(2, SUBLANES):
            blocks.append(jnp.where(riota < PEER_RANKS // (i + 1), v1[i] + v2a[:SUBLANES], NEG_INF))
        blocks.append(jnp.concatenate(v1[SUBLANES:] + [pad], axis=0) + v2[0])
        cand = jnp.concatenate(blocks, axis=0)
        top = _top_values(cand, PEER_RANKS)
        zsum = sum(jnp.exp(tv - top[0]) for tv in top[:PEER_TOPK])
        th = 0.5 * (top[PEER_TOPK - 1] + top[PEER_TOPK]) - s1
        c1 = jnp.exp(s1 - v1[0]) / zsum
        cw = th_ref.shape[-1]
        for ch in range(th_ref.shape[1]):
            th_ref[hd, ch] = th[:, ch * cw:(ch + 1) * cw]
            c1_ref[hd, ch] = c1[:, ch * cw:(ch + 1) * cw]
        s2_ref[hd] = s2
        e2_ref[hd] = jnp.exp(s2 - v2[0])


def _peer_select(hb, wq, keys, tn):
    n = hb.shape[0]
    big = jax.ShapeDtypeStruct((PEER_HEADS, PEER_N_KEYS, n), F32)
    bspec = pl.BlockSpec((PEER_HEADS, PEER_N_KEYS, tn), lambda i: (0, 0, i))
    cw = min(LANES, tn)
    chunked = jax.ShapeDtypeStruct((PEER_HEADS, n // cw, PEER_N_KEYS, cw), F32)
    cspec = pl.BlockSpec((PEER_HEADS, tn // cw, PEER_N_KEYS, cw), lambda i: (0, i, 0, 0))
    return pl.pallas_call(
        _peer_select_kernel,
        grid=(n // tn,),
        in_specs=[pl.BlockSpec((tn, D_MODEL), lambda i: (i, 0)),
                  pl.BlockSpec(wq.shape, lambda i: (0, 0)),
                  pl.BlockSpec(keys.shape, lambda i: (0, 0, 0, 0))],
        out_specs=[cspec, cspec, bspec, bspec],
        out_shape=[chunked, chunked, big, big],
        compiler_params=_cparams(("parallel",)),
        name="peer_select",
    )(hb, wq, keys)


def _gelu_tanh(x):
    return 0.5 * x * (1.0 + jnp.tanh(math.sqrt(2.0 / math.pi) * (x + 0.044715 * (x * x * x))))


PEER_EC = 1024
PEER_SUB = 512


def _peer_mix_kernel(h_ref, u_ref, vt_ref, th_ref, c1_ref, s2_ref, e2_ref, x1_ref, y_ref,
                     acc_ref, act_ref, g_ref, ht_ref):
    j = pl.program_id(1)

    @pl.when(j == 0)
    def _():
        acc_ref[...] = jnp.zeros_like(acc_ref)
        ht_ref[...] = h_ref[...].T

    tn = act_ref.shape[1]
    cw = min(LANES, tn)

    def gate_chunk(sc):
        start = sc * PEER_SUB
        erows = pl.ds(start if isinstance(sc, int) else pl.multiple_of(start, PEER_SUB), PEER_SUB)
        act_ref[...] = _gelu_tanh(_dot(u_ref[erows, :], ht_ref[...]))
        slot = sc % 2
        for ii in range(PEER_SUB // PEER_N_KEYS):
            i1 = j * (PEER_EC // PEER_N_KEYS) + sc * (PEER_SUB // PEER_N_KEYS) + ii
            rows = slice(ii * PEER_N_KEYS, (ii + 1) * PEER_N_KEYS)
            for ch in range(tn // cw):
                cols = slice(ch * cw, (ch + 1) * cw)
                grp = (PEER_N_KEYS // SUBLANES, SUBLANES, cw)
                w = jnp.zeros(grp, F32)
                for hd in range(PEER_HEADS):
                    thb = jnp.broadcast_to(th_ref[hd, ch, pl.ds(i1, 1), :], (SUBLANES, cw))
                    c1b = jnp.broadcast_to(c1_ref[hd, ch, pl.ds(i1, 1), :], (SUBLANES, cw))
                    chosen = s2_ref[hd, :, cols].reshape(grp) >= thb
                    w = w + jnp.where(chosen, e2_ref[hd, :, cols].reshape(grp) * c1b, 0.0)
                g = act_ref[rows, cols].reshape(grp) * w
                g_ref[slot, rows, cols] = g.reshape(PEER_N_KEYS, cw).astype(BF16)

    def project_chunk(sc):
        acc_ref[...] += _dot(vt_ref[sc], g_ref[sc % 2])

    gate_chunk(0)

    def pipelined(sc, _):
        project_chunk(sc - 1)
        gate_chunk(sc)
        return 0

    nsub = PEER_EC // PEER_SUB
    lax.fori_loop(1, nsub, pipelined, 0)
    project_chunk(nsub - 1)

    @pl.when(j == pl.num_programs(1) - 1)
    def _():
        y_ref[...] = x1_ref[...] + acc_ref[...].T


def _peer_mix(hb, ub, vtb, sel, x1, tn):
    n = hb.shape[0]
    th, c1, s2, e2 = sel
    bspec = pl.BlockSpec((PEER_HEADS, PEER_N_KEYS, tn), lambda i, j: (0, 0, i))
    cw = min(LANES, tn)
    cspec = pl.BlockSpec((PEER_HEADS, tn // cw, PEER_N_KEYS, cw), lambda i, j: (0, i, 0, 0))
    return pl.pallas_call(
        _peer_mix_kernel,
        grid=(n // tn, PEER_EXPERTS // PEER_EC),
        in_specs=[pl.BlockSpec((tn, D_MODEL), lambda i, j: (i, 0)),
                  pl.BlockSpec((PEER_EC, D_MODEL), lambda i, j: (j, 0)),
                  pl.BlockSpec((PEER_EC // PEER_SUB, D_MODEL, PEER_SUB), lambda i, j: (j, 0, 0)),
                  cspec, cspec, bspec, bspec,
                  pl.BlockSpec((tn, D_MODEL), lambda i, j: (i, 0))],
        out_specs=pl.BlockSpec((tn, D_MODEL), lambda i, j: (i, 0)),
        out_shape=jax.ShapeDtypeStruct((n, D_MODEL), F32),
        scratch_shapes=[pltpu.VMEM((D_MODEL, tn), F32),
                        pltpu.VMEM((PEER_SUB, tn), F32),
                        pltpu.VMEM((2, PEER_SUB, tn), BF16),
                        pltpu.VMEM((D_MODEL, tn), BF16)],
        compiler_params=_cparams(("parallel", "arbitrary")),
        name="peer_mix",
    )(hb, ub, vtb, th, c1, s2, e2, x1)


HALO = 8
N_PAIRS = SSD_HEADS // 2
GROUP_W = D_INNER // SSD_GROUPS


def _softplus(x):
    return jnp.maximum(x, 0.0) + jnp.log1p(jnp.exp(-jnp.abs(x)))


def _silu(x):
    return x * jax.nn.sigmoid(x)


def _lane_bcast(col):
    return jnp.broadcast_to(col, (col.shape[0], LANES))


def _ssd_gate_norm(y, xs, z, dvec, gain):
    y = (y + dvec * xs) * _silu(z)
    parts = []
    for g in range(SSD_GROUPS):
        yg = y[:, g * GROUP_W:(g + 1) * GROUP_W]
        ms = jnp.mean(yg * yg, axis=-1, keepdims=True)
        parts.append(yg * lax.rsqrt(ms + EPS))
    return jnp.concatenate(parts, axis=1) * gain


def _ssd_prompt_kernel(xbc_ref, z_ref, sm_ref, cw_ref, cb_ref, dtb_ref, alog_ref, dvec_ref, gain_ref, tri_ref,
                       y_ref, hout_ref, ext_ref, h_ref, yacc_ref):
    c = pl.program_id(1)

    @pl.when(c == 0)
    def _():
        ext_ref[0:HALO, :] = jnp.zeros((HALO, CONV_DIM), F32)
        h_ref[...] = jnp.zeros_like(h_ref)

    x = xbc_ref[...]
    ext_ref[HALO:HALO + SSD_CHUNK, :] = x
    conv = cb_ref[...]
    for j in range(CONV_W):
        conv = conv + cw_ref[j:j + 1, :] * ext_ref[pl.ds(HALO - (CONV_W - 1) + j, SSD_CHUNK), :]
    ext_ref[0:HALO, :] = x[SSD_CHUNK - HALO:, :]
    xc = _silu(conv)
    xs = xc[:, :D_INNER]

    lane = lax.broadcasted_iota(I32, (SSD_CHUNK, LANES), 1)
    row = lax.broadcasted_iota(I32, (SSD_CHUNK, LANES), 0)
    dt = jnp.where(lane < SSD_HEADS, _softplus(sm_ref[...] + dtb_ref[...]), 0.0)
    da = dt * (-jnp.exp(alog_ref[...]))
    hi, mid, lo = _split3(da)
    tri = tri_ref[...]
    acum = _dot(tri, hi) + _dot(tri, mid) + _dot(tri, lo)
    acum_t = acum.T
    dt_t = dt.T
    alast = acum[SSD_CHUNK - 1:SSD_CHUNK, :]
    e_in = jnp.exp(acum)
    e_out = jnp.exp(alast - acum) * dt
    e_all = jnp.exp(alast)
    causal = row >= lane
    lo_half = lane < SSD_HEAD_DIM

    for g in range(SSD_GROUPS):
        bg = xc[:, D_INNER + g * D_STATE:D_INNER + (g + 1) * D_STATE].astype(BF16)
        cg = xc[:, D_INNER + SSD_GROUPS * D_STATE + g * D_STATE:
                D_INNER + SSD_GROUPS * D_STATE + (g + 1) * D_STATE].astype(BF16)
        cbm = _dot_nt(cg, bg)
        for pp in range(2):
            pair = 2 * g + pp
            r0 = 2 * pair
            xp = xs[:, r0 * SSD_HEAD_DIM:(r0 + 2) * SSD_HEAD_DIM]
            yp = jnp.zeros((SSD_CHUNK, LANES), F32)
            for hh in range(2):
                r = r0 + hh
                seg = _lane_bcast(acum[:, r:r + 1]) - acum_t[r:r + 1, :]
                lmat = jnp.exp(jnp.where(causal, seg, NEG_INF))
                m = (cbm * lmat * dt_t[r:r + 1, :]).astype(BF16)
                xm = jnp.where(lo_half == (hh == 0), xp, 0.0).astype(BF16)
                yp = yp + _dot(m, xm)
            hp = h_ref[pair]
            scale_in = jnp.where(lo_half, _lane_bcast(e_in[:, r0:r0 + 1]), _lane_bcast(e_in[:, r0 + 1:r0 + 2]))
            yp = yp + _dot_nt(cg, hp.astype(BF16)) * scale_in
            scale_out = jnp.where(lo_half, _lane_bcast(e_out[:, r0:r0 + 1]), _lane_bcast(e_out[:, r0 + 1:r0 + 2]))
            xd = (xp * scale_out).astype(BF16)
            hdec = jnp.where(row < SSD_HEAD_DIM, e_all[:, r0:r0 + 1], e_all[:, r0 + 1:r0 + 2])
            h_ref[pair] = hp * hdec + _dot_tn(xd, bg)
            yacc_ref[:, r0 * SSD_HEAD_DIM:(r0 + 2) * SSD_HEAD_DIM] = yp

    y_ref[...] = _ssd_gate_norm(yacc_ref[...], xs, z_ref[...], dvec_ref[...], gain_ref[...])

    @pl.when(c == pl.num_programs(1) - 1)
    def _():
        hout_ref[0] = h_ref[...]


def _pad_lanes(v):
    return jnp.zeros((1, LANES), F32).at[0, :v.shape[0]].set(v)


def _ssd_prompt(proj, bsz, conv_w, conv_b, dt_bias, a_log, d_skip, ssd_norm_g):
    n = proj.shape[0]
    nc = n // bsz // SSD_CHUNK
    tri = jnp.asarray(np.tril(np.ones((SSD_CHUNK, SSD_CHUNK), np.float32)), BF16)
    dvec = jnp.repeat(d_skip, SSD_HEAD_DIM).reshape(1, D_INNER)
    small = [conv_w, conv_b.reshape(1, CONV_DIM), _pad_lanes(dt_bias), _pad_lanes(a_log), dvec,
             ssd_norm_g.reshape(1, D_INNER), tri]

    def full(a):
        return pl.BlockSpec(a.shape, lambda b, c: (0,) * a.ndim)

    return pl.pallas_call(
        _ssd_prompt_kernel,
        grid=(bsz, nc),
        in_specs=[pl.BlockSpec((SSD_CHUNK, CONV_DIM), lambda b, c: (b * nc + c, 0)),
                  pl.BlockSpec((SSD_CHUNK, D_INNER), lambda b, c: (b * nc + c, COL_Z // D_INNER)),
                  pl.BlockSpec((SSD_CHUNK, LANES), lambda b, c: (b * nc + c, COL_SMALL // LANES))]
                 + [full(a) for a in small],
        out_specs=[pl.BlockSpec((SSD_CHUNK, D_INNER), lambda b, c: (b * nc + c, 0)),
                   pl.BlockSpec((1, N_PAIRS, LANES, D_STATE), lambda b, c: (b, 0, 0, 0))],
        out_shape=[jax.ShapeDtypeStruct((n, D_INNER), F32),
                   jax.ShapeDtypeStruct((bsz, N_PAIRS, LANES, D_STATE), F32)],
        scratch_shapes=[pltpu.VMEM((HALO + SSD_CHUNK, CONV_DIM), F32),
                        pltpu.VMEM((N_PAIRS, LANES, D_STATE), F32),
                        pltpu.VMEM((SSD_CHUNK, D_INNER), F32)],
        compiler_params=_cparams(("parallel", "arbitrary")),
        name="ssd_prompt",
    )(proj, proj, proj, *small)


def _t5_bucket_np(dist):
    n = np.maximum(dist, 0)
    max_exact = NUM_BUCKETS // 2
    nf = np.maximum(n, 1).astype(np.float32)
    ratio = np.log(nf / np.float32(max_exact)) / np.float32(math.log(MAX_DISTANCE / max_exact))
    large = max_exact + (ratio * np.float32(NUM_BUCKETS - max_exact)).astype(np.int32)
    large = np.minimum(large, NUM_BUCKETS - 1)
    return np.where(n < max_exact, n, large).astype(np.int32)


BIAS_TJ = 2048


def _bias_kernel(rel_t_ref, bucket_ref, o_ref):
    ids = lax.broadcasted_iota(I32, (NUM_BUCKETS, BIAS_TJ), 0)
    onehot = jnp.where(ids == bucket_ref[...], 1.0, 0.0).astype(BF16)
    o_ref[...] = _exact_dot(rel_t_ref[...], onehot)


def _bias_lookup(rel_bias, buckets):
    j = buckets.shape[0]
    return pl.pallas_call(
        _bias_kernel,
        grid=(j // BIAS_TJ,),
        in_specs=[pl.BlockSpec((ATT_HEADS, NUM_BUCKETS), lambda i: (0, 0)),
                  pl.BlockSpec((1, BIAS_TJ), lambda i: (0, i))],
        out_specs=pl.BlockSpec((ATT_HEADS, BIAS_TJ), lambda i: (0, i)),
        out_shape=jax.ShapeDtypeStruct((ATT_HEADS, j), F32),
        compiler_params=_cparams(("parallel",)),
        name="bias_lookup",
    )(rel_bias.T, jnp.asarray(buckets.reshape(1, j)))


def _prompt_bias_buckets():
    i = np.arange(Q_BLOCK)[:, None]
    j = np.arange(Q_BLOCK)[None, :]
    far = np.full((Q_BLOCK, Q_BLOCK), 2 * Q_BLOCK)
    return _t5_bucket_np(np.stack([far, i - j + Q_BLOCK, i - j]))


KEY_TILES = 4
KEY_CHUNK = KEY_TILES * Q_BLOCK


def _fold_tiles(x, op):
    out = x[:, 0:LANES]
    for u in range(1, x.shape[1] // LANES):
        out = op(out, x[:, u * LANES:(u + 1) * LANES])
    return out


def _sortable_key(x):
    b = pltpu.bitcast(x, I32)
    return jnp.where(b < 0, b ^ 0x7FFFFFFF, b)


def _kth_largest_key(count_ge, k, shape):
    def body(it, cand):
        trial = cand + jnp.left_shift(jnp.int32(1), 31 - it)
        return jnp.where(count_ge(trial) >= k, trial, cand)
    return lax.fori_loop(0, 32, body, jnp.full(shape, INT_MIN, I32))


def _select_keys(n_sel, qb, iq_ref, iw_ref, ik_ref, triu_ref, key_ref, mask_ref):
    nchunk = qb // KEY_TILES + 1
    row = lax.broadcasted_iota(I32, (Q_BLOCK, KEY_CHUNK), 0)
    lane = lax.broadcasted_iota(I32, (Q_BLOCK, KEY_CHUNK), 1)

    def admissible(c):
        return lane + (c * KEY_CHUNK - qb * Q_BLOCK) <= row

    iw = iw_ref[...]
    iq = iq_ref[...]
    wcols = [jnp.broadcast_to(iw[:, SMALL_IW + h:SMALL_IW + h + 1], (Q_BLOCK, KEY_CHUNK)) for h in range(IDX_HEADS)]
    iqs = [iq[:, h * IDX_DIM:(h + 1) * IDX_DIM] for h in range(IDX_HEADS)]

    def score_chunk(c, _):
        ikc = ik_ref[pl.ds(pl.multiple_of(c * KEY_CHUNK, KEY_CHUNK), KEY_CHUNK), :]
        sc = jnp.zeros((Q_BLOCK, KEY_CHUNK), F32)
        for h in range(IDX_HEADS):
            sc = sc + wcols[h] * jnp.maximum(_dot_nt(iqs[h], ikc), 0.0)
        key_ref[c] = _sortable_key(jnp.where(admissible(c), sc, NEG_INF))
        return 0

    lax.fori_loop(0, nchunk, score_chunk, 0)

    def count_ge(trial):
        def body(c, acc):
            return acc + _fold_tiles(jnp.where(key_ref[c] >= trial, 1, 0), jnp.add)
        acc = lax.fori_loop(0, nchunk, body, jnp.zeros((Q_BLOCK, LANES), I32))
        return jnp.sum(acc, axis=1, keepdims=True)

    tau = _kth_largest_key(count_ge, n_sel, (Q_BLOCK, 1))

    def count_gt(c, acc):
        return acc + _fold_tiles(jnp.where(key_ref[c] > tau, 1, 0), jnp.add)
    n_gt = jnp.sum(lax.fori_loop(0, nchunk, count_gt, jnp.zeros((Q_BLOCK, LANES), I32)), axis=1, keepdims=True)
    need = (n_sel - n_gt).astype(F32)

    def mask_chunk(c, seen):
        k = key_ref[c]
        eq = k == tau
        eqf = jnp.where(eq, 1.0, 0.0)
        pref = _dot(eqf.astype(BF16), triu_ref[...]) + seen
        sel = jnp.logical_or(k > tau, jnp.logical_and(eq, pref <= need))
        mask_ref[c] = jnp.where(jnp.logical_and(sel, admissible(c)), 0.0, NEG_INF)
        return seen + jnp.sum(eqf, axis=1, keepdims=True)

    lax.fori_loop(0, nchunk, mask_chunk, jnp.zeros((Q_BLOCK, 1), F32))


HEADS_PER_KV = ATT_HEADS // KV_HEADS


def _attn_prompt_kernel(n_sel, q_ref, iq_ref, iw_ref, klo_ref, khi_ref, vlo_ref, vhi_ref, ik_ref, bt_ref, triu_ref,
                        y_ref, key_ref, mask_ref, lbuf_ref):
    qb = pl.program_id(1)
    g = pl.program_id(2)
    nchunk = qb // KEY_TILES + 1

    @pl.when(g == 0)
    def _():
        _select_keys(n_sel, qb, iq_ref, iw_ref, ik_ref, triu_ref, key_ref, mask_ref)

    k_refs = (klo_ref, khi_ref)
    v_refs = (vlo_ref, vhi_ref)
    qps = [q_ref[:, pp * LANES:(pp + 1) * LANES] for pp in range(HEADS_PER_KV // 2)]

    def logits_chunk(c, mxs):
        rows = pl.ds(pl.multiple_of(c * KEY_CHUNK, KEY_CHUNK), KEY_CHUNK)
        kinds = [jnp.clip(c * KEY_TILES + u - qb + 2, 0, 2) for u in range(KEY_TILES)]
        mask = mask_ref[c]
        new = []
        for hh in range(HEADS_PER_KV):
            bias = jnp.concatenate([bt_ref[g * HEADS_PER_KV + hh, kd] for kd in kinds], axis=1)
            lg = _dot_nt(qps[hh // 2], k_refs[hh % 2][g, rows, :]) + bias + mask
            lbuf_ref[hh, c] = lg
            new.append(jnp.maximum(mxs[hh], _fold_tiles(lg, jnp.maximum)))
        return tuple(new)

    neg = jnp.full((Q_BLOCK, LANES), NEG_INF, F32)
    mxs = lax.fori_loop(0, nchunk, logits_chunk, (neg,) * HEADS_PER_KV)
    mrows = [jnp.max(m, axis=1, keepdims=True) for m in mxs]

    def pv_chunk(c, carry):
        rows = pl.ds(pl.multiple_of(c * KEY_CHUNK, KEY_CHUNK), KEY_CHUNK)
        new = []
        for hh in range(HEADS_PER_KV):
            lsum, acc = carry[hh]
            p = jnp.exp(lbuf_ref[hh, c] - mrows[hh])
            new.append((lsum + _fold_tiles(p, jnp.add), acc + _dot(p.astype(BF16), v_refs[hh % 2][g, rows, :])))
        return tuple(new)

    zero = jnp.zeros((Q_BLOCK, LANES), F32)
    res = lax.fori_loop(0, nchunk, pv_chunk, ((zero, zero),) * HEADS_PER_KV)
    outs = [acc / jnp.sum(lsum, axis=1, keepdims=True) for lsum, acc in res]
    for pp in range(HEADS_PER_KV // 2):
        y_ref[:, pp * LANES:(pp + 1) * LANES] = outs[2 * pp] + outs[2 * pp + 1]


def _attn_prompt(prep, btile, bsz):
    qn, _, _, _, klo, khi, vlo, vhi, ikb, iqb, iw = prep
    n = qn.shape[0]
    t = n // bsz
    nqb = t // Q_BLOCK
    assert t % KEY_CHUNK == 0
    nch = t // KEY_CHUNK
    n_sel = min(TOPK_KEYS_MAX, t // 4)
    triu = jnp.asarray(np.triu(np.ones((KEY_CHUNK, KEY_CHUNK), np.float32)), BF16)
    kvspec = pl.BlockSpec((KV_HEADS, t, LANES), lambda b, q, g: (0, b, 0))
    qw = HEADS_PER_KV * ATT_HEAD_DIM
    return pl.pallas_call(
        functools.partial(_attn_prompt_kernel, n_sel),
        grid=(bsz, nqb, KV_HEADS),
        in_specs=[pl.BlockSpec((Q_BLOCK, qw), lambda b, q, g: (b * nqb + q, g)),
                  pl.BlockSpec((Q_BLOCK, IDX_HEADS * IDX_DIM), lambda b, q, g: (b * nqb + q, 0)),
                  pl.BlockSpec((Q_BLOCK, LANES), lambda b, q, g: (b * nqb + q, 0)),
                  kvspec, kvspec, kvspec, kvspec,
                  pl.BlockSpec((t, IDX_DIM), lambda b, q, g: (b, 0)),
                  pl.BlockSpec(btile.shape, lambda b, q, g: (0, 0, 0, 0)),
                  pl.BlockSpec(triu.shape, lambda b, q, g: (0, 0))],
        out_specs=pl.BlockSpec((Q_BLOCK, qw), lambda b, q, g: (b * nqb + q, g)),
        out_shape=jax.ShapeDtypeStruct((n, ATT_WIDTH), F32),
        scratch_shapes=[pltpu.VMEM((nch, Q_BLOCK, KEY_CHUNK), I32),
                        pltpu.VMEM((nch, Q_BLOCK, KEY_CHUNK), F32),
                        pltpu.VMEM((HEADS_PER_KV, nch, Q_BLOCK, KEY_CHUNK), F32)],
        compiler_params=_cparams(("parallel", "arbitrary", "arbitrary")),
        name="attn_prompt",
    )(qn, iqb, iw, klo, khi, vlo, vhi, ikb, btile, triu)


def _head_expand_indicator():
    m = np.zeros((LANES, D_INNER), np.float32)
    m[np.arange(D_INNER) // SSD_HEAD_DIM, np.arange(D_INNER)] = 1.0
    return m


def _ssd_step_kernel(xbc_ref, z_ref, sm_ref, sc_ref, hin_ref, cw_ref, cb_ref, dtb_ref, alog_ref, dvec_ref, gain_ref,
                     exp_ref, y_ref, hout_ref, cs_ref, xs_ref, bm_ref, cm_ref, dtx_t_ref, dec_t_ref, y_t_ref):
    b = pl.program_id(0)
    db = xbc_ref.shape[0]

    @pl.when(b == 0)
    def _():
        x = xbc_ref[...]
        sc = sc_ref[...]
        conv = cb_ref[...] + cw_ref[CONV_W - 1:CONV_W, :] * x
        for j in range(CONV_W - 1):
            conv = conv + cw_ref[j:j + 1, :] * sc[:, j * CONV_DIM:(j + 1) * CONV_DIM]
        cs_ref[:, :(CONV_W - 2) * CONV_DIM] = sc[:, CONV_DIM:]
        cs_ref[:, (CONV_W - 2) * CONV_DIM:] = x
        xc = _silu(conv)
        xs = xc[:, :D_INNER]
        lane = lax.broadcasted_iota(I32, (db, LANES), 1)
        dt = jnp.where(lane < SSD_HEADS, _softplus(sm_ref[...] + dtb_ref[...]), 0.0)
        dec = jnp.exp(dt * (-jnp.exp(alog_ref[...])))
        xs_ref[0:db, :] = xs
        bm_ref[0:db, :] = xc[:, D_INNER:D_INNER + SSD_GROUPS * D_STATE]
        cm_ref[0:db, :] = xc[:, D_INNER + SSD_GROUPS * D_STATE:]
        dtx_t_ref[:, 0:db] = (_exact_dot(dt, exp_ref[...]) * xs).T
        dec_t_ref[:, 0:db] = _exact_dot(dec, exp_ref[...]).T
        y_t_ref[...] = jnp.zeros_like(y_t_ref)

    lane = lax.broadcasted_iota(I32, (LANES, LANES), 1)
    mine = lane == b
    base = pl.multiple_of((b // 8) * 8, 8)
    my_row = lax.broadcasted_iota(I32, (8, SSD_GROUPS * D_STATE), 0) == (b % 8)
    bm_b = jnp.sum(jnp.where(my_row, bm_ref[pl.ds(base, 8), :], 0.0), axis=0, keepdims=True)
    cm_b = jnp.sum(jnp.where(my_row, cm_ref[pl.ds(base, 8), :], 0.0), axis=0, keepdims=True)
    for pair in range(N_PAIRS):
        g = pair // 2
        rows = slice(pair * LANES, (pair + 1) * LANES)
        xcol = jnp.sum(jnp.where(mine, dtx_t_ref[rows, :], 0.0), axis=1, keepdims=True)
        dcol = jnp.sum(jnp.where(mine, dec_t_ref[rows, :], 0.0), axis=1, keepdims=True)
        brow = bm_b[:, g * D_STATE:(g + 1) * D_STATE]
        crow = cm_b[:, g * D_STATE:(g + 1) * D_STATE]
        hnew = hin_ref[0, pair] * dcol + xcol * brow
        hout_ref[0, pair] = hnew
        ycol = jnp.sum(hnew * crow, axis=1, keepdims=True)
        y_t_ref[rows, :] = jnp.where(mine, ycol, y_t_ref[rows, :])

    @pl.when(b == pl.num_programs(0) - 1)
    def _():
        y = y_t_ref[...].T[0:db, :]
        y_ref[...] = _ssd_gate_norm(y, xs_ref[0:db, :], z_ref[...], dvec_ref[...], gain_ref[...])


def _ssd_step(proj, state_conv2d, state_ssm4d, conv_w, conv_b, dt_bias, a_log, d_skip, ssd_norm_g):
    db = proj.shape[0]
    assert db <= LANES and db % 8 == 0
    dvec = jnp.repeat(d_skip, SSD_HEAD_DIM).reshape(1, D_INNER)
    small = [conv_w, conv_b.reshape(1, CONV_DIM), _pad_lanes(dt_bias), _pad_lanes(a_log), dvec,
             ssd_norm_g.reshape(1, D_INNER), jnp.asarray(_head_expand_indicator(), BF16)]

    def full(a):
        return pl.BlockSpec(a.shape, lambda b: (0,) * a.ndim)

    cs_w = (CONV_W - 1) * CONV_DIM
    hspec = pl.BlockSpec((1, N_PAIRS, LANES, D_STATE), lambda b: (b, 0, 0, 0))
    return pl.pallas_call(
        _ssd_step_kernel,
        grid=(db,),
        in_specs=[pl.BlockSpec((db, CONV_DIM), lambda b: (0, 0)),
                  pl.BlockSpec((db, D_INNER), lambda b: (0, COL_Z // D_INNER)),
                  pl.BlockSpec((db, LANES), lambda b: (0, COL_SMALL // LANES)),
                  pl.BlockSpec((db, cs_w), lambda b: (0, 0)),
                  hspec] + [full(a) for a in small],
        out_specs=[pl.BlockSpec((db, D_INNER), lambda b: (0, 0)),
                   hspec,
                   pl.BlockSpec((db, cs_w), lambda b: (0, 0))],
        out_shape=[jax.ShapeDtypeStruct((db, D_INNER), F32),
                   jax.ShapeDtypeStruct(state_ssm4d.shape, F32),
                   jax.ShapeDtypeStruct((db, cs_w), F32)],
        scratch_shapes=[pltpu.VMEM((LANES, D_INNER), F32),
                        pltpu.VMEM((LANES, SSD_GROUPS * D_STATE), F32),
                        pltpu.VMEM((LANES, SSD_GROUPS * D_STATE), F32),
                        pltpu.VMEM((D_INNER, LANES), F32),
                        pltpu.VMEM((D_INNER, LANES), F32),
                        pltpu.VMEM((D_INNER, LANES), F32)],
        compiler_params=_cparams(("arbitrary",)),
        name="ssd_step",
    )(proj, proj, proj, state_conv2d, state_ssm4d, *small)


def _page_copies(pt_ref, seq, cache_ref, buf_ref, slot, sem_ref, n_pages):
    return [pltpu.make_async_copy(cache_ref.at[pt_ref[seq, p]], buf_ref.at[slot, p], sem_ref.at[slot])
            for p in range(n_pages)]


def _paged_fetch(pt_ref, streams, n_pages):
    b = pl.program_id(0)
    slot = b % 2

    @pl.when(b == 0)
    def _():
        for cache_ref, buf_ref, sem_ref in streams:
            for cp in _page_copies(pt_ref, 0, cache_ref, buf_ref, 0, sem_ref, n_pages):
                cp.start()

    @pl.when(b + 1 < pl.num_programs(0))
    def _():
        for cache_ref, buf_ref, sem_ref in streams:
            for cp in _page_copies(pt_ref, b + 1, cache_ref, buf_ref, 1 - slot, sem_ref, n_pages):
                cp.start()

    for cache_ref, buf_ref, sem_ref in streams:
        for cp in _page_copies(pt_ref, b, cache_ref, buf_ref, slot, sem_ref, n_pages):
            cp.wait()
    return slot


def _dec_score_kernel(n_pages, pt_ref, iq_ref, w_ref, iknew_ref, cache_ref, o_ref, buf_ref, sem_ref):
    b = pl.program_id(0)
    slot = _paged_fetch(pt_ref, [(cache_ref, buf_ref, sem_ref)], n_pages)
    past = n_pages * PAGE_SIZE
    iq = iq_ref[...].astype(BF16)
    w = w_ref[...]
    for p in range(n_pages):
        s = _dot(iq, buf_ref[slot, p].astype(BF16))
        o_ref[0, :, p * PAGE_SIZE:(p + 1) * PAGE_SIZE] = jnp.sum(w * jnp.maximum(s, 0.0), axis=0, keepdims=True)
    iknew = iknew_ref[pl.ds(b, 1), :].astype(BF16).astype(F32)
    s_new = jnp.sum(iq.astype(F32) * iknew, axis=1, keepdims=True)
    sc_new = jnp.sum(w * jnp.maximum(s_new, 0.0), axis=0, keepdims=True)
    lane = lax.broadcasted_iota(I32, (1, LANES), 1)
    o_ref[0, :, past:past + LANES] = jnp.where(lane == 0, sc_new, NEG_INF)


def _dec_scores(page_table, cache_ik, iq8, w8, ikb_new):
    db, n_pages = page_table.shape
    s_pad = (n_pages + 1) * PAGE_SIZE
    grid_spec = pltpu.PrefetchScalarGridSpec(
        num_scalar_prefetch=1,
        grid=(db,),
        in_specs=[pl.BlockSpec((IDX_HEADS, IDX_DIM), lambda b, pt: (b, 0)),
                  pl.BlockSpec((IDX_HEADS, 1), lambda b, pt: (b, 0)),
                  pl.BlockSpec(ikb_new.shape, lambda b, pt: (0, 0)),
                  pl.BlockSpec(memory_space=pl.ANY)],
        out_specs=pl.BlockSpec((1, 1, s_pad), lambda b, pt: (b, 0, 0)),
        scratch_shapes=[pltpu.VMEM((2, n_pages, IDX_DIM, PAGE_SIZE), F32),
                        pltpu.SemaphoreType.DMA((2,))],
    )
    return pl.pallas_call(
        functools.partial(_dec_score_kernel, n_pages),
        grid_spec=grid_spec,
        out_shape=jax.ShapeDtypeStruct((db, 1, s_pad), F32),
        compiler_params=_cparams(("arbitrary",)),
        name="dec_scores",
    )(page_table, iq8, w8, ikb_new, cache_ik)


def _dec_select_kernel(n_sel, sc_ref, triu_ref, mask_ref, key_ref):
    db, s_pad = sc_ref.shape
    ntile = s_pad // LANES
    key_ref[...] = _sortable_key(sc_ref[...])

    def count_ge(trial):
        acc = jnp.zeros((db, LANES), I32)
        for j in range(ntile):
            acc = acc + jnp.where(key_ref[:, j * LANES:(j + 1) * LANES] >= trial, 1, 0)
        return jnp.sum(acc, axis=1, keepdims=True)

    tau = _kth_largest_key(count_ge, n_sel, (db, 1))
    acc = jnp.zeros((db, LANES), I32)
    for j in range(ntile):
        acc = acc + jnp.where(key_ref[:, j * LANES:(j + 1) * LANES] > tau, 1, 0)
    need = (n_sel - jnp.sum(acc, axis=1, keepdims=True)).astype(F32)
    seen = jnp.zeros((db, 1), F32)
    for j in range(ntile):
        k = key_ref[:, j * LANES:(j + 1) * LANES]
        eq = k == tau
        pref = _dot(jnp.where(eq, 1.0, 0.0).astype(BF16), triu_ref[...]) + seen
        sel = jnp.logical_or(k > tau, jnp.logical_and(eq, pref <= need))
        mask_ref[:, j * LANES:(j + 1) * LANES] = jnp.where(sel, 0.0, NEG_INF)
        seen = pref[:, LANES - 1:LANES]


def _dec_select(scores2d, n_sel):
    db, s_pad = scores2d.shape
    triu = jnp.asarray(np.triu(np.ones((LANES, LANES), np.float32)), BF16)
    return pl.pallas_call(
        functools.partial(_dec_select_kernel, n_sel),
        grid=(1,),
        in_specs=[pl.BlockSpec((db, s_pad), lambda i: (0, 0)),
                  pl.BlockSpec(triu.shape, lambda i: (0, 0))],
        out_specs=pl.BlockSpec((db, s_pad), lambda i: (0, 0)),
        out_shape=jax.ShapeDtypeStruct((db, s_pad), F32),
        scratch_shapes=[pltpu.VMEM((db, s_pad), I32)],
        compiler_params=_cparams(("arbitrary",)),
        name="dec_select",
    )(scores2d, triu)


def _dec_attn_kernel(n_pages, pt_ref, q_ref, mask_ref, bias_ref, knew_ref, vnew_ref, ck_ref, cv_ref,
                     y_ref, kbuf_ref, vbuf_ref, ksem_ref, vsem_ref):
    b = pl.program_id(0)
    slot = _paged_fetch(pt_ref, [(ck_ref, kbuf_ref, ksem_ref), (cv_ref, vbuf_ref, vsem_ref)], n_pages)
    past = n_pages * PAGE_SIZE
    q = q_ref[...]
    hrow = lax.broadcasted_iota(I32, (ATT_HEADS, KV_WIDTH), 0)
    lane = lax.broadcasted_iota(I32, (ATT_HEADS, KV_WIDTH), 1)
    own = (hrow // (ATT_HEADS // KV_HEADS)) == (lane // ATT_HEAD_DIM)
    qbd = jnp.where(own, jnp.concatenate([q] * KV_HEADS, axis=1), jnp.zeros_like(q[:, :1]))
    knew = knew_ref[pl.ds(b, 1), :].astype(BF16).astype(F32)
    l_new = jnp.sum(qbd.astype(F32) * knew, axis=1, keepdims=True)
    vnew = vnew_ref[pl.ds(b, 1), :].astype(BF16).astype(F32)
    mask = mask_ref[0]
    lg = jnp.concatenate([_dot(qbd, kbuf_ref[slot, p].astype(BF16)) for p in range(n_pages)], axis=1)
    lg = lg + bias_ref[:, 0:past] + mask[:, 0:past]
    tl = lax.broadcasted_iota(I32, (ATT_HEADS, LANES), 1)
    lg_new = jnp.where(tl == 0, l_new, 0.0) + bias_ref[:, past:past + LANES] + mask[:, past:past + LANES]
    m = jnp.maximum(jnp.max(lg, axis=1, keepdims=True), jnp.max(lg_new, axis=1, keepdims=True))
    p = jnp.exp(lg - m)
    p_new = jnp.exp(lg_new - m)
    denom = jnp.sum(p, axis=1, keepdims=True) + jnp.sum(p_new, axis=1, keepdims=True)
    pn = p_new[:, 0:1].astype(BF16).astype(F32)
    pb = p.astype(BF16)
    out = pn * vnew
    for pg in range(n_pages):
        out = out + _dot_nt(pb[:, pg * PAGE_SIZE:(pg + 1) * PAGE_SIZE], vbuf_ref[slot, pg].astype(BF16))
    out = out / denom
    r16 = lax.broadcasted_iota(I32, (ATT_HEADS, ATT_HEAD_DIM), 0) // (ATT_HEADS // KV_HEADS)
    y = jnp.zeros((ATT_HEADS, ATT_HEAD_DIM), F32)
    for g in range(KV_HEADS):
        y = jnp.where(r16 == g, out[:, g * ATT_HEAD_DIM:(g + 1) * ATT_HEAD_DIM], y)
    y_ref[...] = y


def _dec_attn(page_table, q16, mask3d, bias_dec, kn_new, v_new, cache_k3d, cache_v3d):
    db, n_pages = page_table.shape
    s_pad = (n_pages + 1) * PAGE_SIZE
    grid_spec = pltpu.PrefetchScalarGridSpec(
        num_scalar_prefetch=1,
        grid=(db,),
        in_specs=[pl.BlockSpec((ATT_HEADS, ATT_HEAD_DIM), lambda b, pt: (b, 0)),
                  pl.BlockSpec((1, 1, s_pad), lambda b, pt: (b, 0, 0)),
                  pl.BlockSpec(bias_dec.shape, lambda b, pt: (0, 0)),
                  pl.BlockSpec(kn_new.shape, lambda b, pt: (0, 0)),
                  pl.BlockSpec(v_new.shape, lambda b, pt: (0, 0)),
                  pl.BlockSpec(memory_space=pl.ANY),
                  pl.BlockSpec(memory_space=pl.ANY)],
        out_specs=pl.BlockSpec((ATT_HEADS, ATT_HEAD_DIM), lambda b, pt: (b, 0)),
        scratch_shapes=[pltpu.VMEM((2, n_pages, KV_WIDTH, PAGE_SIZE), F32),
                        pltpu.VMEM((2, n_pages, KV_WIDTH, PAGE_SIZE), F32),
                        pltpu.SemaphoreType.DMA((2,)),
                        pltpu.SemaphoreType.DMA((2,))],
    )
    return pl.pallas_call(
        functools.partial(_dec_attn_kernel, n_pages),
        grid_spec=grid_spec,
        out_shape=jax.ShapeDtypeStruct((db * ATT_HEADS, ATT_HEAD_DIM), F32),
        compiler_params=_cparams(("arbitrary",)),
        name="dec_attn",
    )(page_table, q16, mask3d, bias_dec, kn_new, v_new, cache_k3d, cache_v3d)


def _channel_mixer(hb, x1, peer_wq, peer_keys, ub, vtb, tn):
    sel = _peer_select(hb, peer_wq, peer_keys, tn)
    return _peer_mix(hb, ub, vtb, sel, x1, tn)


def kernel(x_prompt, x_sample, cache_k, cache_v, cache_idx_k, state_ssm, state_conv, page_table, rel_bias, norm1_g, w_in, conv_w, conv_b, dt_bias, a_log, d_skip, ssd_norm_g, q_norm_g, k_norm_g, w_branch_a, w_branch_b, w_out, norm2_g, peer_wq, peer_keys, peer_u, peer_v):
    depth = w_in.shape[0]
    bsz, t, _ = x_prompt.shape
    db, ds, _ = x_sample.shape
    assert depth == 1 and ds == 1, "single layer, one new token per sequence"
    n = bsz * t
    n_pages = page_table.shape[1]
    past = n_pages * PAGE_SIZE
    l = 0

    w_perm = _permute_w_in(w_in[l])
    wa, wb, wo = w_branch_a[l].astype(BF16), w_branch_b[l].astype(BF16), w_out[l].astype(BF16)
    wq, keys = peer_wq[l].astype(BF16), peer_keys[l].astype(BF16)
    ub = peer_u[l].astype(BF16)
    vtb = jnp.transpose(peer_v[l].reshape(PEER_EXPERTS // PEER_SUB, PEER_SUB, D_MODEL), (0, 2, 1)).astype(BF16)
    ssd_w = (conv_w[l], conv_b[l], dt_bias[l], a_log[l], d_skip[l], ssd_norm_g[l])

    dec_dist = np.concatenate([past - np.arange(past), np.zeros(PAGE_SIZE, np.int64)])
    n_prompt_b = 3 * Q_BLOCK * Q_BLOCK
    buckets = np.concatenate([_prompt_bias_buckets().reshape(-1), _t5_bucket_np(dec_dist)])
    pad = (-buckets.shape[0]) % BIAS_TJ
    bias_all = _bias_lookup(rel_bias, np.concatenate([buckets, np.zeros(pad, np.int32)]))
    btile = bias_all[:, :n_prompt_b].reshape(ATT_HEADS, 3, Q_BLOCK, Q_BLOCK)
    bias_dec = bias_all[:, n_prompt_b:n_prompt_b + past + PAGE_SIZE]

    xp = x_prompt.reshape(n, D_MODEL)
    proj = _in_proj(xp, norm1_g[l], w_perm, min(1024, n))
    y_ssd, h_t = _ssd_prompt(proj, bsz, *ssd_w)
    prep = _attn_prep(proj, q_norm_g[l], k_norm_g[l], min(512, n))
    y_att = _attn_prompt(prep, btile, bsz)
    x1, hb = _merge(y_ssd, y_att, proj, xp, wa, wb, wo, norm2_g[l], min(512, n))
    y_prompt = _channel_mixer(hb, x1, wq, keys, ub, vtb, min(512, n))

    k_prompt = prep[1].reshape(1, bsz, t, KV_HEADS, ATT_HEAD_DIM)
    v_prompt = prep[2].reshape(1, bsz, t, KV_HEADS, ATT_HEAD_DIM)
    ik_prompt = prep[3].reshape(1, bsz, t, IDX_DIM)
    ssm_prompt = h_t.reshape(1, bsz, SSD_HEADS, SSD_HEAD_DIM, D_STATE)
    conv_prompt = proj.reshape(bsz, t, PROJ_COLS)[:, t - (CONV_W - 1):, :CONV_DIM][None]

    xs_ = x_sample.reshape(db, D_MODEL)
    proj_s = _in_proj(xs_, norm1_g[l], w_perm, db)
    y_ssd_s, h_s, conv_s = _ssd_step(proj_s, state_conv[l].reshape(db, (CONV_W - 1) * CONV_DIM),
                                     state_ssm[l].reshape(db, N_PAIRS, LANES, D_STATE), *ssd_w)
    prep_s = _attn_prep(proj_s, q_norm_g[l], k_norm_g[l], db)
    qn_s, kn_s, v_s, ik_s, _, _, _, _, _, _, iw_s = prep_s
    iq8 = proj_s[:, COL_IQ:COL_IQ + IDX_HEADS * IDX_DIM].reshape(db * IDX_HEADS, IDX_DIM)
    w8 = iw_s[:, SMALL_IW:SMALL_IW + IDX_HEADS].reshape(db * IDX_HEADS, 1)
    ikt_pages = jnp.transpose(cache_idx_k[l], (0, 2, 1))
    kt_pages = jnp.transpose(cache_k[l], (0, 2, 3, 1)).reshape(-1, KV_WIDTH, PAGE_SIZE)
    vt_pages = jnp.transpose(cache_v[l], (0, 2, 3, 1)).reshape(-1, KV_WIDTH, PAGE_SIZE)
    scores = _dec_scores(page_table, ikt_pages, iq8, w8, ik_s)
    n_sel = min(TOPK_KEYS_MAX, (past + ds) // 4)
    mask = _dec_select(scores.reshape(db, past + PAGE_SIZE), n_sel)
    y_att_s = _dec_attn(page_table, qn_s.reshape(db * ATT_HEADS, ATT_HEAD_DIM),
                        mask.reshape(db, 1, past + PAGE_SIZE), bias_dec, kn_s, v_s,
                        kt_pages, vt_pages)
    x1_s, hb_s = _merge(y_ssd_s, y_att_s.reshape(db, ATT_WIDTH), proj_s, xs_, wa, wb, wo, norm2_g[l], db)
    y_sample = _channel_mixer(hb_s, x1_s, wq, keys, ub, vtb, db)

    return (y_prompt.reshape(bsz, t, D_MODEL), y_sample.reshape(db, ds, D_MODEL),
            k_prompt, v_prompt, ik_prompt, ssm_prompt, conv_prompt,
            kn_s.reshape(1, db, ds, KV_HEADS, ATT_HEAD_DIM), v_s.reshape(1, db, ds, KV_HEADS, ATT_HEAD_DIM),
            ik_s.reshape(1, db, ds, IDX_DIM),
            h_s.reshape(1, db, SSD_HEADS, SSD_HEAD_DIM, D_STATE),
            conv_s.reshape(1, db, CONV_W - 1, CONV_DIM))
```

```python
import functools
import math

import numpy as np
import jax
import jax.numpy as jnp
from jax import lax
from jax.experimental import pallas as pl
from jax.experimental.pallas import tpu as pltpu

F32 = jnp.float32
BF16 = jnp.bfloat16
I32 = jnp.int32

D_MODEL = 1024
PAGE_SIZE = 128
D_INNER = 2048
SSD_HEAD_DIM = 64
SSD_HEADS = 32
SSD_GROUPS = 8
D_STATE = 128
CONV_W = 4
CONV_DIM = D_INNER + 2 * SSD_GROUPS * D_STATE
SSD_CHUNK = 128
ATT_HEADS = 16
ATT_HEAD_DIM = 64
ATT_WIDTH = ATT_HEADS * ATT_HEAD_DIM
KV_HEADS = 4
KV_WIDTH = KV_HEADS * ATT_HEAD_DIM
IDX_HEADS = 8
IDX_DIM = 64
TOPK_KEYS_MAX = 256
Q_BLOCK = 128
NUM_BUCKETS = 32
MAX_DISTANCE = 128
PEER_HEADS = 8
PEER_N_KEYS = 128
PEER_EXPERTS = PEER_N_KEYS * PEER_N_KEYS
PEER_HALF = 128
PEER_TOPK = 16
EPS = 1e-6

LANES = 128
SUBLANES = 8
VMEM_LIMIT_BYTES = 56 * 1024 * 1024

COL_XBC = 0
COL_Z = COL_XBC + CONV_DIM
COL_Q = COL_Z + D_INNER
COL_K = COL_Q + ATT_WIDTH
COL_V = COL_K + KV_WIDTH
COL_IQ = COL_V + KV_WIDTH
COL_GA = COL_IQ + IDX_HEADS * IDX_DIM
COL_GB = COL_GA + D_MODEL
COL_SMALL = COL_GB + D_MODEL
SMALL_DT = 0
SMALL_IK = SSD_HEADS
SMALL_IW = SSD_HEADS + IDX_DIM
PROJ_COLS = COL_SMALL + LANES
PROJ_TN = 1152

INT_MIN = -2 ** 31
NEG_INF = float("-inf")


def _cparams(sem):
    return pltpu.CompilerParams(dimension_semantics=sem, vmem_limit_bytes=VMEM_LIMIT_BYTES)


def _split3(x):
    hi = x.astype(BF16)
    r1 = x - hi.astype(F32)
    mid = r1.astype(BF16)
    lo = (r1 - mid.astype(F32)).astype(BF16)
    return hi, mid, lo


def _dot(a, b):
    return jnp.dot(a, b, preferred_element_type=F32)


def _dot_nt(a, b):
    return lax.dot_general(a, b, (((1,), (1,)), ((), ())), preferred_element_type=F32)


def _dot_tn(a, b):
    return lax.dot_general(a, b, (((0,), (0,)), ((), ())), preferred_element_type=F32)


def _exact_dot(x, onehot_bf16):
    hi, mid, lo = _split3(x)
    return _dot(hi, onehot_bf16) + _dot(mid, onehot_bf16) + _dot(lo, onehot_bf16)


def _in_proj_kernel(x_ref, g_ref, w_ref, o_ref, xn_ref):
    @pl.when(pl.program_id(1) == 0)
    def _():
        x = x_ref[...]
        ms = jnp.mean(x * x, axis=-1, keepdims=True)
        xn_ref[...] = (x * lax.rsqrt(ms + EPS) * g_ref[...]).astype(BF16)

    o_ref[...] = _dot(xn_ref[...], w_ref[...])


def _in_proj(x2d, g, w_perm, tm):
    n = x2d.shape[0]
    return pl.pallas_call(
        _in_proj_kernel,
        grid=(n // tm, PROJ_COLS // PROJ_TN),
        in_specs=[pl.BlockSpec((tm, D_MODEL), lambda i, j: (i, 0)),
                  pl.BlockSpec((1, D_MODEL), lambda i, j: (0, 0)),
                  pl.BlockSpec((D_MODEL, PROJ_TN), lambda i, j: (0, j))],
        out_specs=pl.BlockSpec((tm, PROJ_TN), lambda i, j: (i, j)),
        out_shape=jax.ShapeDtypeStruct((n, PROJ_COLS), F32),
        scratch_shapes=[pltpu.VMEM((tm, D_MODEL), BF16)],
        compiler_params=_cparams(("parallel", "arbitrary")),
        name="in_proj",
    )(x2d, g.reshape(1, D_MODEL), w_perm)


def _permute_w_in(w_in):
    offs = np.cumsum([0, D_INNER, CONV_DIM, SSD_HEADS, ATT_WIDTH, KV_WIDTH, KV_WIDTH,
                      IDX_HEADS * IDX_DIM, IDX_DIM, IDX_HEADS, D_MODEL, D_MODEL])
    z, xbc, dt, q, k, v, iq, ik, iw, ga, gb = [w_in[:, offs[i]:offs[i + 1]] for i in range(11)]
    pad = jnp.zeros((D_MODEL, LANES - SSD_HEADS - IDX_DIM - IDX_HEADS), w_in.dtype)
    return jnp.concatenate([xbc, z, q, k, v, iq, ga, gb, dt, ik, iw, pad], axis=1).astype(BF16)


def _seg_indicator(width, seg):
    m = np.zeros((width, LANES), np.float32)
    m[np.arange(width), np.arange(width) // seg] = 1.0
    return m


def _half_placement():
    lo = np.zeros((KV_WIDTH, KV_HEADS * LANES), np.float32)
    hi = np.zeros((KV_WIDTH, KV_HEADS * LANES), np.float32)
    c = np.arange(KV_WIDTH)
    g, d = c // ATT_HEAD_DIM, c % ATT_HEAD_DIM
    lo[c, g * LANES + d] = 1.0
    hi[c, g * LANES + ATT_HEAD_DIM + d] = 1.0
    return lo, hi


def _head_rms(x, ind, ind_t, gain):
    sq = x * x
    hi = sq.astype(BF16)
    lo = (sq - hi.astype(F32)).astype(BF16)
    ss = _dot(hi, ind) + _dot(lo, ind)
    r = lax.rsqrt(ss * (1.0 / ATT_HEAD_DIM) + EPS)
    rb = _exact_dot(r, ind_t)
    return x * rb * gain


def _attn_prep_kernel(q_ref, kv_ref, sm_ref, gq_ref, gk_ref, indq_ref, indqt_ref, indk_ref, indkt_ref,
                      plo_ref, phi_ref,
                      qn_ref, kn_ref, v_ref, ik_ref, klo_ref, khi_ref, vlo_ref, vhi_ref,
                      ikb_ref, iqb_ref, iw_ref):
    qn = _head_rms(q_ref[...], indq_ref[...], indqt_ref[...], gq_ref[...])
    qn_ref[...] = (qn * (ATT_HEAD_DIM ** -0.5)).astype(BF16)
    kv = kv_ref[...]
    kn = _head_rms(kv[:, :KV_WIDTH], indk_ref[...], indkt_ref[...], gk_ref[...])
    v = kv[:, KV_WIDTH:]
    kn_ref[...] = kn
    v_ref[...] = v
    knb = kn.astype(BF16)
    vb = v.astype(BF16)
    for src, place_ref, dst_ref in ((knb, plo_ref, klo_ref), (knb, phi_ref, khi_ref),
                                    (vb, plo_ref, vlo_ref), (vb, phi_ref, vhi_ref)):
        placed = _dot(src, place_ref[...]).astype(BF16)
        for g in range(KV_HEADS):
            dst_ref[g] = placed[:, g * LANES:(g + 1) * LANES]
    sm = sm_ref[...]
    ik = sm[:, SMALL_IK:SMALL_IK + IDX_DIM]
    ik_ref[...] = ik
    ikb_ref[...] = ik.astype(BF16)
    iw_ref[...] = sm * (IDX_HEADS ** -0.5 * IDX_DIM ** -0.5)


def _attn_prep(proj, q_norm_g, k_norm_g, tm):
    n = proj.shape[0]
    gq = jnp.tile(q_norm_g, ATT_HEADS).reshape(1, ATT_WIDTH)
    gk = jnp.tile(k_norm_g, KV_HEADS).reshape(1, KV_WIDTH)
    indq = _seg_indicator(ATT_WIDTH, ATT_HEAD_DIM)
    indk = _seg_indicator(KV_WIDTH, ATT_HEAD_DIM)
    plo, phi = _half_placement()
    consts = [jnp.asarray(a, BF16) for a in (indq, indq.T, indk, indk.T, plo, phi)]

    def full(a):
        return pl.BlockSpec(a.shape, lambda i: (0,) * a.ndim)

    outs = [
        ((n, ATT_WIDTH), BF16), ((n, KV_WIDTH), F32), ((n, KV_WIDTH), F32), ((n, IDX_DIM), F32),
        ((KV_HEADS, n, LANES), BF16), ((KV_HEADS, n, LANES), BF16),
        ((KV_HEADS, n, LANES), BF16), ((KV_HEADS, n, LANES), BF16),
        ((n, IDX_DIM), BF16), ((n, IDX_HEADS * IDX_DIM), BF16), ((n, LANES), F32),
    ]

    def body(q_ref, kv_ref, sm_ref, iq_ref, *rest):
        (gq_ref, gk_ref, indq_ref, indqt_ref, indk_ref, indkt_ref, plo_ref, phi_ref,
         qn_ref, kn_ref, v_ref, ik_ref, klo_ref, khi_ref, vlo_ref, vhi_ref, ikb_ref, iqb_ref, iw_ref) = rest
        _attn_prep_kernel(q_ref, kv_ref, sm_ref, gq_ref, gk_ref, indq_ref, indqt_ref, indk_ref, indkt_ref,
                          plo_ref, phi_ref, qn_ref, kn_ref, v_ref, ik_ref, klo_ref, khi_ref, vlo_ref, vhi_ref,
                          ikb_ref, iqb_ref, iw_ref)
        iqb_ref[...] = iq_ref[...].astype(BF16)

    return pl.pallas_call(
        body,
        grid=(n // tm,),
        in_specs=[pl.BlockSpec((tm, ATT_WIDTH), lambda i: (i, COL_Q // ATT_WIDTH)),
                  pl.BlockSpec((tm, 2 * KV_WIDTH), lambda i: (i, COL_K // (2 * KV_WIDTH))),
                  pl.BlockSpec((tm, LANES), lambda i: (i, COL_SMALL // LANES)),
                  pl.BlockSpec((tm, IDX_HEADS * IDX_DIM), lambda i: (i, COL_IQ // (IDX_HEADS * IDX_DIM))),
                  full(gq), full(gk)] + [full(c) for c in consts],
        out_specs=[pl.BlockSpec((tm, s[1]), lambda i: (i, 0)) if len(s) == 2
                   else pl.BlockSpec((KV_HEADS, tm, LANES), lambda i: (0, i, 0)) for s, _ in outs],
        out_shape=[jax.ShapeDtypeStruct(s, d) for s, d in outs],
        compiler_params=_cparams(("parallel",)),
        name="attn_prep",
    )(proj, proj, proj, proj, gq, gk, *consts)


def _merge_kernel(ya_ref, yb_ref, ga_ref, gb_ref, x_ref, wa_ref, wb_ref, wo_ref, g2_ref, x1_ref, h_ref):
    a = _dot(ya_ref[...].astype(BF16), wa_ref[...])
    b = _dot(yb_ref[...].astype(BF16), wb_ref[...])
    merged = jax.nn.sigmoid(ga_ref[...]) * a + jax.nn.sigmoid(gb_ref[...]) * b
    x1 = x_ref[...] + _dot(merged.astype(BF16), wo_ref[...])
    x1_ref[...] = x1
    ms = jnp.mean(x1 * x1, axis=-1, keepdims=True)
    h_ref[...] = (x1 * lax.rsqrt(ms + EPS) * g2_ref[...]).astype(BF16)


def _merge(y_ssd, y_att, proj, x2d, wa, wb, wo, g2, tm):
    n = x2d.shape[0]

    def full(a):
        return pl.BlockSpec(a.shape, lambda i: (0,) * a.ndim)

    g2 = g2.reshape(1, D_MODEL)
    return pl.pallas_call(
        _merge_kernel,
        grid=(n // tm,),
        in_specs=[pl.BlockSpec((tm, D_INNER), lambda i: (i, 0)),
                  pl.BlockSpec((tm, ATT_WIDTH), lambda i: (i, 0)),
                  pl.BlockSpec((tm, D_MODEL), lambda i: (i, COL_GA // D_MODEL)),
                  pl.BlockSpec((tm, D_MODEL), lambda i: (i, COL_GB // D_MODEL)),
                  pl.BlockSpec((tm, D_MODEL), lambda i: (i, 0)),
                  full(wa), full(wb), full(wo), full(g2)],
        out_specs=[pl.BlockSpec((tm, D_MODEL), lambda i: (i, 0)),
                   pl.BlockSpec((tm, D_MODEL), lambda i: (i, 0))],
        out_shape=[jax.ShapeDtypeStruct((n, D_MODEL), F32), jax.ShapeDtypeStruct((n, D_MODEL), BF16)],
        compiler_params=_cparams(("parallel",)),
        name="merge",
    )(y_ssd, y_att, proj, proj, x2d, wa, wb, wo, g2)


def _top_values(cur, k):
    vals = []
    for _ in range(k):
        m = jnp.max(cur, axis=0, keepdims=True)
        vals.append(m)
        cur = jnp.where(cur >= m, NEG_INF, cur)
    return vals


PEER_RANKS = PEER_TOPK + 1


def _peer_select_kernel(h_ref, wq_ref, keys_ref, th_ref, c1_ref, s2_ref, e2_ref):
    q = _dot(h_ref[...], wq_ref[...]).astype(BF16)
    t = q.shape[0]
    riota = lax.broadcasted_iota(I32, (SUBLANES, t), 0)
    for hd in range(PEER_HEADS):
        base = hd * 2 * PEER_HALF
        s1 = _dot_nt(keys_ref[0, hd], q[:, base:base + PEER_HALF])
        s2 = _dot_nt(keys_ref[1, hd], q[:, base + PEER_HALF:base + 2 * PEER_HALF])
        v1 = _top_values(s1, PEER_RANKS)
        v2 = _top_values(s2, PEER_RANKS)
        pad = jnp.full((SUBLANES - 1, t), NEG_INF, F32)
        v2a = jnp.concatenate(v2 + [pad], axis=0)
        blocks = [v1[0] + v2a, v1[1] + v2a[:SUBLANES]]
        for i in range(2, SUBLANES):
            blocks.append(jnp.where(riota < PEER_RANKS // (i + 1), v1[i] + v2a[:SUBLANES], NEG_INF))
        blocks.append(jnp.concatenate(v1[SUBLANES:] + [pad], axis=0) + v2[0])
        cand = jnp.concatenate(blocks, axis=0)
        top = _top_values(cand, PEER_RANKS)
        zsum = sum(jnp.exp(tv - top[0]) for tv in top[:PEER_TOPK])
        th = 0.5 * (top[PEER_TOPK - 1] + top[PEER_TOPK]) - s1
        c1 = jnp.exp(s1 - v1[0]) / zsum
        cw = th_ref.shape[-1]
        for ch in range(th_ref.shape[1]):
            th_ref[hd, ch] = th[:, ch * cw:(ch + 1) * cw]
            c1_ref[hd, ch] = c1[:, ch * cw:(ch + 1) * cw]
        s2_ref[hd] = s2
        e2_ref[hd] = jnp.exp(s2 - v2[0])


def _peer_select(hb, wq, keys, tn):
    n = hb.shape[0]
    big = jax.ShapeDtypeStruct((PEER_HEADS, PEER_N_KEYS, n), F32)
    bspec = pl.BlockSpec((PEER_HEADS, PEER_N_KEYS, tn), lambda i: (0, 0, i))
    cw = min(LANES, tn)
    chunked = jax.ShapeDtypeStruct((PEER_HEADS, n // cw, PEER_N_KEYS, cw), F32)
    cspec = pl.BlockSpec((PEER_HEADS, tn // cw, PEER_N_KEYS, cw), lambda i: (0, i, 0, 0))
    return pl.pallas_call(
        _peer_select_kernel,
        grid=(n // tn,),
        in_specs=[pl.BlockSpec((tn, D_MODEL), lambda i: (i, 0)),
                  pl.BlockSpec(wq.shape, lambda i: (0, 0)),
                  pl.BlockSpec(keys.shape, lambda i: (0, 0, 0, 0))],
        out_specs=[cspec, cspec, bspec, bspec],
        out_shape=[chunked, chunked, big, big],
        compiler_params=_cparams(("parallel",)),
        name="peer_select",
    )(hb, wq, keys)


def _gelu_tanh(x):
    return 0.5 * x * (1.0 + jnp.tanh(math.sqrt(2.0 / math.pi) * (x + 0.044715 * (x * x * x))))


PEER_EC = 1024
PEER_SUB = 512


def _peer_mix_kernel(h_ref, u_ref, vt_ref, th_ref, c1_ref, s2_ref, e2_ref, x1_ref, y_ref,
                     acc_ref, act_ref, g_ref, ht_ref):
    j = pl.program_id(1)

    @pl.when(j == 0)
    def _():
        acc_ref[...] = jnp.zeros_like(acc_ref)
        ht_ref[...] = h_ref[...].T

    tn = act_ref.shape[1]
    cw = min(LANES, tn)

    def gate_chunk(sc):
        start = sc * PEER_SUB
        erows = pl.ds(start if isinstance(sc, int) else pl.multiple_of(start, PEER_SUB), PEER_SUB)
        act_ref[...] = _gelu_tanh(_dot(u_ref[erows, :], ht_ref[...]))
        slot = sc % 2
        for ii in range(PEER_SUB // PEER_N_KEYS):
            i1 = j * (PEER_EC // PEER_N_KEYS) + sc * (PEER_SUB // PEER_N_KEYS) + ii
            rows = slice(ii * PEER_N_KEYS, (ii + 1) * PEER_N_KEYS)
            for ch in range(tn // cw):
                cols = slice(ch * cw, (ch + 1) * cw)
                grp = (PEER_N_KEYS // SUBLANES, SUBLANES, cw)
                w = jnp.zeros(grp, F32)
                for hd in range(PEER_HEADS):
                    thb = jnp.broadcast_to(th_ref[hd, ch, pl.ds(i1, 1), :], (SUBLANES, cw))
                    c1b = jnp.broadcast_to(c1_ref[hd, ch, pl.ds(i1, 1), :], (SUBLANES, cw))
                    chosen = s2_ref[hd, :, cols].reshape(grp) >= thb
                    w = w + jnp.where(chosen, e2_ref[hd, :, cols].reshape(grp) * c1b, 0.0)
                g = act_ref[rows, cols].reshape(grp) * w
                g_ref[slot, rows, cols] = g.reshape(PEER_N_KEYS, cw).astype(BF16)

    def project_chunk(sc):
        acc_ref[...] += _dot(vt_ref[sc], g_ref[sc % 2])

    gate_chunk(0)

    def pipelined(sc, _):
        project_chunk(sc - 1)
        gate_chunk(sc)
        return 0

    nsub = PEER_EC // PEER_SUB
    lax.fori_loop(1, nsub, pipelined, 0)
    project_chunk(nsub - 1)

    @pl.when(j == pl.num_programs(1) - 1)
    def _():
        y_ref[...] = x1_ref[...] + acc_ref[...].T


def _peer_mix(hb, ub, vtb, sel, x1, tn):
    n = hb.shape[0]
    th, c1, s2, e2 = sel
    bspec = pl.BlockSpec((PEER_HEADS, PEER_N_KEYS, tn), lambda i, j: (0, 0, i))
    cw = min(LANES, tn)
    cspec = pl.BlockSpec((PEER_HEADS, tn // cw, PEER_N_KEYS, cw), lambda i, j: (0, i, 0, 0))
    return pl.pallas_call(
        _peer_mix_kernel,
        grid=(n // tn, PEER_EXPERTS // PEER_EC),
        in_specs=[pl.BlockSpec((tn, D_MODEL), lambda i, j: (i, 0)),
                  pl.BlockSpec((PEER_EC, D_MODEL), lambda i, j: (j, 0)),
                  pl.BlockSpec((PEER_EC // PEER_SUB, D_MODEL, PEER_SUB), lambda i, j: (j, 0, 0)),
                  cspec, cspec, bspec, bspec,
                  pl.BlockSpec((tn, D_MODEL), lambda i, j: (i, 0))],
        out_specs=pl.BlockSpec((tn, D_MODEL), lambda i, j: (i, 0)),
        out_shape=jax.ShapeDtypeStruct((n, D_MODEL), F32),
        scratch_shapes=[pltpu.VMEM((D_MODEL, tn), F32),
                        pltpu.VMEM((PEER_SUB, tn), F32),
                        pltpu.VMEM((2, PEER_SUB, tn), BF16),
                        pltpu.VMEM((D_MODEL, tn), BF16)],
        compiler_params=_cparams(("parallel", "arbitrary")),
        name="peer_mix",
    )(hb, ub, vtb, th, c1, s2, e2, x1)


HALO = 8
N_PAIRS = SSD_HEADS // 2
GROUP_W = D_INNER // SSD_GROUPS


def _softplus(x):
    return jnp.maximum(x, 0.0) + jnp.log1p(jnp.exp(-jnp.abs(x)))


def _silu(x):
    return x * jax.nn.sigmoid(x)


def _lane_bcast(col):
    return jnp.broadcast_to(col, (col.shape[0], LANES))


def _ssd_gate_norm(y, xs, z, dvec, gain):
    y = (y + dvec * xs) * _silu(z)
    parts = []
    for g in range(SSD_GROUPS):
        yg = y[:, g * GROUP_W:(g + 1) * GROUP_W]
        ms = jnp.mean(yg * yg, axis=-1, keepdims=True)
        parts.append(yg * lax.rsqrt(ms + EPS))
    return jnp.concatenate(parts, axis=1) * gain


def _ssd_prompt_kernel(xbc_ref, z_ref, sm_ref, cw_ref, cb_ref, dtb_ref, alog_ref, dvec_ref, gain_ref, tri_ref,
                       y_ref, hout_ref, ext_ref, h_ref, yacc_ref):
    c = pl.program_id(1)

    @pl.when(c == 0)
    def _():
        ext_ref[0:HALO, :] = jnp.zeros((HALO, CONV_DIM), F32)
        h_ref[...] = jnp.zeros_like(h_ref)

    x = xbc_ref[...]
    ext_ref[HALO:HALO + SSD_CHUNK, :] = x
    conv = cb_ref[...]
    for j in range(CONV_W):
        conv = conv + cw_ref[j:j + 1, :] * ext_ref[pl.ds(HALO - (CONV_W - 1) + j, SSD_CHUNK), :]
    ext_ref[0:HALO, :] = x[SSD_CHUNK - HALO:, :]
    xc = _silu(conv)
    xs = xc[:, :D_INNER]

    lane = lax.broadcasted_iota(I32, (SSD_CHUNK, LANES), 1)
    row = lax.broadcasted_iota(I32, (SSD_CHUNK, LANES), 0)
    dt = jnp.where(lane < SSD_HEADS, _softplus(sm_ref[...] + dtb_ref[...]), 0.0)
    da = dt * (-jnp.exp(alog_ref[...]))
    hi, mid, lo = _split3(da)
    tri = tri_ref[...]
    acum = _dot(tri, hi) + _dot(tri, mid) + _dot(tri, lo)
    acum_t = acum.T
    dt_t = dt.T
    alast = acum[SSD_CHUNK - 1:SSD_CHUNK, :]
    e_in = jnp.exp(acum)
    e_out = jnp.exp(alast - acum) * dt
    e_all = jnp.exp(alast)
    causal = row >= lane
    lo_half = lane < SSD_HEAD_DIM

    for g in range(SSD_GROUPS):
        bg = xc[:, D_INNER + g * D_STATE:D_INNER + (g + 1) * D_STATE].astype(BF16)
        cg = xc[:, D_INNER + SSD_GROUPS * D_STATE + g * D_STATE:
                D_INNER + SSD_GROUPS * D_STATE + (g + 1) * D_STATE].astype(BF16)
        cbm = _dot_nt(cg, bg)
        for pp in range(2):
            pair = 2 * g + pp
            r0 = 2 * pair
            xp = xs[:, r0 * SSD_HEAD_DIM:(r0 + 2) * SSD_HEAD_DIM]
            yp = jnp.zeros((SSD_CHUNK, LANES), F32)
            for hh in range(2):
                r = r0 + hh
                seg = _lane_bcast(acum[:, r:r + 1]) - acum_t[r:r + 1, :]
                lmat = jnp.exp(jnp.where(causal, seg, NEG_INF))
                m = (cbm * lmat * dt_t[r:r + 1, :]).astype(BF16)
                xm = jnp.where(lo_half == (hh == 0), xp, 0.0).astype(BF16)
                yp = yp + _dot(m, xm)
            hp = h_ref[pair]
            scale_in = jnp.where(lo_half, _lane_bcast(e_in[:, r0:r0 + 1]), _lane_bcast(e_in[:, r0 + 1:r0 + 2]))
            yp = yp + _dot_nt(cg, hp.astype(BF16)) * scale_in
            scale_out = jnp.where(lo_half, _lane_bcast(e_out[:, r0:r0 + 1]), _lane_bcast(e_out[:, r0 + 1:r0 + 2]))
            xd = (xp * scale_out).astype(BF16)
            hdec = jnp.where(row < SSD_HEAD_DIM, e_all[:, r0:r0 + 1], e_all[:, r0 + 1:r0 + 2])
            h_ref[pair] = hp * hdec + _dot_tn(xd, bg)
            yacc_ref[:, r0 * SSD_HEAD_DIM:(r0 + 2) * SSD_HEAD_DIM] = yp

    y_ref[...] = _ssd_gate_norm(yacc_ref[...], xs, z_ref[...], dvec_ref[...], gain_ref[...])

    @pl.when(c == pl.num_programs(1) - 1)
    def _():
        hout_ref[0] = h_ref[...]


def _pad_lanes(v):
    return jnp.zeros((1, LANES), F32).at[0, :v.shape[0]].set(v)


def _ssd_prompt(proj, bsz, conv_w, conv_b, dt_bias, a_log, d_skip, ssd_norm_g):
    n = proj.shape[0]
    nc = n // bsz // SSD_CHUNK
    tri = jnp.asarray(np.tril(np.ones((SSD_CHUNK, SSD_CHUNK), np.float32)), BF16)
    dvec = jnp.repeat(d_skip, SSD_HEAD_DIM).reshape(1, D_INNER)
    small = [conv_w, conv_b.reshape(1, CONV_DIM), _pad_lanes(dt_bias), _pad_lanes(a_log), dvec,
             ssd_norm_g.reshape(1, D_INNER), tri]

    def full(a):
        return pl.BlockSpec(a.shape, lambda b, c: (0,) * a.ndim)

    return pl.pallas_call(
        _ssd_prompt_kernel,
        grid=(bsz, nc),
        in_specs=[pl.BlockSpec((SSD_CHUNK, CONV_DIM), lambda b, c: (b * nc + c, 0)),
                  pl.BlockSpec((SSD_CHUNK, D_INNER), lambda b, c: (b * nc + c, COL_Z // D_INNER)),
                  pl.BlockSpec((SSD_CHUNK, LANES), lambda b, c: (b * nc + c, COL_SMALL // LANES))]
                 + [full(a) for a in small],
        out_specs=[pl.BlockSpec((SSD_CHUNK, D_INNER), lambda b, c: (b * nc + c, 0)),
                   pl.BlockSpec((1, N_PAIRS, LANES, D_STATE), lambda b, c: (b, 0, 0, 0))],
        out_shape=[jax.ShapeDtypeStruct((n, D_INNER), F32),
                   jax.ShapeDtypeStruct((bsz, N_PAIRS, LANES, D_STATE), F32)],
        scratch_shapes=[pltpu.VMEM((HALO + SSD_CHUNK, CONV_DIM), F32),
                        pltpu.VMEM((N_PAIRS, LANES, D_STATE), F32),
                        pltpu.VMEM((SSD_CHUNK, D_INNER), F32)],
        compiler_params=_cparams(("parallel", "arbitrary")),
        name="ssd_prompt",
    )(proj, proj, proj, *small)


def _t5_bucket_np(dist):
    n = np.maximum(dist, 0)
    max_exact = NUM_BUCKETS // 2
    nf = np.maximum(n, 1).astype(np.float32)
    ratio = np.log(nf / np.float32(max_exact)) / np.float32(math.log(MAX_DISTANCE / max_exact))
    large = max_exact + (ratio * np.float32(NUM_BUCKETS - max_exact)).astype(np.int32)
    large = np.minimum(large, NUM_BUCKETS - 1)
    return np.where(n < max_exact, n, large).astype(np.int32)


BIAS_TJ = 2048


def _bias_kernel(rel_t_ref, bucket_ref, o_ref):
    ids = lax.broadcasted_iota(I32, (NUM_BUCKETS, BIAS_TJ), 0)
    onehot = jnp.where(ids == bucket_ref[...], 1.0, 0.0).astype(BF16)
    o_ref[...] = _exact_dot(rel_t_ref[...], onehot)


def _bias_lookup(rel_bias, buckets):
    j = buckets.shape[0]
    return pl.pallas_call(
        _bias_kernel,
        grid=(j // BIAS_TJ,),
        in_specs=[pl.BlockSpec((ATT_HEADS, NUM_BUCKETS), lambda i: (0, 0)),
                  pl.BlockSpec((1, BIAS_TJ), lambda i: (0, i))],
        out_specs=pl.BlockSpec((ATT_HEADS, BIAS_TJ), lambda i: (0, i)),
        out_shape=jax.ShapeDtypeStruct((ATT_HEADS, j), F32),
        compiler_params=_cparams(("parallel",)),
        name="bias_lookup",
    )(rel_bias.T, jnp.asarray(buckets.reshape(1, j)))


def _prompt_bias_buckets():
    i = np.arange(Q_BLOCK)[:, None]
    j = np.arange(Q_BLOCK)[None, :]
    far = np.full((Q_BLOCK, Q_BLOCK), 2 * Q_BLOCK)
    return _t5_bucket_np(np.stack([far, i - j + Q_BLOCK, i - j]))


KEY_TILES = 4
KEY_CHUNK = KEY_TILES * Q_BLOCK


def _fold_tiles(x, op):
    out = x[:, 0:LANES]
    for u in range(1, x.shape[1] // LANES):
        out = op(out, x[:, u * LANES:(u + 1) * LANES])
    return out


def _sortable_key(x):
    b = pltpu.bitcast(x, I32)
    return jnp.where(b < 0, b ^ 0x7FFFFFFF, b)


def _kth_largest_key(count_ge, k, shape):
    def body(it, cand):
        trial = cand + jnp.left_shift(jnp.int32(1), 31 - it)
        return jnp.where(count_ge(trial) >= k, trial, cand)
    return lax.fori_loop(0, 32, body, jnp.full(shape, INT_MIN, I32))


def _chunks_before(qb):
    a, r = qb // KEY_TILES, qb % KEY_TILES
    return KEY_TILES * (a * (a + 1) // 2) + r * (a + 1)


def _attn_select_kernel(n_sel, nqb, iq_ref, iw_ref, ik_ref, triu_ref, mask_ref, key_ref, cand_ref):
    nch = mask_ref.shape[1]
    row = lax.broadcasted_iota(I32, (Q_BLOCK, KEY_CHUNK), 0)
    lane = lax.broadcasted_iota(I32, (Q_BLOCK, KEY_CHUNK), 1)

    def admissible(qb, c):
        return lane + (c * KEY_CHUNK - qb * Q_BLOCK) <= row

    def score_block(qb, _):
        rows = pl.ds(pl.multiple_of(qb * Q_BLOCK, Q_BLOCK), Q_BLOCK)
        iw = iw_ref[rows, :]
        iq = iq_ref[rows, :]
        wcols = [jnp.broadcast_to(iw[:, SMALL_IW + h:SMALL_IW + h + 1], (Q_BLOCK, KEY_CHUNK))
                 for h in range(IDX_HEADS)]
        iqs = [iq[:, h * IDX_DIM:(h + 1) * IDX_DIM] for h in range(IDX_HEADS)]
        base = _chunks_before(qb)

        def score_chunk(c, _):
            ikc = ik_ref[pl.ds(pl.multiple_of(c * KEY_CHUNK, KEY_CHUNK), KEY_CHUNK), :]
            sc = jnp.zeros((Q_BLOCK, KEY_CHUNK), F32)
            for h in range(IDX_HEADS):
                sc = sc + wcols[h] * jnp.maximum(_dot_nt(iqs[h], ikc), 0.0)
            key_ref[base + c] = _sortable_key(jnp.where(admissible(qb, c), sc, NEG_INF))
            return 0

        lax.fori_loop(0, qb // KEY_TILES + 1, score_chunk, 0)
        cand_ref[qb] = jnp.full((Q_BLOCK, LANES), INT_MIN, I32)
        return 0

    lax.fori_loop(0, nqb, score_block, 0)

    def search_bit(it, _):
        bit = jnp.left_shift(jnp.int32(1), 31 - it)
        for qb in range(nqb):
            trial = cand_ref[qb] + bit
            acc = jnp.zeros((Q_BLOCK, LANES), I32)
            for c in range(qb // KEY_TILES + 1):
                k = key_ref[_chunks_before(qb) + c]
                for u in range(KEY_TILES):
                    acc = acc + jnp.where(k[:, u * LANES:(u + 1) * LANES] >= trial, 1, 0)
            enough = jnp.sum(acc, axis=1, keepdims=True) >= n_sel
            cand_ref[qb] = jnp.where(enough, trial, cand_ref[qb])
        return 0

    lax.fori_loop(0, 32, search_bit, 0)

    def mask_block(qb, _):
        tau = cand_ref[qb][:, 0:1]
        base = _chunks_before(qb)
        nchunk = qb // KEY_TILES + 1

        def count_gt(c, acc):
            return acc + _fold_tiles(jnp.where(key_ref[base + c] > tau, 1, 0), jnp.add)
        n_gt = jnp.sum(lax.fori_loop(0, nchunk, count_gt, jnp.zeros((Q_BLOCK, LANES), I32)), axis=1, keepdims=True)
        need = (n_sel - n_gt).astype(F32)

        def mask_chunk(c, seen):
            k = key_ref[base + c]
            eq = k == tau
            eqf = jnp.where(eq, 1.0, 0.0)
            pref = _dot(eqf.astype(BF16), triu_ref[...]) + seen
            sel = jnp.logical_or(k > tau, jnp.logical_and(eq, pref <= need))
            mask_ref[qb, c] = jnp.where(jnp.logical_and(sel, admissible(qb, c)), 0.0, NEG_INF).astype(BF16)
            return seen + jnp.sum(eqf, axis=1, keepdims=True)

        lax.fori_loop(0, nchunk, mask_chunk, jnp.zeros((Q_BLOCK, 1), F32))

        def blank_chunk(c, _):
            mask_ref[qb, c] = jnp.full((Q_BLOCK, KEY_CHUNK), NEG_INF, BF16)
            return 0

        lax.fori_loop(nchunk, nch, blank_chunk, 0)
        return 0

    lax.fori_loop(0, nqb, mask_block, 0)


def _attn_select(prep, bsz):
    ikb, iqb, iw = prep[8], prep[9], prep[10]
    n = iqb.shape[0]
    t = n // bsz
    nqb = t // Q_BLOCK
    assert t % KEY_CHUNK == 0
    nch = t // KEY_CHUNK
    n_sel = min(TOPK_KEYS_MAX, t // 4)
    triu = jnp.asarray(np.triu(np.ones((KEY_CHUNK, KEY_CHUNK), np.float32)), BF16)
    n_key_chunks = int(_chunks_before(nqb))
    return pl.pallas_call(
        functools.partial(_attn_select_kernel, n_sel, nqb),
        grid=(bsz,),
        in_specs=[pl.BlockSpec((t, IDX_HEADS * IDX_DIM), lambda b: (b, 0)),
                  pl.BlockSpec((t, LANES), lambda b: (b, 0)),
                  pl.BlockSpec((t, IDX_DIM), lambda b: (b, 0)),
                  pl.BlockSpec(triu.shape, lambda b: (0, 0))],
        out_specs=pl.BlockSpec((nqb, nch, Q_BLOCK, KEY_CHUNK), lambda b: (b, 0, 0, 0)),
        out_shape=jax.ShapeDtypeStruct((bsz * nqb, nch, Q_BLOCK, KEY_CHUNK), BF16),
        scratch_shapes=[pltpu.VMEM((n_key_chunks, Q_BLOCK, KEY_CHUNK), I32),
                        pltpu.VMEM((nqb, Q_BLOCK, LANES), I32)],
        compiler_params=_cparams(("parallel",)),
        name="attn_select",
    )(iqb, iw, ikb, triu)


HEADS_PER_KV = ATT_HEADS // KV_HEADS


def _attn_prompt_kernel(q_ref, mask_ref, klo_ref, khi_ref, vlo_ref, vhi_ref, bt_ref, y_ref, lbuf_ref):
    qb = pl.program_id(1)
    g = pl.program_id(2)
    nchunk = qb // KEY_TILES + 1

    k_refs = (klo_ref, khi_ref)
    v_refs = (vlo_ref, vhi_ref)
    qps = [q_ref[:, pp * LANES:(pp + 1) * LANES] for pp in range(HEADS_PER_KV // 2)]

    def logits_chunk(c, mxs):
        rows = pl.ds(pl.multiple_of(c * KEY_CHUNK, KEY_CHUNK), KEY_CHUNK)
        kinds = [jnp.clip(c * KEY_TILES + u - qb + 2, 0, 2) for u in range(KEY_TILES)]
        mask = mask_ref[0, c].astype(F32)
        new = []
        for hh in range(HEADS_PER_KV):
            bias = jnp.concatenate([bt_ref[g * HEADS_PER_KV + hh, kd] for kd in kinds], axis=1)
            lg = _dot_nt(qps[hh // 2], k_refs[hh % 2][g, rows, :]) + bias + mask
            lbuf_ref[hh, c] = lg
            new.append(jnp.maximum(mxs[hh], _fold_tiles(lg, jnp.maximum)))
        return tuple(new)

    neg = jnp.full((Q_BLOCK, LANES), NEG_INF, F32)
    mxs = lax.fori_loop(0, nchunk, logits_chunk, (neg,) * HEADS_PER_KV)
    mrows = [jnp.max(m, axis=1, keepdims=True) for m in mxs]

    def pv_chunk(c, carry):
        rows = pl.ds(pl.multiple_of(c * KEY_CHUNK, KEY_CHUNK), KEY_CHUNK)
        new = []
        for hh in range(HEADS_PER_KV):
            lsum, acc = carry[hh]
            p = jnp.exp(lbuf_ref[hh, c] - mrows[hh])
            new.append((lsum + _fold_tiles(p, jnp.add), acc + _dot(p.astype(BF16), v_refs[hh % 2][g, rows, :])))
        return tuple(new)

    zero = jnp.zeros((Q_BLOCK, LANES), F32)
    res = lax.fori_loop(0, nchunk, pv_chunk, ((zero, zero),) * HEADS_PER_KV)
    outs = [acc / jnp.sum(lsum, axis=1, keepdims=True) for lsum, acc in res]
    for pp in range(HEADS_PER_KV // 2):
        y_ref[:, pp * LANES:(pp + 1) * LANES] = outs[2 * pp] + outs[2 * pp + 1]


def _attn_prompt(prep, mask, btile, bsz):
    qn, klo, khi, vlo, vhi = prep[0], prep[4], prep[5], prep[6], prep[7]
    n = qn.shape[0]
    t = n // bsz
    nqb = t // Q_BLOCK
    nch = t // KEY_CHUNK
    kvspec = pl.BlockSpec((KV_HEADS, t, LANES), lambda b, q, g: (0, b, 0))
    qw = HEADS_PER_KV * ATT_HEAD_DIM
    return pl.pallas_call(
        _attn_prompt_kernel,
        grid=(bsz, nqb, KV_HEADS),
        in_specs=[pl.BlockSpec((Q_BLOCK, qw), lambda b, q, g: (b * nqb + q, g)),
                  pl.BlockSpec((1, nch, Q_BLOCK, KEY_CHUNK), lambda b, q, g: (b * nqb + q, 0, 0, 0)),
                  kvspec, kvspec, kvspec, kvspec,
                  pl.BlockSpec(btile.shape, lambda b, q, g: (0, 0, 0, 0))],
        out_specs=pl.BlockSpec((Q_BLOCK, qw), lambda b, q, g: (b * nqb + q, g)),
        out_shape=jax.ShapeDtypeStruct((n, ATT_WIDTH), F32),
        scratch_shapes=[pltpu.VMEM((HEADS_PER_KV, nch, Q_BLOCK, KEY_CHUNK), F32)],
        compiler_params=_cparams(("parallel", "parallel", "arbitrary")),
        name="attn_prompt",
    )(qn, mask, klo, khi, vlo, vhi, btile)


def _head_expand_indicator():
    m = np.zeros((LANES, D_INNER), np.float32)
    m[np.arange(D_INNER) // SSD_HEAD_DIM, np.arange(D_INNER)] = 1.0
    return m


def _ssd_step_kernel(xbc_ref, z_ref, sm_ref, sc_ref, hin_ref, cw_ref, cb_ref, dtb_ref, alog_ref, dvec_ref, gain_ref,
                     exp_ref, y_ref, hout_ref, cs_ref, xs_ref, bm_ref, cm_ref, dtx_t_ref, dec_t_ref, y_t_ref):
    b = pl.program_id(0)
    db = xbc_ref.shape[0]

    @pl.when(b == 0)
    def _():
        x = xbc_ref[...]
        sc = sc_ref[...]
        conv = cb_ref[...] + cw_ref[CONV_W - 1:CONV_W, :] * x
        for j in range(CONV_W - 1):
            conv = conv + cw_ref[j:j + 1, :] * sc[:, j * CONV_DIM:(j + 1) * CONV_DIM]
        cs_ref[:, :(CONV_W - 2) * CONV_DIM] = sc[:, CONV_DIM:]
        cs_ref[:, (CONV_W - 2) * CONV_DIM:] = x
        xc = _silu(conv)
        xs = xc[:, :D_INNER]
        lane = lax.broadcasted_iota(I32, (db, LANES), 1)
        dt = jnp.where(lane < SSD_HEADS, _softplus(sm_ref[...] + dtb_ref[...]), 0.0)
        dec = jnp.exp(dt * (-jnp.exp(alog_ref[...])))
        xs_ref[0:db, :] = xs
        bm_ref[0:db, :] = xc[:, D_INNER:D_INNER + SSD_GROUPS * D_STATE]
        cm_ref[0:db, :] = xc[:, D_INNER + SSD_GROUPS * D_STATE:]
        dtx_t_ref[:, 0:db] = (_exact_dot(dt, exp_ref[...]) * xs).T
        dec_t_ref[:, 0:db] = _exact_dot(dec, exp_ref[...]).T
        y_t_ref[...] = jnp.zeros_like(y_t_ref)

    lane = lax.broadcasted_iota(I32, (LANES, LANES), 1)
    mine = lane == b
    base = pl.multiple_of((b // 8) * 8, 8)
    my_row = lax.broadcasted_iota(I32, (8, SSD_GROUPS * D_STATE), 0) == (b % 8)
    bm_b = jnp.sum(jnp.where(my_row, bm_ref[pl.ds(base, 8), :], 0.0), axis=0, keepdims=True)
    cm_b = jnp.sum(jnp.where(my_row, cm_ref[pl.ds(base, 8), :], 0.0), axis=0, keepdims=True)
    for pair in range(N_PAIRS):
        g = pair // 2
        rows = slice(pair * LANES, (pair + 1) * LANES)
        xcol = jnp.sum(jnp.where(mine, dtx_t_ref[rows, :], 0.0), axis=1, keepdims=True)
        dcol = jnp.sum(jnp.where(mine, dec_t_ref[rows, :], 0.0), axis=1, keepdims=True)
        brow = bm_b[:, g * D_STATE:(g + 1) * D_STATE]
        crow = cm_b[:, g * D_STATE:(g + 1) * D_STATE]
        hnew = hin_ref[0, pair] * dcol + xcol * brow
        hout_ref[0, pair] = hnew
        ycol = jnp.sum(hnew * crow, axis=1, keepdims=True)
        y_t_ref[rows, :] = jnp.where(mine, ycol, y_t_ref[rows, :])

    @pl.when(b == pl.num_programs(0) - 1)
    def _():
        y = y_t_ref[...].T[0:db, :]
        y_ref[...] = _ssd_gate_norm(y, xs_ref[0:db, :], z_ref[...], dvec_ref[...], gain_ref[...])


def _ssd_step(proj, state_conv2d, state_ssm4d, conv_w, conv_b, dt_bias, a_log, d_skip, ssd_norm_g):
    db = proj.shape[0]
    assert db <= LANES and db % 8 == 0
    dvec = jnp.repeat(d_skip, SSD_HEAD_DIM).reshape(1, D_INNER)
    small = [conv_w, conv_b.reshape(1, CONV_DIM), _pad_lanes(dt_bias), _pad_lanes(a_log), dvec,
             ssd_norm_g.reshape(1, D_INNER), jnp.asarray(_head_expand_indicator(), BF16)]

    def full(a):
        return pl.BlockSpec(a.shape, lambda b: (0,) * a.ndim)

    cs_w = (CONV_W - 1) * CONV_DIM
    hspec = pl.BlockSpec((1, N_PAIRS, LANES, D_STATE), lambda b: (b, 0, 0, 0))
    return pl.pallas_call(
        _ssd_step_kernel,
        grid=(db,),
        in_specs=[pl.BlockSpec((db, CONV_DIM), lambda b: (0, 0)),
                  pl.BlockSpec((db, D_INNER), lambda b: (0, COL_Z // D_INNER)),
                  pl.BlockSpec((db, LANES), lambda b: (0, COL_SMALL // LANES)),
                  pl.BlockSpec((db, cs_w), lambda b: (0, 0)),
                  hspec] + [full(a) for a in small],
        out_specs=[pl.BlockSpec((db, D_INNER), lambda b: (0, 0)),
                   hspec,
                   pl.BlockSpec((db, cs_w), lambda b: (0, 0))],
        out_shape=[jax.ShapeDtypeStruct((db, D_INNER), F32),
                   jax.ShapeDtypeStruct(state_ssm4d.shape, F32),
                   jax.ShapeDtypeStruct((db, cs_w), F32)],
        scratch_shapes=[pltpu.VMEM((LANES, D_INNER), F32),
                        pltpu.VMEM((LANES, SSD_GROUPS * D_STATE), F32),
                        pltpu.VMEM((LANES, SSD_GROUPS * D_STATE), F32),
                        pltpu.VMEM((D_INNER, LANES), F32),
                        pltpu.VMEM((D_INNER, LANES), F32),
                        pltpu.VMEM((D_INNER, LANES), F32)],
        compiler_params=_cparams(("arbitrary",)),
        name="ssd_step",
    )(proj, proj, proj, state_conv2d, state_ssm4d, *small)


def _page_copies(pt_ref, seq, cache_ref, buf_ref, slot, sem_ref, n_pages):
    return [pltpu.make_async_copy(cache_ref.at[pt_ref[seq, p]], buf_ref.at[slot, p], sem_ref.at[slot])
            for p in range(n_pages)]


def _paged_fetch(pt_ref, streams, n_pages):
    b = pl.program_id(0)
    slot = b % 2

    @pl.when(b == 0)
    def _():
        for cache_ref, buf_ref, sem_ref in streams:
            for cp in _page_copies(pt_ref, 0, cache_ref, buf_ref, 0, sem_ref, n_pages):
                cp.start()

    @pl.when(b + 1 < pl.num_programs(0))
    def _():
        for cache_ref, buf_ref, sem_ref in streams:
            for cp in _page_copies(pt_ref, b + 1, cache_ref, buf_ref, 1 - slot, sem_ref, n_pages):
                cp.start()

    for cache_ref, buf_ref, sem_ref in streams:
        for cp in _page_copies(pt_ref, b, cache_ref, buf_ref, slot, sem_ref, n_pages):
            cp.wait()
    return slot


def _dec_score_kernel(n_pages, pt_ref, iq_ref, w_ref, iknew_ref, cache_ref, o_ref, buf_ref, sem_ref):
    b = pl.program_id(0)
    slot = _paged_fetch(pt_ref, [(cache_ref, buf_ref, sem_ref)], n_pages)
    past = n_pages * PAGE_SIZE
    iq = iq_ref[...].astype(BF16)
    w = w_ref[...]
    for p in range(n_pages):
        s = _dot(iq, buf_ref[slot, p].astype(BF16))
        o_ref[0, :, p * PAGE_SIZE:(p + 1) * PAGE_SIZE] = jnp.sum(w * jnp.maximum(s, 0.0), axis=0, keepdims=True)
    iknew = iknew_ref[pl.ds(b, 1), :].astype(BF16).astype(F32)
    s_new = jnp.sum(iq.astype(F32) * iknew, axis=1, keepdims=True)
    sc_new = jnp.sum(w * jnp.maximum(s_new, 0.0), axis=0, keepdims=True)
    lane = lax.broadcasted_iota(I32, (1, LANES), 1)
    o_ref[0, :, past:past + LANES] = jnp.where(lane == 0, sc_new, NEG_INF)


def _dec_scores(page_table, cache_ik, iq8, w8, ikb_new):
    db, n_pages = page_table.shape
    s_pad = (n_pages + 1) * PAGE_SIZE
    grid_spec = pltpu.PrefetchScalarGridSpec(
        num_scalar_prefetch=1,
        grid=(db,),
        in_specs=[pl.BlockSpec((IDX_HEADS, IDX_DIM), lambda b, pt: (b, 0)),
                  pl.BlockSpec((IDX_HEADS, 1), lambda b, pt: (b, 0)),
                  pl.BlockSpec(ikb_new.shape, lambda b, pt: (0, 0)),
                  pl.BlockSpec(memory_space=pl.ANY)],
        out_specs=pl.BlockSpec((1, 1, s_pad), lambda b, pt: (b, 0, 0)),
        scratch_shapes=[pltpu.VMEM((2, n_pages, IDX_DIM, PAGE_SIZE), F32),
                        pltpu.SemaphoreType.DMA((2,))],
    )
    return pl.pallas_call(
        functools.partial(_dec_score_kernel, n_pages),
        grid_spec=grid_spec,
        out_shape=jax.ShapeDtypeStruct((db, 1, s_pad), F32),
        compiler_params=_cparams(("arbitrary",)),
        name="dec_scores",
    )(page_table, iq8, w8, ikb_new, cache_ik)


def _dec_select_kernel(n_sel, sc_ref, triu_ref, mask_ref, key_ref):
    db, s_pad = sc_ref.shape
    ntile = s_pad // LANES
    key_ref[...] = _sortable_key(sc_ref[...])

    def count_ge(trial):
        acc = jnp.zeros((db, LANES), I32)
        for j in range(ntile):
            acc = acc + jnp.where(key_ref[:, j * LANES:(j + 1) * LANES] >= trial, 1, 0)
        return jnp.sum(acc, axis=1, keepdims=True)

    tau = _kth_largest_key(count_ge, n_sel, (db, 1))
    acc = jnp.zeros((db, LANES), I32)
    for j in range(ntile):
        acc = acc + jnp.where(key_ref[:, j * LANES:(j + 1) * LANES] > tau, 1, 0)
    need = (n_sel - jnp.sum(acc, axis=1, keepdims=True)).astype(F32)
    seen = jnp.zeros((db, 1), F32)
    for j in range(ntile):
        k = key_ref[:, j * LANES:(j + 1) * LANES]
        eq = k == tau
        pref = _dot(jnp.where(eq, 1.0, 0.0).astype(BF16), triu_ref[...]) + seen
        sel = jnp.logical_or(k > tau, jnp.logical_and(eq, pref <= need))
        mask_ref[:, j * LANES:(j + 1) * LANES] = jnp.where(sel, 0.0, NEG_INF)
        seen = pref[:, LANES - 1:LANES]


def _dec_select(scores2d, n_sel):
    db, s_pad = scores2d.shape
    triu = jnp.asarray(np.triu(np.ones((LANES, LANES), np.float32)), BF16)
    return pl.pallas_call(
        functools.partial(_dec_select_kernel, n_sel),
        grid=(1,),
        in_specs=[pl.BlockSpec((db, s_pad), lambda i: (0, 0)),
                  pl.BlockSpec(triu.shape, lambda i: (0, 0))],
        out_specs=pl.BlockSpec((db, s_pad), lambda i: (0, 0)),
        out_shape=jax.ShapeDtypeStruct((db, s_pad), F32),
        scratch_shapes=[pltpu.VMEM((db, s_pad), I32)],
        compiler_params=_cparams(("arbitrary",)),
        name="dec_select",
    )(scores2d, triu)


def _dec_attn_kernel(n_pages, pt_ref, q_ref, mask_ref, bias_ref, knew_ref, vnew_ref, ck_ref, cv_ref,
                     y_ref, kbuf_ref, vbuf_ref, ksem_ref, vsem_ref):
    b = pl.program_id(0)
    slot = _paged_fetch(pt_ref, [(ck_ref, kbuf_ref, ksem_ref), (cv_ref, vbuf_ref, vsem_ref)], n_pages)
    past = n_pages * PAGE_SIZE
    q = q_ref[...]
    hrow = lax.broadcasted_iota(I32, (ATT_HEADS, KV_WIDTH), 0)
    lane = lax.broadcasted_iota(I32, (ATT_HEADS, KV_WIDTH), 1)
    own = (hrow // (ATT_HEADS // KV_HEADS)) == (lane // ATT_HEAD_DIM)
    qbd = jnp.where(own, jnp.concatenate([q] * KV_HEADS, axis=1), jnp.zeros_like(q[:, :1]))
    knew = knew_ref[pl.ds(b, 1), :].astype(BF16).astype(F32)
    l_new = jnp.sum(qbd.astype(F32) * knew, axis=1, keepdims=True)
    vnew = vnew_ref[pl.ds(b, 1), :].astype(BF16).astype(F32)
    mask = mask_ref[0]
    lg = jnp.concatenate([_dot(qbd, kbuf_ref[slot, p].astype(BF16)) for p in range(n_pages)], axis=1)
    lg = lg + bias_ref[:, 0:past] + mask[:, 0:past]
    tl = lax.broadcasted_iota(I32, (ATT_HEADS, LANES), 1)
    lg_new = jnp.where(tl == 0, l_new, 0.0) + bias_ref[:, past:past + LANES] + mask[:, past:past + LANES]
    m = jnp.maximum(jnp.max(lg, axis=1, keepdims=True), jnp.max(lg_new, axis=1, keepdims=True))
    p = jnp.exp(lg - m)
    p_new = jnp.exp(lg_new - m)
    denom = jnp.sum(p, axis=1, keepdims=True) + jnp.sum(p_new, axis=1, keepdims=True)
    pn = p_new[:, 0:1].astype(BF16).astype(F32)
    pb = p.astype(BF16)
    out = pn * vnew
    for pg in range(n_pages):
        out = out + _dot_nt(pb[:, pg * PAGE_SIZE:(pg + 1) * PAGE_SIZE], vbuf_ref[slot, pg].astype(BF16))
    out = out / denom
    r16 = lax.broadcasted_iota(I32, (ATT_HEADS, ATT_HEAD_DIM), 0) // (ATT_HEADS // KV_HEADS)
    y = jnp.zeros((ATT_HEADS, ATT_HEAD_DIM), F32)
    for g in range(KV_HEADS):
        y = jnp.where(r16 == g, out[:, g * ATT_HEAD_DIM:(g + 1) * ATT_HEAD_DIM], y)
    y_ref[...] = y


def _dec_attn(page_table, q16, mask3d, bias_dec, kn_new, v_new, cache_k3d, cache_v3d):
    db, n_pages = page_table.shape
    s_pad = (n_pages + 1) * PAGE_SIZE
    grid_spec = pltpu.PrefetchScalarGridSpec(
        num_scalar_prefetch=1,
        grid=(db,),
        in_specs=[pl.BlockSpec((ATT_HEADS, ATT_HEAD_DIM), lambda b, pt: (b, 0)),
                  pl.BlockSpec((1, 1, s_pad), lambda b, pt: (b, 0, 0)),
                  pl.BlockSpec(bias_dec.shape, lambda b, pt: (0, 0)),
                  pl.BlockSpec(kn_new.shape, lambda b, pt: (0, 0)),
                  pl.BlockSpec(v_new.shape, lambda b, pt: (0, 0)),
                  pl.BlockSpec(memory_space=pl.ANY),
                  pl.BlockSpec(memory_space=pl.ANY)],
        out_specs=pl.BlockSpec((ATT_HEADS, ATT_HEAD_DIM), lambda b, pt: (b, 0)),
        scratch_shapes=[pltpu.VMEM((2, n_pages, KV_WIDTH, PAGE_SIZE), F32),
                        pltpu.VMEM((2, n_pages, KV_WIDTH, PAGE_SIZE), F32),
                        pltpu.SemaphoreType.DMA((2,)),
                        pltpu.SemaphoreType.DMA((2,))],
    )
    return pl.pallas_call(
        functools.partial(_dec_attn_kernel, n_pages),
        grid_spec=grid_spec,
        out_shape=jax.ShapeDtypeStruct((db * ATT_HEADS, ATT_HEAD_DIM), F32),
        compiler_params=_cparams(("arbitrary",)),
        name="dec_attn",
    )(page_table, q16, mask3d, bias_dec, kn_new, v_new, cache_k3d, cache_v3d)


def _channel_mixer(hb, x1, peer_wq, peer_keys, ub, vtb, tn):
    sel = _peer_select(hb, peer_wq, peer_keys, tn)
    return _peer_mix(hb, ub, vtb, sel, x1, tn)


def kernel(x_prompt, x_sample, cache_k, cache_v, cache_idx_k, state_ssm, state_conv, page_table, rel_bias, norm1_g, w_in, conv_w, conv_b, dt_bias, a_log, d_skip, ssd_norm_g, q_norm_g, k_norm_g, w_branch_a, w_branch_b, w_out, norm2_g, peer_wq, peer_keys, peer_u, peer_v):
    depth = w_in.shape[0]
    bsz, t, _ = x_prompt.shape
    db, ds, _ = x_sample.shape
    assert depth == 1 and ds == 1, "single layer, one new token per sequence"
    n = bsz * t
    n_pages = page_table.shape[1]
    past = n_pages * PAGE_SIZE
    l = 0

    w_perm = _permute_w_in(w_in[l])
    wa, wb, wo = w_branch_a[l].astype(BF16), w_branch_b[l].astype(BF16), w_out[l].astype(BF16)
    wq, keys = peer_wq[l].astype(BF16), peer_keys[l].astype(BF16)
    ub = peer_u[l].astype(BF16)
    vtb = jnp.transpose(peer_v[l].reshape(PEER_EXPERTS // PEER_SUB, PEER_SUB, D_MODEL), (0, 2, 1)).astype(BF16)
    ssd_w = (conv_w[l], conv_b[l], dt_bias[l], a_log[l], d_skip[l], ssd_norm_g[l])

    dec_dist = np.concatenate([past - np.arange(past), np.zeros(PAGE_SIZE, np.int64)])
    n_prompt_b = 3 * Q_BLOCK * Q_BLOCK
    buckets = np.concatenate([_prompt_bias_buckets().reshape(-1), _t5_bucket_np(dec_dist)])
    pad = (-buckets.shape[0]) % BIAS_TJ
    bias_all = _bias_lookup(rel_bias, np.concatenate([buckets, np.zeros(pad, np.int32)]))
    btile = bias_all[:, :n_prompt_b].reshape(ATT_HEADS, 3, Q_BLOCK, Q_BLOCK)
    bias_dec = bias_all[:, n_prompt_b:n_prompt_b + past + PAGE_SIZE]

    xp = x_prompt.reshape(n, D_MODEL)
    proj = _in_proj(xp, norm1_g[l], w_perm, min(1024, n))
    y_ssd, h_t = _ssd_prompt(proj, bsz, *ssd_w)
    prep = _attn_prep(proj, q_norm_g[l], k_norm_g[l], min(512, n))
    y_att = _attn_prompt(prep, _attn_select(prep, bsz), btile, bsz)
    x1, hb = _merge(y_ssd, y_att, proj, xp, wa, wb, wo, norm2_g[l], min(512, n))
    y_prompt = _channel_mixer(hb, x1, wq, keys, ub, vtb, min(512, n))

    k_prompt = prep[1].reshape(1, bsz, t, KV_HEADS, ATT_HEAD_DIM)
    v_prompt = prep[2].reshape(1, bsz, t, KV_HEADS, ATT_HEAD_DIM)
    ik_prompt = prep[3].reshape(1, bsz, t, IDX_DIM)
    ssm_prompt = h_t.reshape(1, bsz, SSD_HEADS, SSD_HEAD_DIM, D_STATE)
    conv_prompt = proj.reshape(bsz, t, PROJ_COLS)[:, t - (CONV_W - 1):, :CONV_DIM][None]

    xs_ = x_sample.reshape(db, D_MODEL)
    proj_s = _in_proj(xs_, norm1_g[l], w_perm, db)
    y_ssd_s, h_s, conv_s = _ssd_step(proj_s, state_conv[l].reshape(db, (CONV_W - 1) * CONV_DIM),
                                     state_ssm[l].reshape(db, N_PAIRS, LANES, D_STATE), *ssd_w)
    prep_s = _attn_prep(proj_s, q_norm_g[l], k_norm_g[l], db)
    qn_s, kn_s, v_s, ik_s, _, _, _, _, _, _, iw_s = prep_s
    iq8 = proj_s[:, COL_IQ:COL_IQ + IDX_HEADS * IDX_DIM].reshape(db * IDX_HEADS, IDX_DIM)
    w8 = iw_s[:, SMALL_IW:SMALL_IW + IDX_HEADS].reshape(db * IDX_HEADS, 1)
    ikt_pages = jnp.transpose(cache_idx_k[l], (0, 2, 1))
    kt_pages = jnp.transpose(cache_k[l], (0, 2, 3, 1)).reshape(-1, KV_WIDTH, PAGE_SIZE)
    vt_pages = jnp.transpose(cache_v[l], (0, 2, 3, 1)).reshape(-1, KV_WIDTH, PAGE_SIZE)
    scores = _dec_scores(page_table, ikt_pages, iq8, w8, ik_s)
    n_sel = min(TOPK_KEYS_MAX, (past + ds) // 4)
    mask = _dec_select(scores.reshape(db, past + PAGE_SIZE), n_sel)
    y_att_s = _dec_attn(page_table, qn_s.reshape(db * ATT_HEADS, ATT_HEAD_DIM),
                        mask.reshape(db, 1, past + PAGE_SIZE), bias_dec, kn_s, v_s,
                        kt_pages, vt_pages)
    x1_s, hb_s = _merge(y_ssd_s, y_att_s.reshape(db, ATT_WIDTH), proj_s, xs_, wa, wb, wo, norm2_g[l], db)
    y_sample = _channel_mixer(hb_s, x1_s, wq, keys, ub, vtb, db)

    return (y_prompt.reshape(bsz, t, D_MODEL), y_sample.reshape(db, ds, D_MODEL),
            k_prompt, v_prompt, ik_prompt, ssm_prompt, conv_prompt,
            kn_s.reshape(1, db, ds, KV_HEADS, ATT_HEAD_DIM), v_s.reshape(1, db, ds, KV_HEADS, ATT_HEAD_DIM),
            ik_s.reshape(1, db, ds, IDX_DIM),
            h_s.reshape(1, db, SSD_HEADS, SSD_HEAD_DIM, D_STATE),
            conv_s.reshape(1, db, CONV_W - 1, CONV_DIM))
```

```python
import functools
import math

import numpy as np
import jax
import jax.numpy as jnp
from jax import lax
from jax.experimental import pallas as pl
from jax.experimental.pallas import tpu as pltpu

F32 = jnp.float32
BF16 = jnp.bfloat16
I32 = jnp.int32

D_MODEL = 1024
PAGE_SIZE = 128
D_INNER = 2048
SSD_HEAD_DIM = 64
SSD_HEADS = 32
SSD_GROUPS = 8
D_STATE = 128
CONV_W = 4
CONV_DIM = D_INNER + 2 * SSD_GROUPS * D_STATE
SSD_CHUNK = 128
ATT_HEADS = 16
ATT_HEAD_DIM = 64
ATT_WIDTH = ATT_HEADS * ATT_HEAD_DIM
KV_HEADS = 4
KV_WIDTH = KV_HEADS * ATT_HEAD_DIM
IDX_HEADS = 8
IDX_DIM = 64
TOPK_KEYS_MAX = 256
Q_BLOCK = 128
NUM_BUCKETS = 32
MAX_DISTANCE = 128
PEER_HEADS = 8
PEER_N_KEYS = 128
PEER_EXPERTS = PEER_N_KEYS * PEER_N_KEYS
PEER_HALF = 128
PEER_TOPK = 16
EPS = 1e-6

LANES = 128
SUBLANES = 8
VMEM_LIMIT_BYTES = 56 * 1024 * 1024

COL_XBC = 0
COL_Z = COL_XBC + CONV_DIM
COL_Q = COL_Z + D_INNER
COL_K = COL_Q + ATT_WIDTH
COL_V = COL_K + KV_WIDTH
COL_IQ = COL_V + KV_WIDTH
COL_GA = COL_IQ + IDX_HEADS * IDX_DIM
COL_GB = COL_GA + D_MODEL
COL_SMALL = COL_GB + D_MODEL
SMALL_DT = 0
SMALL_IK = SSD_HEADS
SMALL_IW = SSD_HEADS + IDX_DIM
PROJ_COLS = COL_SMALL + LANES
PROJ_TN = 1152

INT_MIN = -2 ** 31
NEG_INF = float("-inf")


def _cparams(sem):
    return pltpu.CompilerParams(dimension_semantics=sem, vmem_limit_bytes=VMEM_LIMIT_BYTES)


def _split3(x):
    hi = x.astype(BF16)
    r1 = x - hi.astype(F32)
    mid = r1.astype(BF16)
    lo = (r1 - mid.astype(F32)).astype(BF16)
    return hi, mid, lo


def _dot(a, b):
    return jnp.dot(a, b, preferred_element_type=F32)


def _dot_nt(a, b):
    return lax.dot_general(a, b, (((1,), (1,)), ((), ())), preferred_element_type=F32)


def _dot_tn(a, b):
    return lax.dot_general(a, b, (((0,), (0,)), ((), ())), preferred_element_type=F32)


def _exact_dot(x, onehot_bf16):
    hi, mid, lo = _split3(x)
    return _dot(hi, onehot_bf16) + _dot(mid, onehot_bf16) + _dot(lo, onehot_bf16)


def _in_proj_kernel(x_ref, g_ref, w_ref, o_ref, xn_ref):
    @pl.when(pl.program_id(1) == 0)
    def _():
        x = x_ref[...]
        ms = jnp.mean(x * x, axis=-1, keepdims=True)
        xn_ref[...] = (x * lax.rsqrt(ms + EPS) * g_ref[...]).astype(BF16)

    o_ref[...] = _dot(xn_ref[...], w_ref[...])


def _in_proj(x2d, g, w_perm, tm):
    n = x2d.shape[0]
    return pl.pallas_call(
        _in_proj_kernel,
        grid=(n // tm, PROJ_COLS // PROJ_TN),
        in_specs=[pl.BlockSpec((tm, D_MODEL), lambda i, j: (i, 0)),
                  pl.BlockSpec((1, D_MODEL), lambda i, j: (0, 0)),
                  pl.BlockSpec((D_MODEL, PROJ_TN), lambda i, j: (0, j))],
        out_specs=pl.BlockSpec((tm, PROJ_TN), lambda i, j: (i, j)),
        out_shape=jax.ShapeDtypeStruct((n, PROJ_COLS), F32),
        scratch_shapes=[pltpu.VMEM((tm, D_MODEL), BF16)],
        compiler_params=_cparams(("parallel", "arbitrary")),
        name="in_proj",
    )(x2d, g.reshape(1, D_MODEL), w_perm)


def _permute_w_in(w_in):
    offs = np.cumsum([0, D_INNER, CONV_DIM, SSD_HEADS, ATT_WIDTH, KV_WIDTH, KV_WIDTH,
                      IDX_HEADS * IDX_DIM, IDX_DIM, IDX_HEADS, D_MODEL, D_MODEL])
    z, xbc, dt, q, k, v, iq, ik, iw, ga, gb = [w_in[:, offs[i]:offs[i + 1]] for i in range(11)]
    pad = jnp.zeros((D_MODEL, LANES - SSD_HEADS - IDX_DIM - IDX_HEADS), w_in.dtype)
    return jnp.concatenate([xbc, z, q, k, v, iq, ga, gb, dt, ik, iw, pad], axis=1).astype(BF16)


def _seg_indicator(width, seg):
    m = np.zeros((width, LANES), np.float32)
    m[np.arange(width), np.arange(width) // seg] = 1.0
    return m


def _half_placement():
    lo = np.zeros((KV_WIDTH, KV_HEADS * LANES), np.float32)
    hi = np.zeros((KV_WIDTH, KV_HEADS * LANES), np.float32)
    c = np.arange(KV_WIDTH)
    g, d = c // ATT_HEAD_DIM, c % ATT_HEAD_DIM
    lo[c, g * LANES + d] = 1.0
    hi[c, g * LANES + ATT_HEAD_DIM + d] = 1.0
    return lo, hi


def _head_rms(x, ind, ind_t, gain):
    sq = x * x
    hi = sq.astype(BF16)
    lo = (sq - hi.astype(F32)).astype(BF16)
    ss = _dot(hi, ind) + _dot(lo, ind)
    r = lax.rsqrt(ss * (1.0 / ATT_HEAD_DIM) + EPS)
    rb = _exact_dot(r, ind_t)
    return x * rb * gain


def _attn_prep_kernel(q_ref, kv_ref, sm_ref, gq_ref, gk_ref, indq_ref, indqt_ref, indk_ref, indkt_ref,
                      plo_ref, phi_ref,
                      qn_ref, kn_ref, v_ref, ik_ref, klo_ref, khi_ref, vlo_ref, vhi_ref,
                      ikb_ref, iqb_ref, iw_ref):
    qn = _head_rms(q_ref[...], indq_ref[...], indqt_ref[...], gq_ref[...])
    qn_ref[...] = (qn * (ATT_HEAD_DIM ** -0.5)).astype(BF16)
    kv = kv_ref[...]
    kn = _head_rms(kv[:, :KV_WIDTH], indk_ref[...], indkt_ref[...], gk_ref[...])
    v = kv[:, KV_WIDTH:]
    kn_ref[...] = kn
    v_ref[...] = v
    knb = kn.astype(BF16)
    vb = v.astype(BF16)
    for src, place_ref, dst_ref in ((knb, plo_ref, klo_ref), (knb, phi_ref, khi_ref),
                                    (vb, plo_ref, vlo_ref), (vb, phi_ref, vhi_ref)):
        placed = _dot(src, place_ref[...]).astype(BF16)
        for g in range(KV_HEADS):
            dst_ref[g] = placed[:, g * LANES:(g + 1) * LANES]
    sm = sm_ref[...]
    ik = sm[:, SMALL_IK:SMALL_IK + IDX_DIM]
    ik_ref[...] = ik
    ikb_ref[...] = ik.astype(BF16)
    iw_ref[...] = sm * (IDX_HEADS ** -0.5 * IDX_DIM ** -0.5)


def _attn_prep(proj, q_norm_g, k_norm_g, tm):
    n = proj.shape[0]
    gq = jnp.tile(q_norm_g, ATT_HEADS).reshape(1, ATT_WIDTH)
    gk = jnp.tile(k_norm_g, KV_HEADS).reshape(1, KV_WIDTH)
    indq = _seg_indicator(ATT_WIDTH, ATT_HEAD_DIM)
    indk = _seg_indicator(KV_WIDTH, ATT_HEAD_DIM)
    plo, phi = _half_placement()
    consts = [jnp.asarray(a, BF16) for a in (indq, indq.T, indk, indk.T, plo, phi)]

    def full(a):
        return pl.BlockSpec(a.shape, lambda i: (0,) * a.ndim)

    outs = [
        ((n, ATT_WIDTH), BF16), ((n, KV_WIDTH), F32), ((n, KV_WIDTH), F32), ((n, IDX_DIM), F32),
        ((KV_HEADS, n, LANES), BF16), ((KV_HEADS, n, LANES), BF16),
        ((KV_HEADS, n, LANES), BF16), ((KV_HEADS, n, LANES), BF16),
        ((n, IDX_DIM), BF16), ((n, IDX_HEADS * IDX_DIM), BF16), ((n, LANES), F32),
    ]

    def body(q_ref, kv_ref, sm_ref, iq_ref, *rest):
        (gq_ref, gk_ref, indq_ref, indqt_ref, indk_ref, indkt_ref, plo_ref, phi_ref,
         qn_ref, kn_ref, v_ref, ik_ref, klo_ref, khi_ref, vlo_ref, vhi_ref, ikb_ref, iqb_ref, iw_ref) = rest
        _attn_prep_kernel(q_ref, kv_ref, sm_ref, gq_ref, gk_ref, indq_ref, indqt_ref, indk_ref, indkt_ref,
                          plo_ref, phi_ref, qn_ref, kn_ref, v_ref, ik_ref, klo_ref, khi_ref, vlo_ref, vhi_ref,
                          ikb_ref, iqb_ref, iw_ref)
        iqb_ref[...] = iq_ref[...].astype(BF16)

    return pl.pallas_call(
        body,
        grid=(n // tm,),
        in_specs=[pl.BlockSpec((tm, ATT_WIDTH), lambda i: (i, COL_Q // ATT_WIDTH)),
                  pl.BlockSpec((tm, 2 * KV_WIDTH), lambda i: (i, COL_K // (2 * KV_WIDTH))),
                  pl.BlockSpec((tm, LANES), lambda i: (i, COL_SMALL // LANES)),
                  pl.BlockSpec((tm, IDX_HEADS * IDX_DIM), lambda i: (i, COL_IQ // (IDX_HEADS * IDX_DIM))),
                  full(gq), full(gk)] + [full(c) for c in consts],
        out_specs=[pl.BlockSpec((tm, s[1]), lambda i: (i, 0)) if len(s) == 2
                   else pl.BlockSpec((KV_HEADS, tm, LANES), lambda i: (0, i, 0)) for s, _ in outs],
        out_shape=[jax.ShapeDtypeStruct(s, d) for s, d in outs],
        compiler_params=_cparams(("parallel",)),
        name="attn_prep",
    )(proj, proj, proj, proj, gq, gk, *consts)


def _merge_kernel(ya_ref, yb_ref, ga_ref, gb_ref, x_ref, wa_ref, wb_ref, wo_ref, g2_ref, x1_ref, h_ref):
    a = _dot(ya_ref[...].astype(BF16), wa_ref[...])
    b = _dot(yb_ref[...].astype(BF16), wb_ref[...])
    merged = jax.nn.sigmoid(ga_ref[...]) * a + jax.nn.sigmoid(gb_ref[...]) * b
    x1 = x_ref[...] + _dot(merged.astype(BF16), wo_ref[...])
    x1_ref[...] = x1
    ms = jnp.mean(x1 * x1, axis=-1, keepdims=True)
    h_ref[...] = (x1 * lax.rsqrt(ms + EPS) * g2_ref[...]).astype(BF16)


def _merge(y_ssd, y_att, proj, x2d, wa, wb, wo, g2, tm):
    n = x2d.shape[0]

    def full(a):
        return pl.BlockSpec(a.shape, lambda i: (0,) * a.ndim)

    g2 = g2.reshape(1, D_MODEL)
    return pl.pallas_call(
        _merge_kernel,
        grid=(n // tm,),
        in_specs=[pl.BlockSpec((tm, D_INNER), lambda i: (i, 0)),
                  pl.BlockSpec((tm, ATT_WIDTH), lambda i: (i, 0)),
                  pl.BlockSpec((tm, D_MODEL), lambda i: (i, COL_GA // D_MODEL)),
                  pl.BlockSpec((tm, D_MODEL), lambda i: (i, COL_GB // D_MODEL)),
                  pl.BlockSpec((tm, D_MODEL), lambda i: (i, 0)),
                  full(wa), full(wb), full(wo), full(g2)],
        out_specs=[pl.BlockSpec((tm, D_MODEL), lambda i: (i, 0)),
                   pl.BlockSpec((tm, D_MODEL), lambda i: (i, 0))],
        out_shape=[jax.ShapeDtypeStruct((n, D_MODEL), F32), jax.ShapeDtypeStruct((n, D_MODEL), BF16)],
        compiler_params=_cparams(("parallel",)),
        name="merge",
    )(y_ssd, y_att, proj, proj, x2d, wa, wb, wo, g2)


def _sorting_network(n_pow2, n):
    pairs = []
    p = 1
    while p < n_pow2:
        k = p
        while k >= 1:
            for j in range(k % p, n_pow2 - k, 2 * k):
                for i in range(min(k, n_pow2 - j - k)):
                    if (i + j) // (2 * p) == (i + j + k) // (2 * p):
                        pairs.append((i + j, i + j + k))
            k //= 2
        p *= 2
    return [(a, b) for a, b in pairs if b < n]


def _top_values(x, k):
    m = x.shape[0] // SUBLANES
    cols = [x[i * SUBLANES:(i + 1) * SUBLANES] for i in range(m)]
    for a, b in _sorting_network(1 << (m - 1).bit_length(), m):
        cols[a], cols[b] = jnp.maximum(cols[a], cols[b]), jnp.minimum(cols[a], cols[b])
    vals = []
    for r in range(k):
        top = jnp.max(cols[0], axis=0, keepdims=True)
        vals.append(top)
        if r == k - 1:
            break
        popped = cols[0] >= top
        for i in range(min(m, k - 1 - r)):
            below = cols[i + 1] if i + 1 < m else NEG_INF
            cols[i] = jnp.where(popped, below, cols[i])
    return vals


PEER_RANKS = PEER_TOPK + 1


def _peer_select_kernel(h_ref, wq_ref, keys_ref, th_ref, c1_ref, s2_ref, e2_ref):
    q = _dot(h_ref[...], wq_ref[...]).astype(BF16)
    t = q.shape[0]
    riota = lax.broadcasted_iota(I32, (SUBLANES, t), 0)
    for hd in range(PEER_HEADS):
        base = hd * 2 * PEER_HALF
        s1 = _dot_nt(keys_ref[0, hd], q[:, base:base + PEER_HALF])
        s2 = _dot_nt(keys_ref[1, hd], q[:, base + PEER_HALF:base + 2 * PEER_HALF])
        v1 = _top_values(s1, PEER_RANKS)
        v2 = _top_values(s2, PEER_RANKS)
        pad = jnp.full((SUBLANES - 1, t), NEG_INF, F32)
        v2a = jnp.concatenate(v2 + [pad], axis=0)
        blocks = [v1[0] + v2a, v1[1] + v2a[:SUBLANES]]
        for i in range(2, SUBLANES):
            blocks.append(jnp.where(riota < PEER_RANKS // (i + 1), v1[i] + v2a[:SUBLANES], NEG_INF))
        blocks.append(jnp.concatenate(v1[SUBLANES:] + [pad], axis=0) + v2[0])
        cand = jnp.concatenate(blocks, axis=0)
        top = _top_values(cand, PEER_RANKS)
        zsum = sum(jnp.exp(tv - top[0]) for tv in top[:PEER_TOPK])
        th = 0.5 * (top[PEER_TOPK - 1] + top[PEER_TOPK]) - s1
        c1 = jnp.exp(s1 - v1[0]) / zsum
        cw = th_ref.shape[-1]
        for ch in range(th_ref.shape[1]):
            th_ref[hd, ch] = th[:, ch * cw:(ch + 1) * cw]
            c1_ref[hd, ch] = c1[:, ch * cw:(ch + 1) * cw]
        s2_ref[hd] = s2
        e2_ref[hd] = jnp.exp(s2 - v2[0])


def _peer_select(hb, wq, keys, tn):
    n = hb.shape[0]
    big = jax.ShapeDtypeStruct((PEER_HEADS, PEER_N_KEYS, n), F32)
    bspec = pl.BlockSpec((PEER_HEADS, PEER_N_KEYS, tn), lambda i: (0, 0, i))
    cw = min(LANES, tn)
    chunked = jax.ShapeDtypeStruct((PEER_HEADS, n // cw, PEER_N_KEYS, cw), F32)
    cspec = pl.BlockSpec((PEER_HEADS, tn // cw, PEER_N_KEYS, cw), lambda i: (0, i, 0, 0))
    return pl.pallas_call(
        _peer_select_kernel,
        grid=(n // tn,),
        in_specs=[pl.BlockSpec((tn, D_MODEL), lambda i: (i, 0)),
                  pl.BlockSpec(wq.shape, lambda i: (0, 0)),
                  pl.BlockSpec(keys.shape, lambda i: (0, 0, 0, 0))],
        out_specs=[cspec, cspec, bspec, bspec],
        out_shape=[chunked, chunked, big, big],
        compiler_params=_cparams(("parallel",)),
        name="peer_select",
    )(hb, wq, keys)


def _gelu_tanh(x):
    c = math.sqrt(2.0 / math.pi)
    half = 0.5 * x
    return half + half * jnp.tanh(x * (c + (c * 0.044715) * (x * x)))


PEER_EC = 1024
PEER_SUB = 512


def _peer_mix_kernel(h_ref, u_ref, vt_ref, th_ref, c1_ref, s2_ref, e2_ref, x1_ref, y_ref,
                     acc_ref, act_ref, g_ref, ht_ref):
    j = pl.program_id(1)

    @pl.when(j == 0)
    def _():
        acc_ref[...] = jnp.zeros_like(acc_ref)
        ht_ref[...] = h_ref[...].T

    tn = act_ref.shape[1]
    cw = min(LANES, tn)

    def gate_chunk(sc):
        start = sc * PEER_SUB
        erows = pl.ds(start, PEER_SUB)
        act_ref[...] = _gelu_tanh(_dot(u_ref[erows, :], ht_ref[...]))
        slot = sc
        for ii in range(PEER_SUB // PEER_N_KEYS):
            i1 = j * (PEER_EC // PEER_N_KEYS) + sc * (PEER_SUB // PEER_N_KEYS) + ii
            rows = slice(ii * PEER_N_KEYS, (ii + 1) * PEER_N_KEYS)
            for ch in range(tn // cw):
                cols = slice(ch * cw, (ch + 1) * cw)
                grp = (PEER_N_KEYS // SUBLANES, SUBLANES, cw)
                w = jnp.zeros(grp, F32)
                for hd in range(PEER_HEADS):
                    thb = jnp.broadcast_to(th_ref[hd, ch, pl.ds(i1, 1), :], (SUBLANES, cw))
                    c1b = jnp.broadcast_to(c1_ref[hd, ch, pl.ds(i1, 1), :], (SUBLANES, cw))
                    chosen = s2_ref[hd, :, cols].reshape(grp) >= thb
                    w = w + jnp.where(chosen, e2_ref[hd, :, cols].reshape(grp) * c1b, 0.0)
                g = act_ref[rows, cols].reshape(grp) * w
                g_ref[slot, rows, cols] = g.reshape(PEER_N_KEYS, cw).astype(BF16)

    nsub = PEER_EC // PEER_SUB
    for sc in range(nsub):
        gate_chunk(sc)
    vt = jnp.concatenate([vt_ref[sc] for sc in range(nsub)], axis=1)
    acc_ref[...] += _dot(vt, g_ref[...].reshape(PEER_EC, tn))

    @pl.when(j == pl.num_programs(1) - 1)
    def _():
        y_ref[...] = x1_ref[...] + acc_ref[...].T


def _peer_mix(hb, ub, vtb, sel, x1, tn):
    n = hb.shape[0]
    th, c1, s2, e2 = sel
    bspec = pl.BlockSpec((PEER_HEADS, PEER_N_KEYS, tn), lambda i, j: (0, 0, i))
    cw = min(LANES, tn)
    cspec = pl.BlockSpec((PEER_HEADS, tn // cw, PEER_N_KEYS, cw), lambda i, j: (0, i, 0, 0))
    return pl.pallas_call(
        _peer_mix_kernel,
        grid=(n // tn, PEER_EXPERTS // PEER_EC),
        in_specs=[pl.BlockSpec((tn, D_MODEL), lambda i, j: (i, 0)),
                  pl.BlockSpec((PEER_EC, D_MODEL), lambda i, j: (j, 0)),
                  pl.BlockSpec((PEER_EC // PEER_SUB, D_MODEL, PEER_SUB), lambda i, j: (j, 0, 0)),
                  cspec, cspec, bspec, bspec,
                  pl.BlockSpec((tn, D_MODEL), lambda i, j: (i, 0))],
        out_specs=pl.BlockSpec((tn, D_MODEL), lambda i, j: (i, 0)),
        out_shape=jax.ShapeDtypeStruct((n, D_MODEL), F32),
        scratch_shapes=[pltpu.VMEM((D_MODEL, tn), F32),
                        pltpu.VMEM((PEER_SUB, tn), F32),
                        pltpu.VMEM((PEER_EC // PEER_SUB, PEER_SUB, tn), BF16),
                        pltpu.VMEM((D_MODEL, tn), BF16)],
        compiler_params=_cparams(("parallel", "arbitrary")),
        name="peer_mix",
    )(hb, ub, vtb, th, c1, s2, e2, x1)


HALO = 8
N_PAIRS = SSD_HEADS // 2
GROUP_W = D_INNER // SSD_GROUPS


def _softplus(x):
    return jnp.maximum(x, 0.0) + jnp.log1p(jnp.exp(-jnp.abs(x)))


def _silu(x):
    return x * jax.nn.sigmoid(x)


def _lane_bcast(col):
    return jnp.broadcast_to(col, (col.shape[0], LANES))


def _ssd_gate_norm(y, xs, z, dvec, gain):
    y = (y + dvec * xs) * _silu(z)
    parts = []
    for g in range(SSD_GROUPS):
        yg = y[:, g * GROUP_W:(g + 1) * GROUP_W]
        ms = jnp.mean(yg * yg, axis=-1, keepdims=True)
        parts.append(yg * lax.rsqrt(ms + EPS))
    return jnp.concatenate(parts, axis=1) * gain


def _ssd_prompt_kernel(xbc_ref, z_ref, sm_ref, cw_ref, cb_ref, dtb_ref, alog_ref, dvec_ref, gain_ref, tri_ref,
                       y_ref, hout_ref, ext_ref, h_ref, yacc_ref):
    c = pl.program_id(1)

    @pl.when(c == 0)
    def _():
        ext_ref[0:HALO, :] = jnp.zeros((HALO, CONV_DIM), F32)
        h_ref[...] = jnp.zeros_like(h_ref)

    x = xbc_ref[...]
    ext_ref[HALO:HALO + SSD_CHUNK, :] = x
    conv = cb_ref[...]
    for j in range(CONV_W):
        conv = conv + cw_ref[j:j + 1, :] * ext_ref[pl.ds(HALO - (CONV_W - 1) + j, SSD_CHUNK), :]
    ext_ref[0:HALO, :] = x[SSD_CHUNK - HALO:, :]
    xc = _silu(conv)
    xs = xc[:, :D_INNER]

    lane = lax.broadcasted_iota(I32, (SSD_CHUNK, LANES), 1)
    row = lax.broadcasted_iota(I32, (SSD_CHUNK, LANES), 0)
    dt = jnp.where(lane < SSD_HEADS, _softplus(sm_ref[...] + dtb_ref[...]), 0.0)
    da = dt * (-jnp.exp(alog_ref[...]))
    hi, mid, lo = _split3(da)
    tri = tri_ref[...]
    acum = _dot(tri, hi) + _dot(tri, mid) + _dot(tri, lo)
    acum_t = acum.T
    dt_t = dt.T
    alast = acum[SSD_CHUNK - 1:SSD_CHUNK, :]
    e_in = jnp.exp(acum)
    e_out = jnp.exp(alast - acum) * dt
    e_all = jnp.exp(alast)
    causal = row >= lane
    lo_half = lane < SSD_HEAD_DIM

    for g in range(SSD_GROUPS):
        bg = xc[:, D_INNER + g * D_STATE:D_INNER + (g + 1) * D_STATE].astype(BF16)
        cg = xc[:, D_INNER + SSD_GROUPS * D_STATE + g * D_STATE:
                D_INNER + SSD_GROUPS * D_STATE + (g + 1) * D_STATE].astype(BF16)
        cbm = _dot_nt(cg, bg)
        for pp in range(2):
            pair = 2 * g + pp
            r0 = 2 * pair
            xp = xs[:, r0 * SSD_HEAD_DIM:(r0 + 2) * SSD_HEAD_DIM]
            yp = jnp.zeros((SSD_CHUNK, LANES), F32)
            for hh in range(2):
                r = r0 + hh
                seg = _lane_bcast(acum[:, r:r + 1]) - acum_t[r:r + 1, :]
                lmat = jnp.exp(jnp.where(causal, seg, NEG_INF))
                m = (cbm * lmat * dt_t[r:r + 1, :]).astype(BF16)
                xm = jnp.where(lo_half == (hh == 0), xp, 0.0).astype(BF16)
                yp = yp + _dot(m, xm)
            hp = h_ref[pair]
            scale_in = jnp.where(lo_half, _lane_bcast(e_in[:, r0:r0 + 1]), _lane_bcast(e_in[:, r0 + 1:r0 + 2]))
            yp = yp + _dot_nt(cg, hp.astype(BF16)) * scale_in
            scale_out = jnp.where(lo_half, _lane_bcast(e_out[:, r0:r0 + 1]), _lane_bcast(e_out[:, r0 + 1:r0 + 2]))
            xd = (xp * scale_out).astype(BF16)
            hdec = jnp.where(row < SSD_HEAD_DIM, e_all[:, r0:r0 + 1], e_all[:, r0 + 1:r0 + 2])
            h_ref[pair] = hp * hdec + _dot_tn(xd, bg)
            yacc_ref[:, r0 * SSD_HEAD_DIM:(r0 + 2) * SSD_HEAD_DIM] = yp

    y_ref[...] = _ssd_gate_norm(yacc_ref[...], xs, z_ref[...], dvec_ref[...], gain_ref[...])

    @pl.when(c == pl.num_programs(1) - 1)
    def _():
        hout_ref[0] = h_ref[...]


def _pad_lanes(v):
    return jnp.zeros((1, LANES), F32).at[0, :v.shape[0]].set(v)


def _ssd_prompt(proj, bsz, conv_w, conv_b, dt_bias, a_log, d_skip, ssd_norm_g):
    n = proj.shape[0]
    nc = n // bsz // SSD_CHUNK
    tri = jnp.asarray(np.tril(np.ones((SSD_CHUNK, SSD_CHUNK), np.float32)), BF16)
    dvec = jnp.repeat(d_skip, SSD_HEAD_DIM).reshape(1, D_INNER)
    small = [conv_w, conv_b.reshape(1, CONV_DIM), _pad_lanes(dt_bias), _pad_lanes(a_log), dvec,
             ssd_norm_g.reshape(1, D_INNER), tri]

    def full(a):
        return pl.BlockSpec(a.shape, lambda b, c: (0,) * a.ndim)

    return pl.pallas_call(
        _ssd_prompt_kernel,
        grid=(bsz, nc),
        in_specs=[pl.BlockSpec((SSD_CHUNK, CONV_DIM), lambda b, c: (b * nc + c, 0)),
                  pl.BlockSpec((SSD_CHUNK, D_INNER), lambda b, c: (b * nc + c, COL_Z // D_INNER)),
                  pl.BlockSpec((SSD_CHUNK, LANES), lambda b, c: (b * nc + c, COL_SMALL // LANES))]
                 + [full(a) for a in small],
        out_specs=[pl.BlockSpec((SSD_CHUNK, D_INNER), lambda b, c: (b * nc + c, 0)),
                   pl.BlockSpec((1, N_PAIRS, LANES, D_STATE), lambda b, c: (b, 0, 0, 0))],
        out_shape=[jax.ShapeDtypeStruct((n, D_INNER), F32),
                   jax.ShapeDtypeStruct((bsz, N_PAIRS, LANES, D_STATE), F32)],
        scratch_shapes=[pltpu.VMEM((HALO + SSD_CHUNK, CONV_DIM), F32),
                        pltpu.VMEM((N_PAIRS, LANES, D_STATE), F32),
                        pltpu.VMEM((SSD_CHUNK, D_INNER), F32)],
        compiler_params=_cparams(("parallel", "arbitrary")),
        name="ssd_prompt",
    )(proj, proj, proj, *small)


def _t5_bucket_np(dist):
    n = np.maximum(dist, 0)
    max_exact = NUM_BUCKETS // 2
    nf = np.maximum(n, 1).astype(np.float32)
    ratio = np.log(nf / np.float32(max_exact)) / np.float32(math.log(MAX_DISTANCE / max_exact))
    large = max_exact + (ratio * np.float32(NUM_BUCKETS - max_exact)).astype(np.int32)
    large = np.minimum(large, NUM_BUCKETS - 1)
    return np.where(n < max_exact, n, large).astype(np.int32)


BIAS_TJ = 2048


def _bias_kernel(rel_t_ref, bucket_ref, o_ref):
    ids = lax.broadcasted_iota(I32, (NUM_BUCKETS, BIAS_TJ), 0)
    onehot = jnp.where(ids == bucket_ref[...], 1.0, 0.0).astype(BF16)
    o_ref[...] = _exact_dot(rel_t_ref[...], onehot)


def _bias_lookup(rel_bias, buckets):
    j = buckets.shape[0]
    return pl.pallas_call(
        _bias_kernel,
        grid=(j // BIAS_TJ,),
        in_specs=[pl.BlockSpec((ATT_HEADS, NUM_BUCKETS), lambda i: (0, 0)),
                  pl.BlockSpec((1, BIAS_TJ), lambda i: (0, i))],
        out_specs=pl.BlockSpec((ATT_HEADS, BIAS_TJ), lambda i: (0, i)),
        out_shape=jax.ShapeDtypeStruct((ATT_HEADS, j), F32),
        compiler_params=_cparams(("parallel",)),
        name="bias_lookup",
    )(rel_bias.T, jnp.asarray(buckets.reshape(1, j)))


def _prompt_bias_buckets():
    i = np.arange(Q_BLOCK)[:, None]
    j = np.arange(Q_BLOCK)[None, :]
    far = np.full((Q_BLOCK, Q_BLOCK), 2 * Q_BLOCK)
    return _t5_bucket_np(np.stack([far, i - j + Q_BLOCK, i - j]))


KEY_TILES = 4
KEY_CHUNK = KEY_TILES * Q_BLOCK


def _fold_tiles(x, op):
    out = x[:, 0:LANES]
    for u in range(1, x.shape[1] // LANES):
        out = op(out, x[:, u * LANES:(u + 1) * LANES])
    return out


def _sortable_key(x):
    b = pltpu.bitcast(x, I32)
    return jnp.where(b < 0, b ^ 0x7FFFFFFF, b)


def _kth_largest_key(count_ge, k, shape):
    def body(it, cand):
        trial = cand + jnp.left_shift(jnp.int32(1), 31 - it)
        return jnp.where(count_ge(trial) >= k, trial, cand)
    return lax.fori_loop(0, 32, body, jnp.full(shape, INT_MIN, I32))


def _chunks_before(qb):
    a, r = qb // KEY_TILES, qb % KEY_TILES
    return KEY_TILES * (a * (a + 1) // 2) + r * (a + 1)


def _attn_select_kernel(n_sel, nqb, iq_ref, iw_ref, ik_ref, triu_ref, mask_ref, key_ref, cand_ref):
    nch = mask_ref.shape[1]
    row = lax.broadcasted_iota(I32, (Q_BLOCK, KEY_CHUNK), 0)
    lane = lax.broadcasted_iota(I32, (Q_BLOCK, KEY_CHUNK), 1)

    def admissible(qb, c):
        return lane + (c * KEY_CHUNK - qb * Q_BLOCK) <= row

    def score_block(qb, _):
        rows = pl.ds(pl.multiple_of(qb * Q_BLOCK, Q_BLOCK), Q_BLOCK)
        iw = iw_ref[rows, :]
        iq = iq_ref[rows, :]
        wcols = [jnp.broadcast_to(iw[:, SMALL_IW + h:SMALL_IW + h + 1], (Q_BLOCK, KEY_CHUNK))
                 for h in range(IDX_HEADS)]
        iqs = [iq[:, h * IDX_DIM:(h + 1) * IDX_DIM] for h in range(IDX_HEADS)]
        base = _chunks_before(qb)

        def score_chunk(c, _):
            ikc = ik_ref[pl.ds(pl.multiple_of(c * KEY_CHUNK, KEY_CHUNK), KEY_CHUNK), :]
            sc = jnp.zeros((Q_BLOCK, KEY_CHUNK), F32)
            for h in range(IDX_HEADS):
                sc = sc + wcols[h] * jnp.maximum(_dot_nt(iqs[h], ikc), 0.0)
            key_ref[base + c] = _sortable_key(jnp.where(admissible(qb, c), sc, NEG_INF))
            return 0

        lax.fori_loop(0, qb // KEY_TILES + 1, score_chunk, 0)
        cand_ref[qb] = jnp.full((Q_BLOCK, LANES), INT_MIN, I32)
        return 0

    lax.fori_loop(0, nqb, score_block, 0)

    def search_bit(it, _):
        bit = jnp.left_shift(jnp.int32(1), 31 - it)
        for qb in range(nqb):
            trial = cand_ref[qb] + bit
            acc = jnp.zeros((Q_BLOCK, LANES), I32)
            for c in range(qb // KEY_TILES + 1):
                k = key_ref[_chunks_before(qb) + c]
                for u in range(KEY_TILES):
                    acc = acc + jnp.where(k[:, u * LANES:(u + 1) * LANES] >= trial, 1, 0)
            enough = jnp.sum(acc, axis=1, keepdims=True) >= n_sel
            cand_ref[qb] = jnp.where(enough, trial, cand_ref[qb])
        return 0

    lax.fori_loop(0, 32, search_bit, 0)

    def mask_block(qb, _):
        tau = cand_ref[qb][:, 0:1]
        base = _chunks_before(qb)
        nchunk = qb // KEY_TILES + 1

        def count_gt(c, acc):
            return acc + _fold_tiles(jnp.where(key_ref[base + c] > tau, 1, 0), jnp.add)
        n_gt = jnp.sum(lax.fori_loop(0, nchunk, count_gt, jnp.zeros((Q_BLOCK, LANES), I32)), axis=1, keepdims=True)
        need = (n_sel - n_gt).astype(F32)

        def mask_chunk(c, seen):
            k = key_ref[base + c]
            eq = k == tau
            eqf = jnp.where(eq, 1.0, 0.0)
            pref = _dot(eqf.astype(BF16), triu_ref[...]) + seen
            sel = jnp.logical_or(k > tau, jnp.logical_and(eq, pref <= need))
            mask_ref[qb, c] = jnp.where(jnp.logical_and(sel, admissible(qb, c)), 0.0, NEG_INF).astype(BF16)
            return seen + jnp.sum(eqf, axis=1, keepdims=True)

        lax.fori_loop(0, nchunk, mask_chunk, jnp.zeros((Q_BLOCK, 1), F32))

        def blank_chunk(c, _):
            mask_ref[qb, c] = jnp.full((Q_BLOCK, KEY_CHUNK), NEG_INF, BF16)
            return 0

        lax.fori_loop(nchunk, nch, blank_chunk, 0)
        return 0

    lax.fori_loop(0, nqb, mask_block, 0)


def _attn_select(prep, bsz):
    ikb, iqb, iw = prep[8], prep[9], prep[10]
    n = iqb.shape[0]
    t = n // bsz
    nqb = t // Q_BLOCK
    assert t % KEY_CHUNK == 0
    nch = t // KEY_CHUNK
    n_sel = min(TOPK_KEYS_MAX, t // 4)
    triu = jnp.asarray(np.triu(np.ones((KEY_CHUNK, KEY_CHUNK), np.float32)), BF16)
    n_key_chunks = int(_chunks_before(nqb))
    return pl.pallas_call(
        functools.partial(_attn_select_kernel, n_sel, nqb),
        grid=(bsz,),
        in_specs=[pl.BlockSpec((t, IDX_HEADS * IDX_DIM), lambda b: (b, 0)),
                  pl.BlockSpec((t, LANES), lambda b: (b, 0)),
                  pl.BlockSpec((t, IDX_DIM), lambda b: (b, 0)),
                  pl.BlockSpec(triu.shape, lambda b: (0, 0))],
        out_specs=pl.BlockSpec((nqb, nch, Q_BLOCK, KEY_CHUNK), lambda b: (b, 0, 0, 0)),
        out_shape=jax.ShapeDtypeStruct((bsz * nqb, nch, Q_BLOCK, KEY_CHUNK), BF16),
        scratch_shapes=[pltpu.VMEM((n_key_chunks, Q_BLOCK, KEY_CHUNK), I32),
                        pltpu.VMEM((nqb, Q_BLOCK, LANES), I32)],
        compiler_params=_cparams(("parallel",)),
        name="attn_select",
    )(iqb, iw, ikb, triu)


HEADS_PER_KV = ATT_HEADS // KV_HEADS


def _attn_prompt_kernel(q_ref, mask_ref, klo_ref, khi_ref, vlo_ref, vhi_ref, bt_ref, y_ref, lbuf_ref):
    qb = pl.program_id(1)
    g = pl.program_id(2)
    nchunk = qb // KEY_TILES + 1

    k_refs = (klo_ref, khi_ref)
    v_refs = (vlo_ref, vhi_ref)
    qps = [q_ref[:, pp * LANES:(pp + 1) * LANES] for pp in range(HEADS_PER_KV // 2)]

    def logits_chunk(c, mxs):
        rows = pl.ds(pl.multiple_of(c * KEY_CHUNK, KEY_CHUNK), KEY_CHUNK)
        kinds = [jnp.clip(c * KEY_TILES + u - qb + 2, 0, 2) for u in range(KEY_TILES)]
        mask = mask_ref[0, c].astype(F32)
        new = []
        for hh in range(HEADS_PER_KV):
            bias = jnp.concatenate([bt_ref[g * HEADS_PER_KV + hh, kd] for kd in kinds], axis=1)
            lg = _dot_nt(qps[hh // 2], k_refs[hh % 2][g, rows, :]) + bias + mask
            lbuf_ref[hh, c] = lg
            new.append(jnp.maximum(mxs[hh], _fold_tiles(lg, jnp.maximum)))
        return tuple(new)

    neg = jnp.full((Q_BLOCK, LANES), NEG_INF, F32)
    mxs = lax.fori_loop(0, nchunk, logits_chunk, (neg,) * HEADS_PER_KV)
    mrows = [jnp.max(m, axis=1, keepdims=True) for m in mxs]

    def pv_chunk(c, carry):
        rows = pl.ds(pl.multiple_of(c * KEY_CHUNK, KEY_CHUNK), KEY_CHUNK)
        new = []
        for hh in range(HEADS_PER_KV):
            lsum, acc = carry[hh]
            p = jnp.exp(lbuf_ref[hh, c] - mrows[hh])
            new.append((lsum + _fold_tiles(p, jnp.add), acc + _dot(p.astype(BF16), v_refs[hh % 2][g, rows, :])))
        return tuple(new)

    zero = jnp.zeros((Q_BLOCK, LANES), F32)
    res = lax.fori_loop(0, nchunk, pv_chunk, ((zero, zero),) * HEADS_PER_KV)
    outs = [acc / jnp.sum(lsum, axis=1, keepdims=True) for lsum, acc in res]
    for pp in range(HEADS_PER_KV // 2):
        y_ref[:, pp * LANES:(pp + 1) * LANES] = outs[2 * pp] + outs[2 * pp + 1]


def _attn_prompt(prep, mask, btile, bsz):
    qn, klo, khi, vlo, vhi = prep[0], prep[4], prep[5], prep[6], prep[7]
    n = qn.shape[0]
    t = n // bsz
    nqb = t // Q_BLOCK
    nch = t // KEY_CHUNK
    kvspec = pl.BlockSpec((KV_HEADS, t, LANES), lambda b, q, g: (0, b, 0))
    qw = HEADS_PER_KV * ATT_HEAD_DIM
    return pl.pallas_call(
        _attn_prompt_kernel,
        grid=(bsz, nqb, KV_HEADS),
        in_specs=[pl.BlockSpec((Q_BLOCK, qw), lambda b, q, g: (b * nqb + q, g)),
                  pl.BlockSpec((1, nch, Q_BLOCK, KEY_CHUNK), lambda b, q, g: (b * nqb + q, 0, 0, 0)),
                  kvspec, kvspec, kvspec, kvspec,
                  pl.BlockSpec(btile.shape, lambda b, q, g: (0, 0, 0, 0))],
        out_specs=pl.BlockSpec((Q_BLOCK, qw), lambda b, q, g: (b * nqb + q, g)),
        out_shape=jax.ShapeDtypeStruct((n, ATT_WIDTH), F32),
        scratch_shapes=[pltpu.VMEM((HEADS_PER_KV, nch, Q_BLOCK, KEY_CHUNK), F32)],
        compiler_params=_cparams(("parallel", "parallel", "arbitrary")),
        name="attn_prompt",
    )(qn, mask, klo, khi, vlo, vhi, btile)


def _head_expand_indicator():
    m = np.zeros((LANES, D_INNER), np.float32)
    m[np.arange(D_INNER) // SSD_HEAD_DIM, np.arange(D_INNER)] = 1.0
    return m


def _ssd_step_kernel(xbc_ref, z_ref, sm_ref, sc_ref, hin_ref, cw_ref, cb_ref, dtb_ref, alog_ref, dvec_ref, gain_ref,
                     exp_ref, y_ref, hout_ref, cs_ref, xs_ref, bm_ref, cm_ref, dtx_t_ref, dec_t_ref, y_t_ref):
    b = pl.program_id(0)
    db = xbc_ref.shape[0]

    @pl.when(b == 0)
    def _():
        x = xbc_ref[...]
        sc = sc_ref[...]
        conv = cb_ref[...] + cw_ref[CONV_W - 1:CONV_W, :] * x
        for j in range(CONV_W - 1):
            conv = conv + cw_ref[j:j + 1, :] * sc[:, j * CONV_DIM:(j + 1) * CONV_DIM]
        cs_ref[:, :(CONV_W - 2) * CONV_DIM] = sc[:, CONV_DIM:]
        cs_ref[:, (CONV_W - 2) * CONV_DIM:] = x
        xc = _silu(conv)
        xs = xc[:, :D_INNER]
        lane = lax.broadcasted_iota(I32, (db, LANES), 1)
        dt = jnp.where(lane < SSD_HEADS, _softplus(sm_ref[...] + dtb_ref[...]), 0.0)
        dec = jnp.exp(dt * (-jnp.exp(alog_ref[...])))
        xs_ref[0:db, :] = xs
        bm_ref[0:db, :] = xc[:, D_INNER:D_INNER + SSD_GROUPS * D_STATE]
        cm_ref[0:db, :] = xc[:, D_INNER + SSD_GROUPS * D_STATE:]
        dtx_t_ref[:, 0:db] = (_exact_dot(dt, exp_ref[...]) * xs).T
        dec_t_ref[:, 0:db] = _exact_dot(dec, exp_ref[...]).T
        y_t_ref[...] = jnp.zeros_like(y_t_ref)

    lane = lax.broadcasted_iota(I32, (LANES, LANES), 1)
    mine = lane == b
    base = pl.multiple_of((b // 8) * 8, 8)
    my_row = lax.broadcasted_iota(I32, (8, SSD_GROUPS * D_STATE), 0) == (b % 8)
    bm_b = jnp.sum(jnp.where(my_row, bm_ref[pl.ds(base, 8), :], 0.0), axis=0, keepdims=True)
    cm_b = jnp.sum(jnp.where(my_row, cm_ref[pl.ds(base, 8), :], 0.0), axis=0, keepdims=True)
    for pair in range(N_PAIRS):
        g = pair // 2
        rows = slice(pair * LANES, (pair + 1) * LANES)
        xcol = jnp.sum(jnp.where(mine, dtx_t_ref[rows, :], 0.0), axis=1, keepdims=True)
        dcol = jnp.sum(jnp.where(mine, dec_t_ref[rows, :], 0.0), axis=1, keepdims=True)
        brow = bm_b[:, g * D_STATE:(g + 1) * D_STATE]
        crow = cm_b[:, g * D_STATE:(g + 1) * D_STATE]
        hnew = hin_ref[0, pair] * dcol + xcol * brow
        hout_ref[0, pair] = hnew
        ycol = jnp.sum(hnew * crow, axis=1, keepdims=True)
        y_t_ref[rows, :] = jnp.where(mine, ycol, y_t_ref[rows, :])

    @pl.when(b == pl.num_programs(0) - 1)
    def _():
        y = y_t_ref[...].T[0:db, :]
        y_ref[...] = _ssd_gate_norm(y, xs_ref[0:db, :], z_ref[...], dvec_ref[...], gain_ref[...])


def _ssd_step(proj, state_conv2d, state_ssm4d, conv_w, conv_b, dt_bias, a_log, d_skip, ssd_norm_g):
    db = proj.shape[0]
    assert db <= LANES and db % 8 == 0
    dvec = jnp.repeat(d_skip, SSD_HEAD_DIM).reshape(1, D_INNER)
    small = [conv_w, conv_b.reshape(1, CONV_DIM), _pad_lanes(dt_bias), _pad_lanes(a_log), dvec,
             ssd_norm_g.reshape(1, D_INNER), jnp.asarray(_head_expand_indicator(), BF16)]

    def full(a):
        return pl.BlockSpec(a.shape, lambda b: (0,) * a.ndim)

    cs_w = (CONV_W - 1) * CONV_DIM
    hspec = pl.BlockSpec((1, N_PAIRS, LANES, D_STATE), lambda b: (b, 0, 0, 0))
    return pl.pallas_call(
        _ssd_step_kernel,
        grid=(db,),
        in_specs=[pl.BlockSpec((db, CONV_DIM), lambda b: (0, 0)),
                  pl.BlockSpec((db, D_INNER), lambda b: (0, COL_Z // D_INNER)),
                  pl.BlockSpec((db, LANES), lambda b: (0, COL_SMALL // LANES)),
                  pl.BlockSpec((db, cs_w), lambda b: (0, 0)),
                  hspec] + [full(a) for a in small],
        out_specs=[pl.BlockSpec((db, D_INNER), lambda b: (0, 0)),
                   hspec,
                   pl.BlockSpec((db, cs_w), lambda b: (0, 0))],
        out_shape=[jax.ShapeDtypeStruct((db, D_INNER), F32),
                   jax.ShapeDtypeStruct(state_ssm4d.shape, F32),
                   jax.ShapeDtypeStruct((db, cs_w), F32)],
        scratch_shapes=[pltpu.VMEM((LANES, D_INNER), F32),
                        pltpu.VMEM((LANES, SSD_GROUPS * D_STATE), F32),
                        pltpu.VMEM((LANES, SSD_GROUPS * D_STATE), F32),
                        pltpu.VMEM((D_INNER, LANES), F32),
                        pltpu.VMEM((D_INNER, LANES), F32),
                        pltpu.VMEM((D_INNER, LANES), F32)],
        compiler_params=_cparams(("arbitrary",)),
        name="ssd_step",
    )(proj, proj, proj, state_conv2d, state_ssm4d, *small)


def _page_copies(pt_ref, seq, cache_ref, buf_ref, slot, sem_ref, n_pages):
    return [pltpu.make_async_copy(cache_ref.at[pt_ref[seq, p]], buf_ref.at[slot, p], sem_ref.at[slot])
            for p in range(n_pages)]


def _paged_fetch(pt_ref, streams, n_pages):
    b = pl.program_id(0)
    slot = b % 2

    @pl.when(b == 0)
    def _():
        for cache_ref, buf_ref, sem_ref in streams:
            for cp in _page_copies(pt_ref, 0, cache_ref, buf_ref, 0, sem_ref, n_pages):
                cp.start()

    @pl.when(b + 1 < pl.num_programs(0))
    def _():
        for cache_ref, buf_ref, sem_ref in streams:
            for cp in _page_copies(pt_ref, b + 1, cache_ref, buf_ref, 1 - slot, sem_ref, n_pages):
                cp.start()

    for cache_ref, buf_ref, sem_ref in streams:
        for cp in _page_copies(pt_ref, b, cache_ref, buf_ref, slot, sem_ref, n_pages):
            cp.wait()
    return slot


def _dec_score_kernel(n_pages, pt_ref, iq_ref, w_ref, iknew_ref, cache_ref, o_ref, buf_ref, sem_ref):
    b = pl.program_id(0)
    slot = _paged_fetch(pt_ref, [(cache_ref, buf_ref, sem_ref)], n_pages)
    past = n_pages * PAGE_SIZE
    iq = iq_ref[...].astype(BF16)
    w = w_ref[...]
    for p in range(n_pages):
        s = _dot(iq, buf_ref[slot, p].astype(BF16))
        o_ref[0, :, p * PAGE_SIZE:(p + 1) * PAGE_SIZE] = jnp.sum(w * jnp.maximum(s, 0.0), axis=0, keepdims=True)
    iknew = iknew_ref[pl.ds(b, 1), :].astype(BF16).astype(F32)
    s_new = jnp.sum(iq.astype(F32) * iknew, axis=1, keepdims=True)
    sc_new = jnp.sum(w * jnp.maximum(s_new, 0.0), axis=0, keepdims=True)
    lane = lax.broadcasted_iota(I32, (1, LANES), 1)
    o_ref[0, :, past:past + LANES] = jnp.where(lane == 0, sc_new, NEG_INF)


def _dec_scores(page_table, cache_ik, iq8, w8, ikb_new):
    db, n_pages = page_table.shape
    s_pad = (n_pages + 1) * PAGE_SIZE
    grid_spec = pltpu.PrefetchScalarGridSpec(
        num_scalar_prefetch=1,
        grid=(db,),
        in_specs=[pl.BlockSpec((IDX_HEADS, IDX_DIM), lambda b, pt: (b, 0)),
                  pl.BlockSpec((IDX_HEADS, 1), lambda b, pt: (b, 0)),
                  pl.BlockSpec(ikb_new.shape, lambda b, pt: (0, 0)),
                  pl.BlockSpec(memory_space=pl.ANY)],
        out_specs=pl.BlockSpec((1, 1, s_pad), lambda b, pt: (b, 0, 0)),
        scratch_shapes=[pltpu.VMEM((2, n_pages, IDX_DIM, PAGE_SIZE), F32),
                        pltpu.SemaphoreType.DMA((2,))],
    )
    return pl.pallas_call(
        functools.partial(_dec_score_kernel, n_pages),
        grid_spec=grid_spec,
        out_shape=jax.ShapeDtypeStruct((db, 1, s_pad), F32),
        compiler_params=_cparams(("arbitrary",)),
        name="dec_scores",
    )(page_table, iq8, w8, ikb_new, cache_ik)


def _dec_select_kernel(n_sel, sc_ref, triu_ref, mask_ref, key_ref):
    db, s_pad = sc_ref.shape
    ntile = s_pad // LANES
    key_ref[...] = _sortable_key(sc_ref[...])

    def count_ge(trial):
        acc = jnp.zeros((db, LANES), I32)
        for j in range(ntile):
            acc = acc + jnp.where(key_ref[:, j * LANES:(j + 1) * LANES] >= trial, 1, 0)
        return jnp.sum(acc, axis=1, keepdims=True)

    tau = _kth_largest_key(count_ge, n_sel, (db, 1))
    acc = jnp.zeros((db, LANES), I32)
    for j in range(ntile):
        acc = acc + jnp.where(key_ref[:, j * LANES:(j + 1) * LANES] > tau, 1, 0)
    need = (n_sel - jnp.sum(acc, axis=1, keepdims=True)).astype(F32)
    seen = jnp.zeros((db, 1), F32)
    for j in range(ntile):
        k = key_ref[:, j * LANES:(j + 1) * LANES]
        eq = k == tau
        pref = _dot(jnp.where(eq, 1.0, 0.0).astype(BF16), triu_ref[...]) + seen
        sel = jnp.logical_or(k > tau, jnp.logical_and(eq, pref <= need))
        mask_ref[:, j * LANES:(j + 1) * LANES] = jnp.where(sel, 0.0, NEG_INF)
        seen = pref[:, LANES - 1:LANES]


def _dec_select(scores2d, n_sel):
    db, s_pad = scores2d.shape
    triu = jnp.asarray(np.triu(np.ones((LANES, LANES), np.float32)), BF16)
    return pl.pallas_call(
        functools.partial(_dec_select_kernel, n_sel),
        grid=(1,),
        in_specs=[pl.BlockSpec((db, s_pad), lambda i: (0, 0)),
                  pl.BlockSpec(triu.shape, lambda i: (0, 0))],
        out_specs=pl.BlockSpec((db, s_pad), lambda i: (0, 0)),
        out_shape=jax.ShapeDtypeStruct((db, s_pad), F32),
        scratch_shapes=[pltpu.VMEM((db, s_pad), I32)],
        compiler_params=_cparams(("arbitrary",)),
        name="dec_select",
    )(scores2d, triu)


def _dec_attn_kernel(n_pages, pt_ref, q_ref, mask_ref, bias_ref, knew_ref, vnew_ref, ck_ref, cv_ref,
                     y_ref, kbuf_ref, vbuf_ref, ksem_ref, vsem_ref):
    b = pl.program_id(0)
    slot = _paged_fetch(pt_ref, [(ck_ref, kbuf_ref, ksem_ref), (cv_ref, vbuf_ref, vsem_ref)], n_pages)
    past = n_pages * PAGE_SIZE
    q = q_ref[...]
    hrow = lax.broadcasted_iota(I32, (ATT_HEADS, KV_WIDTH), 0)
    lane = lax.broadcasted_iota(I32, (ATT_HEADS, KV_WIDTH), 1)
    own = (hrow // (ATT_HEADS // KV_HEADS)) == (lane // ATT_HEAD_DIM)
    qbd = jnp.where(own, jnp.concatenate([q] * KV_HEADS, axis=1), jnp.zeros_like(q[:, :1]))
    knew = knew_ref[pl.ds(b, 1), :].astype(BF16).astype(F32)
    l_new = jnp.sum(qbd.astype(F32) * knew, axis=1, keepdims=True)
    vnew = vnew_ref[pl.ds(b, 1), :].astype(BF16).astype(F32)
    mask = mask_ref[0]
    lg = jnp.concatenate([_dot(qbd, kbuf_ref[slot, p].astype(BF16)) for p in range(n_pages)], axis=1)
    lg = lg + bias_ref[:, 0:past] + mask[:, 0:past]
    tl = lax.broadcasted_iota(I32, (ATT_HEADS, LANES), 1)
    lg_new = jnp.where(tl == 0, l_new, 0.0) + bias_ref[:, past:past + LANES] + mask[:, past:past + LANES]
    m = jnp.maximum(jnp.max(lg, axis=1, keepdims=True), jnp.max(lg_new, axis=1, keepdims=True))
    p = jnp.exp(lg - m)
    p_new = jnp.exp(lg_new - m)
    denom = jnp.sum(p, axis=1, keepdims=True) + jnp.sum(p_new, axis=1, keepdims=True)
    pn = p_new[:, 0:1].astype(BF16).astype(F32)
    pb = p.astype(BF16)
    out = pn * vnew
    for pg in range(n_pages):
        out = out + _dot_nt(pb[:, pg * PAGE_SIZE:(pg + 1) * PAGE_SIZE], vbuf_ref[slot, pg].astype(BF16))
    out = out / denom
    r16 = lax.broadcasted_iota(I32, (ATT_HEADS, ATT_HEAD_DIM), 0) // (ATT_HEADS // KV_HEADS)
    y = jnp.zeros((ATT_HEADS, ATT_HEAD_DIM), F32)
    for g in range(KV_HEADS):
        y = jnp.where(r16 == g, out[:, g * ATT_HEAD_DIM:(g + 1) * ATT_HEAD_DIM], y)
    y_ref[...] = y


def _dec_attn(page_table, q16, mask3d, bias_dec, kn_new, v_new, cache_k3d, cache_v3d):
    db, n_pages = page_table.shape
    s_pad = (n_pages + 1) * PAGE_SIZE
    grid_spec = pltpu.PrefetchScalarGridSpec(
        num_scalar_prefetch=1,
        grid=(db,),
        in_specs=[pl.BlockSpec((ATT_HEADS, ATT_HEAD_DIM), lambda b, pt: (b, 0)),
                  pl.BlockSpec((1, 1, s_pad), lambda b, pt: (b, 0, 0)),
                  pl.BlockSpec(bias_dec.shape, lambda b, pt: (0, 0)),
                  pl.BlockSpec(kn_new.shape, lambda b, pt: (0, 0)),
                  pl.BlockSpec(v_new.shape, lambda b, pt: (0, 0)),
                  pl.BlockSpec(memory_space=pl.ANY),
                  pl.BlockSpec(memory_space=pl.ANY)],
        out_specs=pl.BlockSpec((ATT_HEADS, ATT_HEAD_DIM), lambda b, pt: (b, 0)),
        scratch_shapes=[pltpu.VMEM((2, n_pages, KV_WIDTH, PAGE_SIZE), F32),
                        pltpu.VMEM((2, n_pages, KV_WIDTH, PAGE_SIZE), F32),
                        pltpu.SemaphoreType.DMA((2,)),
                        pltpu.SemaphoreType.DMA((2,))],
    )
    return pl.pallas_call(
        functools.partial(_dec_attn_kernel, n_pages),
        grid_spec=grid_spec,
        out_shape=jax.ShapeDtypeStruct((db * ATT_HEADS, ATT_HEAD_DIM), F32),
        compiler_params=_cparams(("arbitrary",)),
        name="dec_attn",
    )(page_table, q16, mask3d, bias_dec, kn_new, v_new, cache_k3d, cache_v3d)


def _channel_mixer(hb, x1, peer_wq, peer_keys, ub, vtb, tn):
    sel = _peer_select(hb, peer_wq, peer_keys, tn)
    return _peer_mix(hb, ub, vtb, sel, x1, tn)


def kernel(x_prompt, x_sample, cache_k, cache_v, cache_idx_k, state_ssm, state_conv, page_table, rel_bias, norm1_g, w_in, conv_w, conv_b, dt_bias, a_log, d_skip, ssd_norm_g, q_norm_g, k_norm_g, w_branch_a, w_branch_b, w_out, norm2_g, peer_wq, peer_keys, peer_u, peer_v):
    depth = w_in.shape[0]
    bsz, t, _ = x_prompt.shape
    db, ds, _ = x_sample.shape
    assert depth == 1 and ds == 1, "single layer, one new token per sequence"
    n = bsz * t
    n_pages = page_table.shape[1]
    past = n_pages * PAGE_SIZE
    l = 0

    w_perm = _permute_w_in(w_in[l])
    wa, wb, wo = w_branch_a[l].astype(BF16), w_branch_b[l].astype(BF16), w_out[l].astype(BF16)
    wq, keys = peer_wq[l].astype(BF16), peer_keys[l].astype(BF16)
    ub = peer_u[l].astype(BF16)
    vtb = jnp.transpose(peer_v[l].reshape(PEER_EXPERTS // PEER_SUB, PEER_SUB, D_MODEL), (0, 2, 1)).astype(BF16)
    ssd_w = (conv_w[l], conv_b[l], dt_bias[l], a_log[l], d_skip[l], ssd_norm_g[l])

    dec_dist = np.concatenate([past - np.arange(past), np.zeros(PAGE_SIZE, np.int64)])
    n_prompt_b = 3 * Q_BLOCK * Q_BLOCK
    buckets = np.concatenate([_prompt_bias_buckets().reshape(-1), _t5_bucket_np(dec_dist)])
    pad = (-buckets.shape[0]) % BIAS_TJ
    bias_all = _bias_lookup(rel_bias, np.concatenate([buckets, np.zeros(pad, np.int32)]))
    btile = bias_all[:, :n_prompt_b].reshape(ATT_HEADS, 3, Q_BLOCK, Q_BLOCK)
    bias_dec = bias_all[:, n_prompt_b:n_prompt_b + past + PAGE_SIZE]

    xp = x_prompt.reshape(n, D_MODEL)
    proj = _in_proj(xp, norm1_g[l], w_perm, min(1024, n))
    y_ssd, h_t = _ssd_prompt(proj, bsz, *ssd_w)
    prep = _attn_prep(proj, q_norm_g[l], k_norm_g[l], min(512, n))
    y_att = _attn_prompt(prep, _attn_select(prep, bsz), btile, bsz)
    x1, hb = _merge(y_ssd, y_att, proj, xp, wa, wb, wo, norm2_g[l], min(512, n))
    y_prompt = _channel_mixer(hb, x1, wq, keys, ub, vtb, min(512, n))

    k_prompt = prep[1].reshape(1, bsz, t, KV_HEADS, ATT_HEAD_DIM)
    v_prompt = prep[2].reshape(1, bsz, t, KV_HEADS, ATT_HEAD_DIM)
    ik_prompt = prep[3].reshape(1, bsz, t, IDX_DIM)
    ssm_prompt = h_t.reshape(1, bsz, SSD_HEADS, SSD_HEAD_DIM, D_STATE)
    conv_prompt = proj.reshape(bsz, t, PROJ_COLS)[:, t - (CONV_W - 1):, :CONV_DIM][None]

    xs_ = x_sample.reshape(db, D_MODEL)
    proj_s = _in_proj(xs_, norm1_g[l], w_perm, db)
    y_ssd_s, h_s, conv_s = _ssd_step(proj_s, state_conv[l].reshape(db, (CONV_W - 1) * CONV_DIM),
                                     state_ssm[l].reshape(db, N_PAIRS, LANES, D_STATE), *ssd_w)
    prep_s = _attn_prep(proj_s, q_norm_g[l], k_norm_g[l], db)
    qn_s, kn_s, v_s, ik_s, _, _, _, _, _, _, iw_s = prep_s
    iq8 = proj_s[:, COL_IQ:COL_IQ + IDX_HEADS * IDX_DIM].reshape(db * IDX_HEADS, IDX_DIM)
    w8 = iw_s[:, SMALL_IW:SMALL_IW + IDX_HEADS].reshape(db * IDX_HEADS, 1)
    ikt_pages = jnp.transpose(cache_idx_k[l], (0, 2, 1))
    kt_pages = jnp.transpose(cache_k[l], (0, 2, 3, 1)).reshape(-1, KV_WIDTH, PAGE_SIZE)
    vt_pages = jnp.transpose(cache_v[l], (0, 2, 3, 1)).reshape(-1, KV_WIDTH, PAGE_SIZE)
    scores = _dec_scores(page_table, ikt_pages, iq8, w8, ik_s)
    n_sel = min(TOPK_KEYS_MAX, (past + ds) // 4)
    mask = _dec_select(scores.reshape(db, past + PAGE_SIZE), n_sel)
    y_att_s = _dec_attn(page_table, qn_s.reshape(db * ATT_HEADS, ATT_HEAD_DIM),
                        mask.reshape(db, 1, past + PAGE_SIZE), bias_dec, kn_s, v_s,
                        kt_pages, vt_pages)
    x1_s, hb_s = _merge(y_ssd_s, y_att_s.reshape(db, ATT_WIDTH), proj_s, xs_, wa, wb, wo, norm2_g[l], db)
    y_sample = _channel_mixer(hb_s, x1_s, wq, keys, ub, vtb, db)

    return (y_prompt.reshape(bsz, t, D_MODEL), y_sample.reshape(db, ds, D_MODEL),
            k_prompt, v_prompt, ik_prompt, ssm_prompt, conv_prompt,
            kn_s.reshape(1, db, ds, KV_HEADS, ATT_HEAD_DIM), v_s.reshape(1, db, ds, KV_HEADS, ATT_HEAD_DIM),
            ik_s.reshape(1, db, ds, IDX_DIM),
            h_s.reshape(1, db, SSD_HEADS, SSD_HEAD_DIM, D_STATE),
            conv_s.reshape(1, db, CONV_W - 1, CONV_DIM))
```

```python
import functools
import math

import numpy as np
import jax
import jax.numpy as jnp
from jax import lax
from jax.experimental import pallas as pl
from jax.experimental.pallas import tpu as pltpu

F32 = jnp.float32
BF16 = jnp.bfloat16
I32 = jnp.int32

D_MODEL = 1024
PAGE_SIZE = 128
D_INNER = 2048
SSD_HEAD_DIM = 64
SSD_HEADS = 32
SSD_GROUPS = 8
D_STATE = 128
CONV_W = 4
CONV_DIM = D_INNER + 2 * SSD_GROUPS * D_STATE
SSD_CHUNK = 128
ATT_HEADS = 16
ATT_HEAD_DIM = 64
ATT_WIDTH = ATT_HEADS * ATT_HEAD_DIM
KV_HEADS = 4
KV_WIDTH = KV_HEADS * ATT_HEAD_DIM
IDX_HEADS = 8
IDX_DIM = 64
TOPK_KEYS_MAX = 256
Q_BLOCK = 128
NUM_BUCKETS = 32
MAX_DISTANCE = 128
PEER_HEADS = 8
PEER_N_KEYS = 128
PEER_EXPERTS = PEER_N_KEYS * PEER_N_KEYS
PEER_HALF = 128
PEER_TOPK = 16
EPS = 1e-6

LANES = 128
SUBLANES = 8
VMEM_LIMIT_BYTES = 56 * 1024 * 1024

COL_XBC = 0
COL_Z = COL_XBC + CONV_DIM
COL_Q = COL_Z + D_INNER
COL_K = COL_Q + ATT_WIDTH
COL_V = COL_K + KV_WIDTH
COL_IQ = COL_V + KV_WIDTH
COL_GA = COL_IQ + IDX_HEADS * IDX_DIM
COL_GB = COL_GA + D_MODEL
COL_SMALL = COL_GB + D_MODEL
SMALL_DT = 0
SMALL_IK = SSD_HEADS
SMALL_IW = SSD_HEADS + IDX_DIM
PROJ_COLS = COL_SMALL + LANES
PROJ_TN = 1152

INT_MIN = -2 ** 31
NEG_INF = float("-inf")


def _cparams(sem):
    return pltpu.CompilerParams(dimension_semantics=sem, vmem_limit_bytes=VMEM_LIMIT_BYTES)


def _split3(x):
    hi = x.astype(BF16)
    r1 = x - hi.astype(F32)
    mid = r1.astype(BF16)
    lo = (r1 - mid.astype(F32)).astype(BF16)
    return hi, mid, lo


def _dot(a, b):
    return jnp.dot(a, b, preferred_element_type=F32)


def _dot_nt(a, b):
    return lax.dot_general(a, b, (((1,), (1,)), ((), ())), preferred_element_type=F32)


def _dot_tn(a, b):
    return lax.dot_general(a, b, (((0,), (0,)), ((), ())), preferred_element_type=F32)


def _exact_dot(x, onehot_bf16):
    hi, mid, lo = _split3(x)
    return _dot(hi, onehot_bf16) + _dot(mid, onehot_bf16) + _dot(lo, onehot_bf16)


def _in_proj_kernel(x_ref, g_ref, w_ref, o_ref, xn_ref):
    @pl.when(pl.program_id(1) == 0)
    def _():
        x = x_ref[...]
        ms = jnp.mean(x * x, axis=-1, keepdims=True)
        xn_ref[...] = (x * lax.rsqrt(ms + EPS) * g_ref[...]).astype(BF16)

    o_ref[...] = _dot(xn_ref[...], w_ref[...])


def _in_proj(x2d, g, w_perm, tm):
    n = x2d.shape[0]
    return pl.pallas_call(
        _in_proj_kernel,
        grid=(n // tm, PROJ_COLS // PROJ_TN),
        in_specs=[pl.BlockSpec((tm, D_MODEL), lambda i, j: (i, 0)),
                  pl.BlockSpec((1, D_MODEL), lambda i, j: (0, 0)),
                  pl.BlockSpec((D_MODEL, PROJ_TN), lambda i, j: (0, j))],
        out_specs=pl.BlockSpec((tm, PROJ_TN), lambda i, j: (i, j)),
        out_shape=jax.ShapeDtypeStruct((n, PROJ_COLS), F32),
        scratch_shapes=[pltpu.VMEM((tm, D_MODEL), BF16)],
        compiler_params=_cparams(("parallel", "arbitrary")),
        name="in_proj",
    )(x2d, g.reshape(1, D_MODEL), w_perm)


def _permute_w_in(w_in):
    offs = np.cumsum([0, D_INNER, CONV_DIM, SSD_HEADS, ATT_WIDTH, KV_WIDTH, KV_WIDTH,
                      IDX_HEADS * IDX_DIM, IDX_DIM, IDX_HEADS, D_MODEL, D_MODEL])
    z, xbc, dt, q, k, v, iq, ik, iw, ga, gb = [w_in[:, offs[i]:offs[i + 1]] for i in range(11)]
    pad = jnp.zeros((D_MODEL, LANES - SSD_HEADS - IDX_DIM - IDX_HEADS), w_in.dtype)
    return jnp.concatenate([xbc, z, q, k, v, iq, ga, gb, dt, ik, iw, pad], axis=1).astype(BF16)


def _seg_indicator(width, seg):
    m = np.zeros((width, LANES), np.float32)
    m[np.arange(width), np.arange(width) // seg] = 1.0
    return m


def _half_placement():
    lo = np.zeros((KV_WIDTH, KV_HEADS * LANES), np.float32)
    hi = np.zeros((KV_WIDTH, KV_HEADS * LANES), np.float32)
    c = np.arange(KV_WIDTH)
    g, d = c // ATT_HEAD_DIM, c % ATT_HEAD_DIM
    lo[c, g * LANES + d] = 1.0
    hi[c, g * LANES + ATT_HEAD_DIM + d] = 1.0
    return lo, hi


def _head_rms(x, ind, ind_t, gain):
    sq = x * x
    hi = sq.astype(BF16)
    lo = (sq - hi.astype(F32)).astype(BF16)
    ss = _dot(hi, ind) + _dot(lo, ind)
    r = lax.rsqrt(ss * (1.0 / ATT_HEAD_DIM) + EPS)
    rb = _exact_dot(r, ind_t)
    return x * rb * gain


def _attn_prep_kernel(q_ref, kv_ref, sm_ref, gq_ref, gk_ref, indq_ref, indqt_ref, indk_ref, indkt_ref,
                      plo_ref, phi_ref,
                      qn_ref, kn_ref, v_ref, ik_ref, klo_ref, khi_ref, vlo_ref, vhi_ref,
                      ikb_ref, iqb_ref, iw_ref):
    qn = _head_rms(q_ref[...], indq_ref[...], indqt_ref[...], gq_ref[...])
    qn_ref[...] = (qn * (ATT_HEAD_DIM ** -0.5)).astype(BF16)
    kv = kv_ref[...]
    kn = _head_rms(kv[:, :KV_WIDTH], indk_ref[...], indkt_ref[...], gk_ref[...])
    v = kv[:, KV_WIDTH:]
    kn_ref[...] = kn
    v_ref[...] = v
    knb = kn.astype(BF16)
    vb = v.astype(BF16)
    for src, place_ref, dst_ref in ((knb, plo_ref, klo_ref), (knb, phi_ref, khi_ref),
                                    (vb, plo_ref, vlo_ref), (vb, phi_ref, vhi_ref)):
        placed = _dot(src, place_ref[...]).astype(BF16)
        for g in range(KV_HEADS):
            dst_ref[g] = placed[:, g * LANES:(g + 1) * LANES]
    sm = sm_ref[...]
    ik = sm[:, SMALL_IK:SMALL_IK + IDX_DIM]
    ik_ref[...] = ik
    ikb_ref[...] = ik.astype(BF16)
    iw_ref[...] = sm * (IDX_HEADS ** -0.5 * IDX_DIM ** -0.5)


def _attn_prep(proj, q_norm_g, k_norm_g, tm):
    n = proj.shape[0]
    gq = jnp.tile(q_norm_g, ATT_HEADS).reshape(1, ATT_WIDTH)
    gk = jnp.tile(k_norm_g, KV_HEADS).reshape(1, KV_WIDTH)
    indq = _seg_indicator(ATT_WIDTH, ATT_HEAD_DIM)
    indk = _seg_indicator(KV_WIDTH, ATT_HEAD_DIM)
    plo, phi = _half_placement()
    consts = [jnp.asarray(a, BF16) for a in (indq, indq.T, indk, indk.T, plo, phi)]

    def full(a):
        return pl.BlockSpec(a.shape, lambda i: (0,) * a.ndim)

    outs = [
        ((n, ATT_WIDTH), BF16), ((n, KV_WIDTH), F32), ((n, KV_WIDTH), F32), ((n, IDX_DIM), F32),
        ((KV_HEADS, n, LANES), BF16), ((KV_HEADS, n, LANES), BF16),
        ((KV_HEADS, n, LANES), BF16), ((KV_HEADS, n, LANES), BF16),
        ((n, IDX_DIM), BF16), ((n, IDX_HEADS * IDX_DIM), BF16), ((n, LANES), F32),
    ]

    def body(q_ref, kv_ref, sm_ref, iq_ref, *rest):
        (gq_ref, gk_ref, indq_ref, indqt_ref, indk_ref, indkt_ref, plo_ref, phi_ref,
         qn_ref, kn_ref, v_ref, ik_ref, klo_ref, khi_ref, vlo_ref, vhi_ref, ikb_ref, iqb_ref, iw_ref) = rest
        _attn_prep_kernel(q_ref, kv_ref, sm_ref, gq_ref, gk_ref, indq_ref, indqt_ref, indk_ref, indkt_ref,
                          plo_ref, phi_ref, qn_ref, kn_ref, v_ref, ik_ref, klo_ref, khi_ref, vlo_ref, vhi_ref,
                          ikb_ref, iqb_ref, iw_ref)
        iqb_ref[...] = iq_ref[...].astype(BF16)

    return pl.pallas_call(
        body,
        grid=(n // tm,),
        in_specs=[pl.BlockSpec((tm, ATT_WIDTH), lambda i: (i, COL_Q // ATT_WIDTH)),
                  pl.BlockSpec((tm, 2 * KV_WIDTH), lambda i: (i, COL_K // (2 * KV_WIDTH))),
                  pl.BlockSpec((tm, LANES), lambda i: (i, COL_SMALL // LANES)),
                  pl.BlockSpec((tm, IDX_HEADS * IDX_DIM), lambda i: (i, COL_IQ // (IDX_HEADS * IDX_DIM))),
                  full(gq), full(gk)] + [full(c) for c in consts],
        out_specs=[pl.BlockSpec((tm, s[1]), lambda i: (i, 0)) if len(s) == 2
                   else pl.BlockSpec((KV_HEADS, tm, LANES), lambda i: (0, i, 0)) for s, _ in outs],
        out_shape=[jax.ShapeDtypeStruct(s, d) for s, d in outs],
        compiler_params=_cparams(("parallel",)),
        name="attn_prep",
    )(proj, proj, proj, proj, gq, gk, *consts)


def _merge_kernel(ya_ref, yb_ref, ga_ref, gb_ref, x_ref, wa_ref, wb_ref, wo_ref, g2_ref, x1_ref, h_ref):
    a = _dot(ya_ref[...].astype(BF16), wa_ref[...])
    b = _dot(yb_ref[...].astype(BF16), wb_ref[...])
    merged = jax.nn.sigmoid(ga_ref[...]) * a + jax.nn.sigmoid(gb_ref[...]) * b
    x1 = x_ref[...] + _dot(merged.astype(BF16), wo_ref[...])
    x1_ref[...] = x1
    ms = jnp.mean(x1 * x1, axis=-1, keepdims=True)
    h_ref[...] = (x1 * lax.rsqrt(ms + EPS) * g2_ref[...]).astype(BF16)


def _merge(y_ssd, y_att, proj, x2d, wa, wb, wo, g2, tm):
    n = x2d.shape[0]

    def full(a):
        return pl.BlockSpec(a.shape, lambda i: (0,) * a.ndim)

    g2 = g2.reshape(1, D_MODEL)
    return pl.pallas_call(
        _merge_kernel,
        grid=(n // tm,),
        in_specs=[pl.BlockSpec((tm, D_INNER), lambda i: (i, 0)),
                  pl.BlockSpec((tm, ATT_WIDTH), lambda i: (i, 0)),
                  pl.BlockSpec((tm, D_MODEL), lambda i: (i, COL_GA // D_MODEL)),
                  pl.BlockSpec((tm, D_MODEL), lambda i: (i, COL_GB // D_MODEL)),
                  pl.BlockSpec((tm, D_MODEL), lambda i: (i, 0)),
                  full(wa), full(wb), full(wo), full(g2)],
        out_specs=[pl.BlockSpec((tm, D_MODEL), lambda i: (i, 0)),
                   pl.BlockSpec((tm, D_MODEL), lambda i: (i, 0))],
        out_shape=[jax.ShapeDtypeStruct((n, D_MODEL), F32), jax.ShapeDtypeStruct((n, D_MODEL), BF16)],
        compiler_params=_cparams(("parallel",)),
        name="merge",
    )(y_ssd, y_att, proj, proj, x2d, wa, wb, wo, g2)


def _sorting_network(n_pow2, n):
    pairs = []
    p = 1
    while p < n_pow2:
        k = p
        while k >= 1:
            for j in range(k % p, n_pow2 - k, 2 * k):
                for i in range(min(k, n_pow2 - j - k)):
                    if (i + j) // (2 * p) == (i + j + k) // (2 * p):
                        pairs.append((i + j, i + j + k))
            k //= 2
        p *= 2
    return [(a, b) for a, b in pairs if b < n]


def _top_values(x, k):
    m = x.shape[0] // SUBLANES
    cols = [x[i * SUBLANES:(i + 1) * SUBLANES] for i in range(m)]
    for a, b in _sorting_network(1 << (m - 1).bit_length(), m):
        cols[a], cols[b] = jnp.maximum(cols[a], cols[b]), jnp.minimum(cols[a], cols[b])
    vals = []
    for r in range(k):
        top = jnp.max(cols[0], axis=0, keepdims=True)
        vals.append(top)
        if r == k - 1:
            break
        popped = cols[0] >= top
        for i in range(min(m, k - 1 - r)):
            below = cols[i + 1] if i + 1 < m else NEG_INF
            cols[i] = jnp.where(popped, below, cols[i])
    return vals


PEER_RANKS = PEER_TOPK + 1


def _peer_select_kernel(h_ref, wq_ref, keys_ref, eth_ref, c1_ref, e2_ref):
    q = _dot(h_ref[...], wq_ref[...]).astype(BF16)
    t = q.shape[0]
    riota = lax.broadcasted_iota(I32, (SUBLANES, t), 0)
    for hd in range(PEER_HEADS):
        base = hd * 2 * PEER_HALF
        s1 = _dot_nt(keys_ref[0, hd], q[:, base:base + PEER_HALF])
        s2 = _dot_nt(keys_ref[1, hd], q[:, base + PEER_HALF:base + 2 * PEER_HALF])
        v1 = _top_values(s1, PEER_RANKS)
        v2 = _top_values(s2, PEER_RANKS)
        pad = jnp.full((SUBLANES - 1, t), NEG_INF, F32)
        v2a = jnp.concatenate(v2 + [pad], axis=0)
        blocks = [v1[0] + v2a, v1[1] + v2a[:SUBLANES]]
        for i in range(2, SUBLANES):
            blocks.append(jnp.where(riota < PEER_RANKS // (i + 1), v1[i] + v2a[:SUBLANES], NEG_INF))
        blocks.append(jnp.concatenate(v1[SUBLANES:] + [pad], axis=0) + v2[0])
        cand = jnp.concatenate(blocks, axis=0)
        top = _top_values(cand, PEER_RANKS)
        zsum = sum(jnp.exp(tv - top[0]) for tv in top[:PEER_TOPK])
        th = 0.5 * (top[PEER_TOPK - 1] + top[PEER_TOPK]) - s1
        eth = jnp.exp(th - v2[0])
        c1 = jnp.exp(s1 - v1[0]) / zsum
        cw = eth_ref.shape[-1]
        for ch in range(eth_ref.shape[1]):
            eth_ref[hd, ch] = eth[:, ch * cw:(ch + 1) * cw]
            c1_ref[hd, ch] = c1[:, ch * cw:(ch + 1) * cw]
        e2_ref[hd] = jnp.exp(s2 - v2[0])


def _peer_select(hb, wq, keys, tn):
    n = hb.shape[0]
    big = jax.ShapeDtypeStruct((PEER_HEADS, PEER_N_KEYS, n), F32)
    bspec = pl.BlockSpec((PEER_HEADS, PEER_N_KEYS, tn), lambda i: (0, 0, i))
    cw = min(LANES, tn)
    chunked = jax.ShapeDtypeStruct((PEER_HEADS, n // cw, PEER_N_KEYS, cw), F32)
    cspec = pl.BlockSpec((PEER_HEADS, tn // cw, PEER_N_KEYS, cw), lambda i: (0, i, 0, 0))
    return pl.pallas_call(
        _peer_select_kernel,
        grid=(n // tn,),
        in_specs=[pl.BlockSpec((tn, D_MODEL), lambda i: (i, 0)),
                  pl.BlockSpec(wq.shape, lambda i: (0, 0)),
                  pl.BlockSpec(keys.shape, lambda i: (0, 0, 0, 0))],
        out_specs=[cspec, cspec, bspec],
        out_shape=[chunked, chunked, big],
        compiler_params=_cparams(("parallel",)),
        name="peer_select",
    )(hb, wq, keys)


def _gelu_tanh(x):
    c = math.sqrt(2.0 / math.pi)
    half = 0.5 * x
    return half + half * jnp.tanh(x * (c + (c * 0.044715) * (x * x)))


PEER_EC = 1024
PEER_SUB = 512


def _peer_mix_kernel(h_ref, u_ref, vt_ref, eth_ref, c1_ref, e2_ref, x1_ref, y_ref,
                     acc_ref, act_ref, g_ref, ht_ref):
    j = pl.program_id(1)

    @pl.when(j == 0)
    def _():
        acc_ref[...] = jnp.zeros_like(acc_ref)
        ht_ref[...] = h_ref[...].T

    tn = act_ref.shape[1]
    cw = min(LANES, tn)

    def gate_chunk(sc):
        start = sc * PEER_SUB
        erows = pl.ds(start, PEER_SUB)
        act_ref[...] = _gelu_tanh(_dot(u_ref[erows, :], ht_ref[...]))
        slot = sc
        for ii in range(PEER_SUB // PEER_N_KEYS):
            i1 = j * (PEER_EC // PEER_N_KEYS) + sc * (PEER_SUB // PEER_N_KEYS) + ii
            rows = slice(ii * PEER_N_KEYS, (ii + 1) * PEER_N_KEYS)
            for ch in range(tn // cw):
                cols = slice(ch * cw, (ch + 1) * cw)
                grp = (PEER_N_KEYS // SUBLANES, SUBLANES, cw)
                w = jnp.zeros(grp, F32)
                for hd in range(PEER_HEADS):
                    ethb = jnp.broadcast_to(eth_ref[hd, ch, pl.ds(i1, 1), :], (SUBLANES, cw))
                    c1b = jnp.broadcast_to(c1_ref[hd, ch, pl.ds(i1, 1), :], (SUBLANES, cw))
                    e2t = e2_ref[hd, :, cols].reshape(grp)
                    w = w + jnp.where(e2t >= ethb, e2t * c1b, 0.0)
                g = act_ref[rows, cols].reshape(grp) * w
                g_ref[slot, rows, cols] = g.reshape(PEER_N_KEYS, cw).astype(BF16)

    nsub = PEER_EC // PEER_SUB
    for sc in range(nsub):
        gate_chunk(sc)
    vt = jnp.concatenate([vt_ref[sc] for sc in range(nsub)], axis=1)
    acc_ref[...] += _dot(vt, g_ref[...].reshape(PEER_EC, tn))

    @pl.when(j == pl.num_programs(1) - 1)
    def _():
        y_ref[...] = x1_ref[...] + acc_ref[...].T


def _peer_mix(hb, ub, vtb, sel, x1, tn):
    n = hb.shape[0]
    eth, c1, e2 = sel
    bspec = pl.BlockSpec((PEER_HEADS, PEER_N_KEYS, tn), lambda i, j: (0, 0, i))
    cw = min(LANES, tn)
    cspec = pl.BlockSpec((PEER_HEADS, tn // cw, PEER_N_KEYS, cw), lambda i, j: (0, i, 0, 0))
    return pl.pallas_call(
        _peer_mix_kernel,
        grid=(n // tn, PEER_EXPERTS // PEER_EC),
        in_specs=[pl.BlockSpec((tn, D_MODEL), lambda i, j: (i, 0)),
                  pl.BlockSpec((PEER_EC, D_MODEL), lambda i, j: (j, 0)),
                  pl.BlockSpec((PEER_EC // PEER_SUB, D_MODEL, PEER_SUB), lambda i, j: (j, 0, 0)),
                  cspec, cspec, bspec,
                  pl.BlockSpec((tn, D_MODEL), lambda i, j: (i, 0))],
        out_specs=pl.BlockSpec((tn, D_MODEL), lambda i, j: (i, 0)),
        out_shape=jax.ShapeDtypeStruct((n, D_MODEL), F32),
        scratch_shapes=[pltpu.VMEM((D_MODEL, tn), F32),
                        pltpu.VMEM((PEER_SUB, tn), F32),
                        pltpu.VMEM((PEER_EC // PEER_SUB, PEER_SUB, tn), BF16),
                        pltpu.VMEM((D_MODEL, tn), BF16)],
        compiler_params=_cparams(("parallel", "arbitrary")),
        name="peer_mix",
    )(hb, ub, vtb, eth, c1, e2, x1)


HALO = 8
N_PAIRS = SSD_HEADS // 2
GROUP_W = D_INNER // SSD_GROUPS


def _softplus(x):
    return jnp.maximum(x, 0.0) + jnp.log1p(jnp.exp(-jnp.abs(x)))


def _silu(x):
    return x * jax.nn.sigmoid(x)


def _lane_bcast(col):
    return jnp.broadcast_to(col, (col.shape[0], LANES))


def _ssd_gate_norm(y, xs, z, dvec, gain):
    y = (y + dvec * xs) * _silu(z)
    parts = []
    for g in range(SSD_GROUPS):
        yg = y[:, g * GROUP_W:(g + 1) * GROUP_W]
        ms = jnp.mean(yg * yg, axis=-1, keepdims=True)
        parts.append(yg * lax.rsqrt(ms + EPS))
    return jnp.concatenate(parts, axis=1) * gain


def _ssd_prompt_kernel(xbc_ref, z_ref, sm_ref, cw_ref, cb_ref, dtb_ref, alog_ref, dvec_ref, gain_ref, tri_ref,
                       y_ref, hout_ref, ext_ref, h_ref, yacc_ref):
    c = pl.program_id(1)

    @pl.when(c == 0)
    def _():
        ext_ref[0:HALO, :] = jnp.zeros((HALO, CONV_DIM), F32)
        h_ref[...] = jnp.zeros_like(h_ref)

    x = xbc_ref[...]
    ext_ref[HALO:HALO + SSD_CHUNK, :] = x
    conv = cb_ref[...]
    for j in range(CONV_W):
        conv = conv + cw_ref[j:j + 1, :] * ext_ref[pl.ds(HALO - (CONV_W - 1) + j, SSD_CHUNK), :]
    ext_ref[0:HALO, :] = x[SSD_CHUNK - HALO:, :]
    xc = _silu(conv)
    xs = xc[:, :D_INNER]

    lane = lax.broadcasted_iota(I32, (SSD_CHUNK, LANES), 1)
    row = lax.broadcasted_iota(I32, (SSD_CHUNK, LANES), 0)
    dt = jnp.where(lane < SSD_HEADS, _softplus(sm_ref[...] + dtb_ref[...]), 0.0)
    da = dt * (-jnp.exp(alog_ref[...]))
    hi, mid, lo = _split3(da)
    tri = tri_ref[...]
    acum = _dot(tri, hi) + _dot(tri, mid) + _dot(tri, lo)
    acum_t = acum.T
    dt_t = dt.T
    alast = acum[SSD_CHUNK - 1:SSD_CHUNK, :]
    e_in = jnp.exp(acum)
    e_out = jnp.exp(alast - acum) * dt
    e_all = jnp.exp(alast)
    causal = row >= lane
    lo_half = lane < SSD_HEAD_DIM

    for g in range(SSD_GROUPS):
        bg = xc[:, D_INNER + g * D_STATE:D_INNER + (g + 1) * D_STATE].astype(BF16)
        cg = xc[:, D_INNER + SSD_GROUPS * D_STATE + g * D_STATE:
                D_INNER + SSD_GROUPS * D_STATE + (g + 1) * D_STATE].astype(BF16)
        cbm = _dot_nt(cg, bg)
        for pp in range(2):
            pair = 2 * g + pp
            r0 = 2 * pair
            xp = xs[:, r0 * SSD_HEAD_DIM:(r0 + 2) * SSD_HEAD_DIM]
            yp = jnp.zeros((SSD_CHUNK, LANES), F32)
            for hh in range(2):
                r = r0 + hh
                seg = _lane_bcast(acum[:, r:r + 1]) - acum_t[r:r + 1, :]
                lmat = jnp.exp(jnp.where(causal, seg, NEG_INF))
                m = (cbm * lmat * dt_t[r:r + 1, :]).astype(BF16)
                xm = jnp.where(lo_half == (hh == 0), xp, 0.0).astype(BF16)
                yp = yp + _dot(m, xm)
            hp = h_ref[pair]
            scale_in = jnp.where(lo_half, _lane_bcast(e_in[:, r0:r0 + 1]), _lane_bcast(e_in[:, r0 + 1:r0 + 2]))
            yp = yp + _dot_nt(cg, hp.astype(BF16)) * scale_in
            scale_out = jnp.where(lo_half, _lane_bcast(e_out[:, r0:r0 + 1]), _lane_bcast(e_out[:, r0 + 1:r0 + 2]))
            xd = (xp * scale_out).astype(BF16)
            hdec = jnp.where(row < SSD_HEAD_DIM, e_all[:, r0:r0 + 1], e_all[:, r0 + 1:r0 + 2])
            h_ref[pair] = hp * hdec + _dot_tn(xd, bg)
            yacc_ref[:, r0 * SSD_HEAD_DIM:(r0 + 2) * SSD_HEAD_DIM] = yp

    y_ref[...] = _ssd_gate_norm(yacc_ref[...], xs, z_ref[...], dvec_ref[...], gain_ref[...])

    @pl.when(c == pl.num_programs(1) - 1)
    def _():
        hout_ref[0] = h_ref[...]


def _pad_lanes(v):
    return jnp.zeros((1, LANES), F32).at[0, :v.shape[0]].set(v)


def _ssd_prompt(proj, bsz, conv_w, conv_b, dt_bias, a_log, d_skip, ssd_norm_g):
    n = proj.shape[0]
    nc = n // bsz // SSD_CHUNK
    tri = jnp.asarray(np.tril(np.ones((SSD_CHUNK, SSD_CHUNK), np.float32)), BF16)
    dvec = jnp.repeat(d_skip, SSD_HEAD_DIM).reshape(1, D_INNER)
    small = [conv_w, conv_b.reshape(1, CONV_DIM), _pad_lanes(dt_bias), _pad_lanes(a_log), dvec,
             ssd_norm_g.reshape(1, D_INNER), tri]

    def full(a):
        return pl.BlockSpec(a.shape, lambda b, c: (0,) * a.ndim)

    return pl.pallas_call(
        _ssd_prompt_kernel,
        grid=(bsz, nc),
        in_specs=[pl.BlockSpec((SSD_CHUNK, CONV_DIM), lambda b, c: (b * nc + c, 0)),
                  pl.BlockSpec((SSD_CHUNK, D_INNER), lambda b, c: (b * nc + c, COL_Z // D_INNER)),
                  pl.BlockSpec((SSD_CHUNK, LANES), lambda b, c: (b * nc + c, COL_SMALL // LANES))]
                 + [full(a) for a in small],
        out_specs=[pl.BlockSpec((SSD_CHUNK, D_INNER), lambda b, c: (b * nc + c, 0)),
                   pl.BlockSpec((1, N_PAIRS, LANES, D_STATE), lambda b, c: (b, 0, 0, 0))],
        out_shape=[jax.ShapeDtypeStruct((n, D_INNER), F32),
                   jax.ShapeDtypeStruct((bsz, N_PAIRS, LANES, D_STATE), F32)],
        scratch_shapes=[pltpu.VMEM((HALO + SSD_CHUNK, CONV_DIM), F32),
                        pltpu.VMEM((N_PAIRS, LANES, D_STATE), F32),
                        pltpu.VMEM((SSD_CHUNK, D_INNER), F32)],
        compiler_params=_cparams(("parallel", "arbitrary")),
        name="ssd_prompt",
    )(proj, proj, proj, *small)


def _t5_bucket_np(dist):
    n = np.maximum(dist, 0)
    max_exact = NUM_BUCKETS // 2
    nf = np.maximum(n, 1).astype(np.float32)
    ratio = np.log(nf / np.float32(max_exact)) / np.float32(math.log(MAX_DISTANCE / max_exact))
    large = max_exact + (ratio * np.float32(NUM_BUCKETS - max_exact)).astype(np.int32)
    large = np.minimum(large, NUM_BUCKETS - 1)
    return np.where(n < max_exact, n, large).astype(np.int32)


BIAS_TJ = 2048


def _bias_kernel(rel_t_ref, bucket_ref, o_ref):
    ids = lax.broadcasted_iota(I32, (NUM_BUCKETS, BIAS_TJ), 0)
    onehot = jnp.where(ids == bucket_ref[...], 1.0, 0.0).astype(BF16)
    o_ref[...] = _exact_dot(rel_t_ref[...], onehot)


def _bias_lookup(rel_bias, buckets):
    j = buckets.shape[0]
    return pl.pallas_call(
        _bias_kernel,
        grid=(j // BIAS_TJ,),
        in_specs=[pl.BlockSpec((ATT_HEADS, NUM_BUCKETS), lambda i: (0, 0)),
                  pl.BlockSpec((1, BIAS_TJ), lambda i: (0, i))],
        out_specs=pl.BlockSpec((ATT_HEADS, BIAS_TJ), lambda i: (0, i)),
        out_shape=jax.ShapeDtypeStruct((ATT_HEADS, j), F32),
        compiler_params=_cparams(("parallel",)),
        name="bias_lookup",
    )(rel_bias.T, jnp.asarray(buckets.reshape(1, j)))


def _prompt_bias_buckets():
    i = np.arange(Q_BLOCK)[:, None]
    j = np.arange(Q_BLOCK)[None, :]
    far = np.full((Q_BLOCK, Q_BLOCK), 2 * Q_BLOCK)
    return _t5_bucket_np(np.stack([far, i - j + Q_BLOCK, i - j]))


KEY_TILES = 4
KEY_CHUNK = KEY_TILES * Q_BLOCK


def _fold_tiles(x, op):
    out = x[:, 0:LANES]
    for u in range(1, x.shape[1] // LANES):
        out = op(out, x[:, u * LANES:(u + 1) * LANES])
    return out


def _sortable_key(x):
    b = pltpu.bitcast(x, I32)
    return jnp.where(b < 0, b ^ 0x7FFFFFFF, b)


def _kth_largest_key(count_ge, k, shape):
    def body(it, cand):
        trial = cand + jnp.left_shift(jnp.int32(1), 31 - it)
        return jnp.where(count_ge(trial) >= k, trial, cand)
    return lax.fori_loop(0, 32, body, jnp.full(shape, INT_MIN, I32))


def _chunks_before(qb):
    a, r = qb // KEY_TILES, qb % KEY_TILES
    return KEY_TILES * (a * (a + 1) // 2) + r * (a + 1)


def _attn_select_kernel(n_sel, nqb, iq_ref, iw_ref, ik_ref, triu_ref, mask_ref, key_ref, cand_ref):
    nch = mask_ref.shape[1]
    row = lax.broadcasted_iota(I32, (Q_BLOCK, KEY_CHUNK), 0)
    lane = lax.broadcasted_iota(I32, (Q_BLOCK, KEY_CHUNK), 1)

    def admissible(qb, c):
        return lane + (c * KEY_CHUNK - qb * Q_BLOCK) <= row

    def score_block(qb, _):
        rows = pl.ds(pl.multiple_of(qb * Q_BLOCK, Q_BLOCK), Q_BLOCK)
        iw = iw_ref[rows, :]
        iq = iq_ref[rows, :]
        wcols = [jnp.broadcast_to(iw[:, SMALL_IW + h:SMALL_IW + h + 1], (Q_BLOCK, KEY_CHUNK))
                 for h in range(IDX_HEADS)]
        iqs = [iq[:, h * IDX_DIM:(h + 1) * IDX_DIM] for h in range(IDX_HEADS)]
        base = _chunks_before(qb)

        def score_chunk(c, _):
            ikc = ik_ref[pl.ds(pl.multiple_of(c * KEY_CHUNK, KEY_CHUNK), KEY_CHUNK), :]
            sc = jnp.zeros((Q_BLOCK, KEY_CHUNK), F32)
            for h in range(IDX_HEADS):
                sc = sc + wcols[h] * jnp.maximum(_dot_nt(iqs[h], ikc), 0.0)
            key_ref[base + c] = _sortable_key(jnp.where(admissible(qb, c), sc, NEG_INF))
            return 0

        lax.fori_loop(0, qb // KEY_TILES + 1, score_chunk, 0)
        cand_ref[qb] = jnp.full((Q_BLOCK, LANES), INT_MIN, I32)
        return 0

    lax.fori_loop(0, nqb, score_block, 0)

    def search_bit(it, _):
        bit = jnp.left_shift(jnp.int32(1), 31 - it)
        for qb in range(nqb):
            trial = cand_ref[qb] + bit
            acc = jnp.zeros((Q_BLOCK, LANES), I32)
            for c in range(qb // KEY_TILES + 1):
                k = key_ref[_chunks_before(qb) + c]
                for u in range(KEY_TILES):
                    acc = acc + jnp.where(k[:, u * LANES:(u + 1) * LANES] >= trial, 1, 0)
            enough = jnp.sum(acc, axis=1, keepdims=True) >= n_sel
            cand_ref[qb] = jnp.where(enough, trial, cand_ref[qb])
        return 0

    lax.fori_loop(0, 32, search_bit, 0)

    def mask_block(qb, _):
        tau = cand_ref[qb][:, 0:1]
        base = _chunks_before(qb)
        nchunk = qb // KEY_TILES + 1

        def count_gt(c, acc):
            return acc + _fold_tiles(jnp.where(key_ref[base + c] > tau, 1, 0), jnp.add)
        n_gt = jnp.sum(lax.fori_loop(0, nchunk, count_gt, jnp.zeros((Q_BLOCK, LANES), I32)), axis=1, keepdims=True)
        need = (n_sel - n_gt).astype(F32)

        def mask_chunk(c, seen):
            k = key_ref[base + c]
            eq = k == tau
            eqf = jnp.where(eq, 1.0, 0.0)
            pref = _dot(eqf.astype(BF16), triu_ref[...]) + seen
            sel = jnp.logical_or(k > tau, jnp.logical_and(eq, pref <= need))
            mask_ref[qb, c] = jnp.where(jnp.logical_and(sel, admissible(qb, c)), 0.0, NEG_INF).astype(BF16)
            return seen + jnp.sum(eqf, axis=1, keepdims=True)

        lax.fori_loop(0, nchunk, mask_chunk, jnp.zeros((Q_BLOCK, 1), F32))

        def blank_chunk(c, _):
            mask_ref[qb, c] = jnp.full((Q_BLOCK, KEY_CHUNK), NEG_INF, BF16)
            return 0

        lax.fori_loop(nchunk, nch, blank_chunk, 0)
        return 0

    lax.fori_loop(0, nqb, mask_block, 0)


def _attn_select(prep, bsz):
    ikb, iqb, iw = prep[8], prep[9], prep[10]
    n = iqb.shape[0]
    t = n // bsz
    nqb = t // Q_BLOCK
    assert t % KEY_CHUNK == 0
    nch = t // KEY_CHUNK
    n_sel = min(TOPK_KEYS_MAX, t // 4)
    triu = jnp.asarray(np.triu(np.ones((KEY_CHUNK, KEY_CHUNK), np.float32)), BF16)
    n_key_chunks = int(_chunks_before(nqb))
    return pl.pallas_call(
        functools.partial(_attn_select_kernel, n_sel, nqb),
        grid=(bsz,),
        in_specs=[pl.BlockSpec((t, IDX_HEADS * IDX_DIM), lambda b: (b, 0)),
                  pl.BlockSpec((t, LANES), lambda b: (b, 0)),
                  pl.BlockSpec((t, IDX_DIM), lambda b: (b, 0)),
                  pl.BlockSpec(triu.shape, lambda b: (0, 0))],
        out_specs=pl.BlockSpec((nqb, nch, Q_BLOCK, KEY_CHUNK), lambda b: (b, 0, 0, 0)),
        out_shape=jax.ShapeDtypeStruct((bsz * nqb, nch, Q_BLOCK, KEY_CHUNK), BF16),
        scratch_shapes=[pltpu.VMEM((n_key_chunks, Q_BLOCK, KEY_CHUNK), I32),
                        pltpu.VMEM((nqb, Q_BLOCK, LANES), I32)],
        compiler_params=_cparams(("parallel",)),
        name="attn_select",
    )(iqb, iw, ikb, triu)


HEADS_PER_KV = ATT_HEADS // KV_HEADS


def _attn_prompt_kernel(q_ref, mask_ref, klo_ref, khi_ref, vlo_ref, vhi_ref, bt_ref, y_ref, lbuf_ref):
    qb = pl.program_id(1)
    g = pl.program_id(2)
    nchunk = qb // KEY_TILES + 1

    k_refs = (klo_ref, khi_ref)
    v_refs = (vlo_ref, vhi_ref)
    qps = [q_ref[:, pp * LANES:(pp + 1) * LANES] for pp in range(HEADS_PER_KV // 2)]

    def logits_chunk(c, mxs):
        rows = pl.ds(pl.multiple_of(c * KEY_CHUNK, KEY_CHUNK), KEY_CHUNK)
        kinds = [jnp.clip(c * KEY_TILES + u - qb + 2, 0, 2) for u in range(KEY_TILES)]
        mask = mask_ref[0, c].astype(F32)
        new = []
        for hh in range(HEADS_PER_KV):
            bias = jnp.concatenate([bt_ref[g * HEADS_PER_KV + hh, kd] for kd in kinds], axis=1)
            lg = _dot_nt(qps[hh // 2], k_refs[hh % 2][g, rows, :]) + bias + mask
            lbuf_ref[hh, c] = lg
            new.append(jnp.maximum(mxs[hh], _fold_tiles(lg, jnp.maximum)))
        return tuple(new)

    neg = jnp.full((Q_BLOCK, LANES), NEG_INF, F32)
    mxs = lax.fori_loop(0, nchunk, logits_chunk, (neg,) * HEADS_PER_KV)
    mrows = [jnp.max(m, axis=1, keepdims=True) for m in mxs]

    def pv_chunk(c, carry):
        rows = pl.ds(pl.multiple_of(c * KEY_CHUNK, KEY_CHUNK), KEY_CHUNK)
        new = []
        for hh in range(HEADS_PER_KV):
            lsum, acc = carry[hh]
            p = jnp.exp(lbuf_ref[hh, c] - mrows[hh])
            new.append((lsum + _fold_tiles(p, jnp.add), acc + _dot(p.astype(BF16), v_refs[hh % 2][g, rows, :])))
        return tuple(new)

    zero = jnp.zeros((Q_BLOCK, LANES), F32)
    res = lax.fori_loop(0, nchunk, pv_chunk, ((zero, zero),) * HEADS_PER_KV)
    outs = [acc / jnp.sum(lsum, axis=1, keepdims=True) for lsum, acc in res]
    for pp in range(HEADS_PER_KV // 2):
        y_ref[:, pp * LANES:(pp + 1) * LANES] = outs[2 * pp] + outs[2 * pp + 1]


def _attn_prompt(prep, mask, btile, bsz):
    qn, klo, khi, vlo, vhi = prep[0], prep[4], prep[5], prep[6], prep[7]
    n = qn.shape[0]
    t = n // bsz
    nqb = t // Q_BLOCK
    nch = t // KEY_CHUNK
    kvspec = pl.BlockSpec((KV_HEADS, t, LANES), lambda b, q, g: (0, b, 0))
    qw = HEADS_PER_KV * ATT_HEAD_DIM
    return pl.pallas_call(
        _attn_prompt_kernel,
        grid=(bsz, nqb, KV_HEADS),
        in_specs=[pl.BlockSpec((Q_BLOCK, qw), lambda b, q, g: (b * nqb + q, g)),
                  pl.BlockSpec((1, nch, Q_BLOCK, KEY_CHUNK), lambda b, q, g: (b * nqb + q, 0, 0, 0)),
                  kvspec, kvspec, kvspec, kvspec,
                  pl.BlockSpec(btile.shape, lambda b, q, g: (0, 0, 0, 0))],
        out_specs=pl.BlockSpec((Q_BLOCK, qw), lambda b, q, g: (b * nqb + q, g)),
        out_shape=jax.ShapeDtypeStruct((n, ATT_WIDTH), F32),
        scratch_shapes=[pltpu.VMEM((HEADS_PER_KV, nch, Q_BLOCK, KEY_CHUNK), F32)],
        compiler_params=_cparams(("parallel", "parallel", "arbitrary")),
        name="attn_prompt",
    )(qn, mask, klo, khi, vlo, vhi, btile)


def _head_expand_indicator():
    m = np.zeros((LANES, D_INNER), np.float32)
    m[np.arange(D_INNER) // SSD_HEAD_DIM, np.arange(D_INNER)] = 1.0
    return m


def _ssd_step_kernel(xbc_ref, z_ref, sm_ref, sc_ref, hin_ref, cw_ref, cb_ref, dtb_ref, alog_ref, dvec_ref, gain_ref,
                     exp_ref, y_ref, hout_ref, cs_ref, xs_ref, bm_ref, cm_ref, dtx_t_ref, dec_t_ref, y_t_ref):
    b = pl.program_id(0)
    db = xbc_ref.shape[0]

    @pl.when(b == 0)
    def _():
        x = xbc_ref[...]
        sc = sc_ref[...]
        conv = cb_ref[...] + cw_ref[CONV_W - 1:CONV_W, :] * x
        for j in range(CONV_W - 1):
            conv = conv + cw_ref[j:j + 1, :] * sc[:, j * CONV_DIM:(j + 1) * CONV_DIM]
        cs_ref[:, :(CONV_W - 2) * CONV_DIM] = sc[:, CONV_DIM:]
        cs_ref[:, (CONV_W - 2) * CONV_DIM:] = x
        xc = _silu(conv)
        xs = xc[:, :D_INNER]
        lane = lax.broadcasted_iota(I32, (db, LANES), 1)
        dt = jnp.where(lane < SSD_HEADS, _softplus(sm_ref[...] + dtb_ref[...]), 0.0)
        dec = jnp.exp(dt * (-jnp.exp(alog_ref[...])))
        xs_ref[0:db, :] = xs
        bm_ref[0:db, :] = xc[:, D_INNER:D_INNER + SSD_GROUPS * D_STATE]
        cm_ref[0:db, :] = xc[:, D_INNER + SSD_GROUPS * D_STATE:]
        dtx_t_ref[:, 0:db] = (_exact_dot(dt, exp_ref[...]) * xs).T
        dec_t_ref[:, 0:db] = _exact_dot(dec, exp_ref[...]).T
        y_t_ref[...] = jnp.zeros_like(y_t_ref)

    lane = lax.broadcasted_iota(I32, (LANES, LANES), 1)
    mine = lane == b
    base = pl.multiple_of((b // 8) * 8, 8)
    my_row = lax.broadcasted_iota(I32, (8, SSD_GROUPS * D_STATE), 0) == (b % 8)
    bm_b = jnp.sum(jnp.where(my_row, bm_ref[pl.ds(base, 8), :], 0.0), axis=0, keepdims=True)
    cm_b = jnp.sum(jnp.where(my_row, cm_ref[pl.ds(base, 8), :], 0.0), axis=0, keepdims=True)
    for pair in range(N_PAIRS):
        g = pair // 2
        rows = slice(pair * LANES, (pair + 1) * LANES)
        xcol = jnp.sum(jnp.where(mine, dtx_t_ref[rows, :], 0.0), axis=1, keepdims=True)
        dcol = jnp.sum(jnp.where(mine, dec_t_ref[rows, :], 0.0), axis=1, keepdims=True)
        brow = bm_b[:, g * D_STATE:(g + 1) * D_STATE]
        crow = cm_b[:, g * D_STATE:(g + 1) * D_STATE]
        hnew = hin_ref[0, pair] * dcol + xcol * brow
        hout_ref[0, pair] = hnew
        ycol = jnp.sum(hnew * crow, axis=1, keepdims=True)
        y_t_ref[rows, :] = jnp.where(mine, ycol, y_t_ref[rows, :])

    @pl.when(b == pl.num_programs(0) - 1)
    def _():
        y = y_t_ref[...].T[0:db, :]
        y_ref[...] = _ssd_gate_norm(y, xs_ref[0:db, :], z_ref[...], dvec_ref[...], gain_ref[...])


def _ssd_step(proj, state_conv2d, state_ssm4d, conv_w, conv_b, dt_bias, a_log, d_skip, ssd_norm_g):
    db = proj.shape[0]
    assert db <= LANES and db % 8 == 0
    dvec = jnp.repeat(d_skip, SSD_HEAD_DIM).reshape(1, D_INNER)
    small = [conv_w, conv_b.reshape(1, CONV_DIM), _pad_lanes(dt_bias), _pad_lanes(a_log), dvec,
             ssd_norm_g.reshape(1, D_INNER), jnp.asarray(_head_expand_indicator(), BF16)]

    def full(a):
        return pl.BlockSpec(a.shape, lambda b: (0,) * a.ndim)

    cs_w = (CONV_W - 1) * CONV_DIM
    hspec = pl.BlockSpec((1, N_PAIRS, LANES, D_STATE), lambda b: (b, 0, 0, 0))
    return pl.pallas_call(
        _ssd_step_kernel,
        grid=(db,),
        in_specs=[pl.BlockSpec((db, CONV_DIM), lambda b: (0, 0)),
                  pl.BlockSpec((db, D_INNER), lambda b: (0, COL_Z // D_INNER)),
                  pl.BlockSpec((db, LANES), lambda b: (0, COL_SMALL // LANES)),
                  pl.BlockSpec((db, cs_w), lambda b: (0, 0)),
                  hspec] + [full(a) for a in small],
        out_specs=[pl.BlockSpec((db, D_INNER), lambda b: (0, 0)),
                   hspec,
                   pl.BlockSpec((db, cs_w), lambda b: (0, 0))],
        out_shape=[jax.ShapeDtypeStruct((db, D_INNER), F32),
                   jax.ShapeDtypeStruct(state_ssm4d.shape, F32),
                   jax.ShapeDtypeStruct((db, cs_w), F32)],
        scratch_shapes=[pltpu.VMEM((LANES, D_INNER), F32),
                        pltpu.VMEM((LANES, SSD_GROUPS * D_STATE), F32),
                        pltpu.VMEM((LANES, SSD_GROUPS * D_STATE), F32),
                        pltpu.VMEM((D_INNER, LANES), F32),
                        pltpu.VMEM((D_INNER, LANES), F32),
                        pltpu.VMEM((D_INNER, LANES), F32)],
        compiler_params=_cparams(("arbitrary",)),
        name="ssd_step",
    )(proj, proj, proj, state_conv2d, state_ssm4d, *small)


def _page_copies(pt_ref, seq, cache_ref, buf_ref, slot, sem_ref, n_pages):
    return [pltpu.make_async_copy(cache_ref.at[pt_ref[seq, p]], buf_ref.at[slot, p], sem_ref.at[slot])
            for p in range(n_pages)]


def _paged_fetch(pt_ref, streams, n_pages):
    b = pl.program_id(0)
    slot = b % 2

    @pl.when(b == 0)
    def _():
        for cache_ref, buf_ref, sem_ref in streams:
            for cp in _page_copies(pt_ref, 0, cache_ref, buf_ref, 0, sem_ref, n_pages):
                cp.start()

    @pl.when(b + 1 < pl.num_programs(0))
    def _():
        for cache_ref, buf_ref, sem_ref in streams:
            for cp in _page_copies(pt_ref, b + 1, cache_ref, buf_ref, 1 - slot, sem_ref, n_pages):
                cp.start()

    for cache_ref, buf_ref, sem_ref in streams:
        for cp in _page_copies(pt_ref, b, cache_ref, buf_ref, slot, sem_ref, n_pages):
            cp.wait()
    return slot


def _dec_score_kernel(n_pages, pt_ref, iq_ref, w_ref, iknew_ref, cache_ref, o_ref, buf_ref, sem_ref):
    b = pl.program_id(0)
    slot = _paged_fetch(pt_ref, [(cache_ref, buf_ref, sem_ref)], n_pages)
    past = n_pages * PAGE_SIZE
    iq = iq_ref[...].astype(BF16)
    w = w_ref[...]
    for p in range(n_pages):
        s = _dot(iq, buf_ref[slot, p].astype(BF16))
        o_ref[0, :, p * PAGE_SIZE:(p + 1) * PAGE_SIZE] = jnp.sum(w * jnp.maximum(s, 0.0), axis=0, keepdims=True)
    iknew = iknew_ref[pl.ds(b, 1), :].astype(BF16).astype(F32)
    s_new = jnp.sum(iq.astype(F32) * iknew, axis=1, keepdims=True)
    sc_new = jnp.sum(w * jnp.maximum(s_new, 0.0), axis=0, keepdims=True)
    lane = lax.broadcasted_iota(I32, (1, LANES), 1)
    o_ref[0, :, past:past + LANES] = jnp.where(lane == 0, sc_new, NEG_INF)


def _dec_scores(page_table, cache_ik, iq8, w8, ikb_new):
    db, n_pages = page_table.shape
    s_pad = (n_pages + 1) * PAGE_SIZE
    grid_spec = pltpu.PrefetchScalarGridSpec(
        num_scalar_prefetch=1,
        grid=(db,),
        in_specs=[pl.BlockSpec((IDX_HEADS, IDX_DIM), lambda b, pt: (b, 0)),
                  pl.BlockSpec((IDX_HEADS, 1), lambda b, pt: (b, 0)),
                  pl.BlockSpec(ikb_new.shape, lambda b, pt: (0, 0)),
                  pl.BlockSpec(memory_space=pl.ANY)],
        out_specs=pl.BlockSpec((1, 1, s_pad), lambda b, pt: (b, 0, 0)),
        scratch_shapes=[pltpu.VMEM((2, n_pages, IDX_DIM, PAGE_SIZE), F32),
                        pltpu.SemaphoreType.DMA((2,))],
    )
    return pl.pallas_call(
        functools.partial(_dec_score_kernel, n_pages),
        grid_spec=grid_spec,
        out_shape=jax.ShapeDtypeStruct((db, 1, s_pad), F32),
        compiler_params=_cparams(("arbitrary",)),
        name="dec_scores",
    )(page_table, iq8, w8, ikb_new, cache_ik)


def _dec_select_kernel(n_sel, sc_ref, triu_ref, mask_ref, key_ref):
    db, s_pad = sc_ref.shape
    ntile = s_pad // LANES
    key_ref[...] = _sortable_key(sc_ref[...])

    def count_ge(trial):
        acc = jnp.zeros((db, LANES), I32)
        for j in range(ntile):
            acc = acc + jnp.where(key_ref[:, j * LANES:(j + 1) * LANES] >= trial, 1, 0)
        return jnp.sum(acc, axis=1, keepdims=True)

    tau = _kth_largest_key(count_ge, n_sel, (db, 1))
    acc = jnp.zeros((db, LANES), I32)
    for j in range(ntile):
        acc = acc + jnp.where(key_ref[:, j * LANES:(j + 1) * LANES] > tau, 1, 0)
    need = (n_sel - jnp.sum(acc, axis=1, keepdims=True)).astype(F32)
    seen = jnp.zeros((db, 1), F32)
    for j in range(ntile):
        k = key_ref[:, j * LANES:(j + 1) * LANES]
        eq = k == tau
        pref = _dot(jnp.where(eq, 1.0, 0.0).astype(BF16), triu_ref[...]) + seen
        sel = jnp.logical_or(k > tau, jnp.logical_and(eq, pref <= need))
        mask_ref[:, j * LANES:(j + 1) * LANES] = jnp.where(sel, 0.0, NEG_INF)
        seen = pref[:, LANES - 1:LANES]


def _dec_select(scores2d, n_sel):
    db, s_pad = scores2d.shape
    triu = jnp.asarray(np.triu(np.ones((LANES, LANES), np.float32)), BF16)
    return pl.pallas_call(
        functools.partial(_dec_select_kernel, n_sel),
        grid=(1,),
        in_specs=[pl.BlockSpec((db, s_pad), lambda i: (0, 0)),
                  pl.BlockSpec(triu.shape, lambda i: (0, 0))],
        out_specs=pl.BlockSpec((db, s_pad), lambda i: (0, 0)),
        out_shape=jax.ShapeDtypeStruct((db, s_pad), F32),
        scratch_shapes=[pltpu.VMEM((db, s_pad), I32)],
        compiler_params=_cparams(("arbitrary",)),
        name="dec_select",
    )(scores2d, triu)


def _dec_attn_kernel(n_pages, pt_ref, q_ref, mask_ref, bias_ref, knew_ref, vnew_ref, ck_ref, cv_ref,
                     y_ref, kbuf_ref, vbuf_ref, ksem_ref, vsem_ref):
    b = pl.program_id(0)
    slot = _paged_fetch(pt_ref, [(ck_ref, kbuf_ref, ksem_ref), (cv_ref, vbuf_ref, vsem_ref)], n_pages)
    past = n_pages * PAGE_SIZE
    q = q_ref[...]
    hrow = lax.broadcasted_iota(I32, (ATT_HEADS, KV_WIDTH), 0)
    lane = lax.broadcasted_iota(I32, (ATT_HEADS, KV_WIDTH), 1)
    own = (hrow // (ATT_HEADS // KV_HEADS)) == (lane // ATT_HEAD_DIM)
    qbd = jnp.where(own, jnp.concatenate([q] * KV_HEADS, axis=1), jnp.zeros_like(q[:, :1]))
    knew = knew_ref[pl.ds(b, 1), :].astype(BF16).astype(F32)
    l_new = jnp.sum(qbd.astype(F32) * knew, axis=1, keepdims=True)
    vnew = vnew_ref[pl.ds(b, 1), :].astype(BF16).astype(F32)
    mask = mask_ref[0]
    lg = jnp.concatenate([_dot(qbd, kbuf_ref[slot, p].astype(BF16)) for p in range(n_pages)], axis=1)
    lg = lg + bias_ref[:, 0:past] + mask[:, 0:past]
    tl = lax.broadcasted_iota(I32, (ATT_HEADS, LANES), 1)
    lg_new = jnp.where(tl == 0, l_new, 0.0) + bias_ref[:, past:past + LANES] + mask[:, past:past + LANES]
    m = jnp.maximum(jnp.max(lg, axis=1, keepdims=True), jnp.max(lg_new, axis=1, keepdims=True))
    p = jnp.exp(lg - m)
    p_new = jnp.exp(lg_new - m)
    denom = jnp.sum(p, axis=1, keepdims=True) + jnp.sum(p_new, axis=1, keepdims=True)
    pn = p_new[:, 0:1].astype(BF16).astype(F32)
    pb = p.astype(BF16)
    out = pn * vnew
    for pg in range(n_pages):
        out = out + _dot_nt(pb[:, pg * PAGE_SIZE:(pg + 1) * PAGE_SIZE], vbuf_ref[slot, pg].astype(BF16))
    out = out / denom
    r16 = lax.broadcasted_iota(I32, (ATT_HEADS, ATT_HEAD_DIM), 0) // (ATT_HEADS // KV_HEADS)
    y = jnp.zeros((ATT_HEADS, ATT_HEAD_DIM), F32)
    for g in range(KV_HEADS):
        y = jnp.where(r16 == g, out[:, g * ATT_HEAD_DIM:(g + 1) * ATT_HEAD_DIM], y)
    y_ref[...] = y


def _dec_attn(page_table, q16, mask3d, bias_dec, kn_new, v_new, cache_k3d, cache_v3d):
    db, n_pages = page_table.shape
    s_pad = (n_pages + 1) * PAGE_SIZE
    grid_spec = pltpu.PrefetchScalarGridSpec(
        num_scalar_prefetch=1,
        grid=(db,),
        in_specs=[pl.BlockSpec((ATT_HEADS, ATT_HEAD_DIM), lambda b, pt: (b, 0)),
                  pl.BlockSpec((1, 1, s_pad), lambda b, pt: (b, 0, 0)),
                  pl.BlockSpec(bias_dec.shape, lambda b, pt: (0, 0)),
                  pl.BlockSpec(kn_new.shape, lambda b, pt: (0, 0)),
                  pl.BlockSpec(v_new.shape, lambda b, pt: (0, 0)),
                  pl.BlockSpec(memory_space=pl.ANY),
                  pl.BlockSpec(memory_space=pl.ANY)],
        out_specs=pl.BlockSpec((ATT_HEADS, ATT_HEAD_DIM), lambda b, pt: (b, 0)),
        scratch_shapes=[pltpu.VMEM((2, n_pages, KV_WIDTH, PAGE_SIZE), F32),
                        pltpu.VMEM((2, n_pages, KV_WIDTH, PAGE_SIZE), F32),
                        pltpu.SemaphoreType.DMA((2,)),
                        pltpu.SemaphoreType.DMA((2,))],
    )
    return pl.pallas_call(
        functools.partial(_dec_attn_kernel, n_pages),
        grid_spec=grid_spec,
        out_shape=jax.ShapeDtypeStruct((db * ATT_HEADS, ATT_HEAD_DIM), F32),
        compiler_params=_cparams(("arbitrary",)),
        name="dec_attn",
    )(page_table, q16, mask3d, bias_dec, kn_new, v_new, cache_k3d, cache_v3d)


def _channel_mixer(hb, x1, peer_wq, peer_keys, ub, vtb, tn):
    sel = _peer_select(hb, peer_wq, peer_keys, tn)
    return _peer_mix(hb, ub, vtb, sel, x1, tn)


def kernel(x_prompt, x_sample, cache_k, cache_v, cache_idx_k, state_ssm, state_conv, page_table, rel_bias, norm1_g, w_in, conv_w, conv_b, dt_bias, a_log, d_skip, ssd_norm_g, q_norm_g, k_norm_g, w_branch_a, w_branch_b, w_out, norm2_g, peer_wq, peer_keys, peer_u, peer_v):
    depth = w_in.shape[0]
    bsz, t, _ = x_prompt.shape
    db, ds, _ = x_sample.shape
    assert depth == 1 and ds == 1, "single layer, one new token per sequence"
    n = bsz * t
    n_pages = page_table.shape[1]
    past = n_pages * PAGE_SIZE
    l = 0

    w_perm = _permute_w_in(w_in[l])
    wa, wb, wo = w_branch_a[l].astype(BF16), w_branch_b[l].astype(BF16), w_out[l].astype(BF16)
    wq, keys = peer_wq[l].astype(BF16), peer_keys[l].astype(BF16)
    ub = peer_u[l].astype(BF16)
    vtb = jnp.transpose(peer_v[l].reshape(PEER_EXPERTS // PEER_SUB, PEER_SUB, D_MODEL), (0, 2, 1)).astype(BF16)
    ssd_w = (conv_w[l], conv_b[l], dt_bias[l], a_log[l], d_skip[l], ssd_norm_g[l])

    dec_dist = np.concatenate([past - np.arange(past), np.zeros(PAGE_SIZE, np.int64)])
    n_prompt_b = 3 * Q_BLOCK * Q_BLOCK
    buckets = np.concatenate([_prompt_bias_buckets().reshape(-1), _t5_bucket_np(dec_dist)])
    pad = (-buckets.shape[0]) % BIAS_TJ
    bias_all = _bias_lookup(rel_bias, np.concatenate([buckets, np.zeros(pad, np.int32)]))
    btile = bias_all[:, :n_prompt_b].reshape(ATT_HEADS, 3, Q_BLOCK, Q_BLOCK)
    bias_dec = bias_all[:, n_prompt_b:n_prompt_b + past + PAGE_SIZE]

    xp = x_prompt.reshape(n, D_MODEL)
    proj = _in_proj(xp, norm1_g[l], w_perm, min(1024, n))
    y_ssd, h_t = _ssd_prompt(proj, bsz, *ssd_w)
    prep = _attn_prep(proj, q_norm_g[l], k_norm_g[l], min(512, n))
    y_att = _attn_prompt(prep, _attn_select(prep, bsz), btile, bsz)
    x1, hb = _merge(y_ssd, y_att, proj, xp, wa, wb, wo, norm2_g[l], min(512, n))
    y_prompt = _channel_mixer(hb, x1, wq, keys, ub, vtb, min(512, n))

    k_prompt = prep[1].reshape(1, bsz, t, KV_HEADS, ATT_HEAD_DIM)
    v_prompt = prep[2].reshape(1, bsz, t, KV_HEADS, ATT_HEAD_DIM)
    ik_prompt = prep[3].reshape(1, bsz, t, IDX_DIM)
    ssm_prompt = h_t.reshape(1, bsz, SSD_HEADS, SSD_HEAD_DIM, D_STATE)
    conv_prompt = proj.reshape(bsz, t, PROJ_COLS)[:, t - (CONV_W - 1):, :CONV_DIM][None]

    xs_ = x_sample.reshape(db, D_MODEL)
    proj_s = _in_proj(xs_, norm1_g[l], w_perm, db)
    y_ssd_s, h_s, conv_s = _ssd_step(proj_s, state_conv[l].reshape(db, (CONV_W - 1) * CONV_DIM),
                                     state_ssm[l].reshape(db, N_PAIRS, LANES, D_STATE), *ssd_w)
    prep_s = _attn_prep(proj_s, q_norm_g[l], k_norm_g[l], db)
    qn_s, kn_s, v_s, ik_s, _, _, _, _, _, _, iw_s = prep_s
    iq8 = proj_s[:, COL_IQ:COL_IQ + IDX_HEADS * IDX_DIM].reshape(db * IDX_HEADS, IDX_DIM)
    w8 = iw_s[:, SMALL_IW:SMALL_IW + IDX_HEADS].reshape(db * IDX_HEADS, 1)
    ikt_pages = jnp.transpose(cache_idx_k[l], (0, 2, 1))
    kt_pages = jnp.transpose(cache_k[l], (0, 2, 3, 1)).reshape(-1, KV_WIDTH, PAGE_SIZE)
    vt_pages = jnp.transpose(cache_v[l], (0, 2, 3, 1)).reshape(-1, KV_WIDTH, PAGE_SIZE)
    scores = _dec_scores(page_table, ikt_pages, iq8, w8, ik_s)
    n_sel = min(TOPK_KEYS_MAX, (past + ds) // 4)
    mask = _dec_select(scores.reshape(db, past + PAGE_SIZE), n_sel)
    y_att_s = _dec_attn(page_table, qn_s.reshape(db * ATT_HEADS, ATT_HEAD_DIM),
                        mask.reshape(db, 1, past + PAGE_SIZE), bias_dec, kn_s, v_s,
                        kt_pages, vt_pages)
    x1_s, hb_s = _merge(y_ssd_s, y_att_s.reshape(db, ATT_WIDTH), proj_s, xs_, wa, wb, wo, norm2_g[l], db)
    y_sample = _channel_mixer(hb_s, x1_s, wq, keys, ub, vtb, db)

    return (y_prompt.reshape(bsz, t, D_MODEL), y_sample.reshape(db, ds, D_MODEL),
            k_prompt, v_prompt, ik_prompt, ssm_prompt, conv_prompt,
            kn_s.reshape(1, db, ds, KV_HEADS, ATT_HEAD_DIM), v_s.reshape(1, db, ds, KV_HEADS, ATT_HEAD_DIM),
            ik_s.reshape(1, db, ds, IDX_DIM),
            h_s.reshape(1, db, SSD_HEADS, SSD_HEAD_DIM, D_STATE),
            conv_s.reshape(1, db, CONV_W - 1, CONV_DIM))
```

```python
import functools
import math

import numpy as np
import jax
import jax.numpy as jnp
from jax import lax
from jax.experimental import pallas as pl
from jax.experimental.pallas import tpu as pltpu

F32 = jnp.float32
BF16 = jnp.bfloat16
I32 = jnp.int32

D_MODEL = 1024
PAGE_SIZE = 128
D_INNER = 2048
SSD_HEAD_DIM = 64
SSD_HEADS = 32
SSD_GROUPS = 8
D_STATE = 128
CONV_W = 4
CONV_DIM = D_INNER + 2 * SSD_GROUPS * D_STATE
SSD_CHUNK = 128
ATT_HEADS = 16
ATT_HEAD_DIM = 64
ATT_WIDTH = ATT_HEADS * ATT_HEAD_DIM
KV_HEADS = 4
KV_WIDTH = KV_HEADS * ATT_HEAD_DIM
IDX_HEADS = 8
IDX_DIM = 64
TOPK_KEYS_MAX = 256
Q_BLOCK = 128
NUM_BUCKETS = 32
MAX_DISTANCE = 128
PEER_HEADS = 8
PEER_N_KEYS = 128
PEER_EXPERTS = PEER_N_KEYS * PEER_N_KEYS
PEER_HALF = 128
PEER_TOPK = 16
EPS = 1e-6

LANES = 128
SUBLANES = 8
VMEM_LIMIT_BYTES = 56 * 1024 * 1024

COL_XBC = 0
COL_Z = COL_XBC + CONV_DIM
COL_Q = COL_Z + D_INNER
COL_K = COL_Q + ATT_WIDTH
COL_V = COL_K + KV_WIDTH
COL_IQ = COL_V + KV_WIDTH
COL_GA = COL_IQ + IDX_HEADS * IDX_DIM
COL_GB = COL_GA + D_MODEL
COL_SMALL = COL_GB + D_MODEL
SMALL_DT = 0
SMALL_IK = SSD_HEADS
SMALL_IW = SSD_HEADS + IDX_DIM
PROJ_COLS = COL_SMALL + LANES
PROJ_TN = 1152

INT_MIN = -2 ** 31
NEG_INF = float("-inf")


def _cparams(sem):
    return pltpu.CompilerParams(dimension_semantics=sem, vmem_limit_bytes=VMEM_LIMIT_BYTES)


def _split3(x):
    hi = x.astype(BF16)
    r1 = x - hi.astype(F32)
    mid = r1.astype(BF16)
    lo = (r1 - mid.astype(F32)).astype(BF16)
    return hi, mid, lo


def _dot(a, b):
    return jnp.dot(a, b, preferred_element_type=F32)


def _dot_nt(a, b):
    return lax.dot_general(a, b, (((1,), (1,)), ((), ())), preferred_element_type=F32)


def _dot_tn(a, b):
    return lax.dot_general(a, b, (((0,), (0,)), ((), ())), preferred_element_type=F32)


def _exact_dot(x, onehot_bf16):
    hi, mid, lo = _split3(x)
    return _dot(hi, onehot_bf16) + _dot(mid, onehot_bf16) + _dot(lo, onehot_bf16)


def _in_proj_kernel(x_ref, g_ref, w_ref, o_ref, xn_ref):
    @pl.when(pl.program_id(1) == 0)
    def _():
        x = x_ref[...]
        ms = jnp.mean(x * x, axis=-1, keepdims=True)
        xn_ref[...] = (x * lax.rsqrt(ms + EPS) * g_ref[...]).astype(BF16)

    o_ref[...] = _dot(xn_ref[...], w_ref[...])


def _in_proj(x2d, g, w_perm, tm):
    n = x2d.shape[0]
    return pl.pallas_call(
        _in_proj_kernel,
        grid=(n // tm, PROJ_COLS // PROJ_TN),
        in_specs=[pl.BlockSpec((tm, D_MODEL), lambda i, j: (i, 0)),
                  pl.BlockSpec((1, D_MODEL), lambda i, j: (0, 0)),
                  pl.BlockSpec((D_MODEL, PROJ_TN), lambda i, j: (0, j))],
        out_specs=pl.BlockSpec((tm, PROJ_TN), lambda i, j: (i, j)),
        out_shape=jax.ShapeDtypeStruct((n, PROJ_COLS), F32),
        scratch_shapes=[pltpu.VMEM((tm, D_MODEL), BF16)],
        compiler_params=_cparams(("parallel", "arbitrary")),
        name="in_proj",
    )(x2d, g.reshape(1, D_MODEL), w_perm)


def _permute_w_in(w_in):
    offs = np.cumsum([0, D_INNER, CONV_DIM, SSD_HEADS, ATT_WIDTH, KV_WIDTH, KV_WIDTH,
                      IDX_HEADS * IDX_DIM, IDX_DIM, IDX_HEADS, D_MODEL, D_MODEL])
    z, xbc, dt, q, k, v, iq, ik, iw, ga, gb = [w_in[:, offs[i]:offs[i + 1]] for i in range(11)]
    pad = jnp.zeros((D_MODEL, LANES - SSD_HEADS - IDX_DIM - IDX_HEADS), w_in.dtype)
    return jnp.concatenate([xbc, z, q, k, v, iq, ga, gb, dt, ik, iw, pad], axis=1).astype(BF16)


def _seg_indicator(width, seg):
    m = np.zeros((width, LANES), np.float32)
    m[np.arange(width), np.arange(width) // seg] = 1.0
    return m


def _half_placement():
    lo = np.zeros((KV_WIDTH, KV_HEADS * LANES), np.float32)
    hi = np.zeros((KV_WIDTH, KV_HEADS * LANES), np.float32)
    c = np.arange(KV_WIDTH)
    g, d = c // ATT_HEAD_DIM, c % ATT_HEAD_DIM
    lo[c, g * LANES + d] = 1.0
    hi[c, g * LANES + ATT_HEAD_DIM + d] = 1.0
    return lo, hi


def _head_rms(x, ind, ind_t, gain):
    sq = x * x
    hi = sq.astype(BF16)
    lo = (sq - hi.astype(F32)).astype(BF16)
    ss = _dot(hi, ind) + _dot(lo, ind)
    r = lax.rsqrt(ss * (1.0 / ATT_HEAD_DIM) + EPS)
    rb = _exact_dot(r, ind_t)
    return x * rb * gain


def _attn_prep_kernel(q_ref, kv_ref, sm_ref, gq_ref, gk_ref, indq_ref, indqt_ref, indk_ref, indkt_ref,
                      plo_ref, phi_ref,
                      qn_ref, kn_ref, v_ref, ik_ref, klo_ref, khi_ref, vlo_ref, vhi_ref,
                      ikb_ref, iqb_ref, iw_ref):
    qn = _head_rms(q_ref[...], indq_ref[...], indqt_ref[...], gq_ref[...])
    qn_ref[...] = (qn * (ATT_HEAD_DIM ** -0.5)).astype(BF16)
    kv = kv_ref[...]
    kn = _head_rms(kv[:, :KV_WIDTH], indk_ref[...], indkt_ref[...], gk_ref[...])
    v = kv[:, KV_WIDTH:]
    kn_ref[...] = kn
    v_ref[...] = v
    knb = kn.astype(BF16)
    vb = v.astype(BF16)
    for src, place_ref, dst_ref in ((knb, plo_ref, klo_ref), (knb, phi_ref, khi_ref),
                                    (vb, plo_ref, vlo_ref), (vb, phi_ref, vhi_ref)):
        placed = _dot(src, place_ref[...]).astype(BF16)
        for g in range(KV_HEADS):
            dst_ref[g] = placed[:, g * LANES:(g + 1) * LANES]
    sm = sm_ref[...]
    ik = sm[:, SMALL_IK:SMALL_IK + IDX_DIM]
    ik_ref[...] = ik
    ikb_ref[...] = ik.astype(BF16)
    iw_ref[...] = sm * (IDX_HEADS ** -0.5 * IDX_DIM ** -0.5)


def _attn_prep(proj, q_norm_g, k_norm_g, tm):
    n = proj.shape[0]
    gq = jnp.tile(q_norm_g, ATT_HEADS).reshape(1, ATT_WIDTH)
    gk = jnp.tile(k_norm_g, KV_HEADS).reshape(1, KV_WIDTH)
    indq = _seg_indicator(ATT_WIDTH, ATT_HEAD_DIM)
    indk = _seg_indicator(KV_WIDTH, ATT_HEAD_DIM)
    plo, phi = _half_placement()
    consts = [jnp.asarray(a, BF16) for a in (indq, indq.T, indk, indk.T, plo, phi)]

    def full(a):
        return pl.BlockSpec(a.shape, lambda i: (0,) * a.ndim)

    outs = [
        ((n, ATT_WIDTH), BF16), ((n, KV_WIDTH), F32), ((n, KV_WIDTH), F32), ((n, IDX_DIM), F32),
        ((KV_HEADS, n, LANES), BF16), ((KV_HEADS, n, LANES), BF16),
        ((KV_HEADS, n, LANES), BF16), ((KV_HEADS, n, LANES), BF16),
        ((n, IDX_DIM), BF16), ((n, IDX_HEADS * IDX_DIM), BF16), ((n, LANES), F32),
    ]

    def body(q_ref, kv_ref, sm_ref, iq_ref, *rest):
        (gq_ref, gk_ref, indq_ref, indqt_ref, indk_ref, indkt_ref, plo_ref, phi_ref,
         qn_ref, kn_ref, v_ref, ik_ref, klo_ref, khi_ref, vlo_ref, vhi_ref, ikb_ref, iqb_ref, iw_ref) = rest
        _attn_prep_kernel(q_ref, kv_ref, sm_ref, gq_ref, gk_ref, indq_ref, indqt_ref, indk_ref, indkt_ref,
                          plo_ref, phi_ref, qn_ref, kn_ref, v_ref, ik_ref, klo_ref, khi_ref, vlo_ref, vhi_ref,
                          ikb_ref, iqb_ref, iw_ref)
        iqb_ref[...] = iq_ref[...].astype(BF16)

    return pl.pallas_call(
        body,
        grid=(n // tm,),
        in_specs=[pl.BlockSpec((tm, ATT_WIDTH), lambda i: (i, COL_Q // ATT_WIDTH)),
                  pl.BlockSpec((tm, 2 * KV_WIDTH), lambda i: (i, COL_K // (2 * KV_WIDTH))),
                  pl.BlockSpec((tm, LANES), lambda i: (i, COL_SMALL // LANES)),
                  pl.BlockSpec((tm, IDX_HEADS * IDX_DIM), lambda i: (i, COL_IQ // (IDX_HEADS * IDX_DIM))),
                  full(gq), full(gk)] + [full(c) for c in consts],
        out_specs=[pl.BlockSpec((tm, s[1]), lambda i: (i, 0)) if len(s) == 2
                   else pl.BlockSpec((KV_HEADS, tm, LANES), lambda i: (0, i, 0)) for s, _ in outs],
        out_shape=[jax.ShapeDtypeStruct(s, d) for s, d in outs],
        compiler_params=_cparams(("parallel",)),
        name="attn_prep",
    )(proj, proj, proj, proj, gq, gk, *consts)


def _merge_kernel(ya_ref, yb_ref, ga_ref, gb_ref, x_ref, wa_ref, wb_ref, wo_ref, g2_ref, x1_ref, h_ref):
    a = _dot(ya_ref[...].astype(BF16), wa_ref[...])
    b = _dot(yb_ref[...].astype(BF16), wb_ref[...])
    merged = jax.nn.sigmoid(ga_ref[...]) * a + jax.nn.sigmoid(gb_ref[...]) * b
    x1 = x_ref[...] + _dot(merged.astype(BF16), wo_ref[...])
    x1_ref[...] = x1
    ms = jnp.mean(x1 * x1, axis=-1, keepdims=True)
    h_ref[...] = (x1 * lax.rsqrt(ms + EPS) * g2_ref[...]).astype(BF16)


def _merge(y_ssd, y_att, proj, x2d, wa, wb, wo, g2, tm):
    n = x2d.shape[0]

    def full(a):
        return pl.BlockSpec(a.shape, lambda i: (0,) * a.ndim)

    g2 = g2.reshape(1, D_MODEL)
    return pl.pallas_call(
        _merge_kernel,
        grid=(n // tm,),
        in_specs=[pl.BlockSpec((tm, D_INNER), lambda i: (i, 0)),
                  pl.BlockSpec((tm, ATT_WIDTH), lambda i: (i, 0)),
                  pl.BlockSpec((tm, D_MODEL), lambda i: (i, COL_GA // D_MODEL)),
                  pl.BlockSpec((tm, D_MODEL), lambda i: (i, COL_GB // D_MODEL)),
                  pl.BlockSpec((tm, D_MODEL), lambda i: (i, 0)),
                  full(wa), full(wb), full(wo), full(g2)],
        out_specs=[pl.BlockSpec((tm, D_MODEL), lambda i: (i, 0)),
                   pl.BlockSpec((tm, D_MODEL), lambda i: (i, 0))],
        out_shape=[jax.ShapeDtypeStruct((n, D_MODEL), F32), jax.ShapeDtypeStruct((n, D_MODEL), BF16)],
        compiler_params=_cparams(("parallel",)),
        name="merge",
    )(y_ssd, y_att, proj, proj, x2d, wa, wb, wo, g2)


def _sorting_network(n_pow2, n):
    pairs = []
    p = 1
    while p < n_pow2:
        k = p
        while k >= 1:
            for j in range(k % p, n_pow2 - k, 2 * k):
                for i in range(min(k, n_pow2 - j - k)):
                    if (i + j) // (2 * p) == (i + j + k) // (2 * p):
                        pairs.append((i + j, i + j + k))
            k //= 2
        p *= 2
    return [(a, b) for a, b in pairs if b < n]


def _top_values(x, k):
    m = x.shape[0] // SUBLANES
    cols = [x[i * SUBLANES:(i + 1) * SUBLANES] for i in range(m)]
    for a, b in _sorting_network(1 << (m - 1).bit_length(), m):
        cols[a], cols[b] = jnp.maximum(cols[a], cols[b]), jnp.minimum(cols[a], cols[b])
    vals = []
    for r in range(k):
        top = jnp.max(cols[0], axis=0, keepdims=True)
        vals.append(top)
        if r == k - 1:
            break
        popped = cols[0] >= top
        for i in range(min(m, k - 1 - r)):
            below = cols[i + 1] if i + 1 < m else NEG_INF
            cols[i] = jnp.where(popped, below, cols[i])
    return vals


PEER_RANKS = PEER_TOPK + 1


def _peer_select_kernel(h_ref, wq_ref, keys_ref, eth_ref, c1_ref, e2_ref):
    q = _dot(h_ref[...], wq_ref[...]).astype(BF16)
    t = q.shape[0]
    riota = lax.broadcasted_iota(I32, (SUBLANES, t), 0)
    for hd in range(PEER_HEADS):
        base = hd * 2 * PEER_HALF
        s1 = _dot_nt(keys_ref[0, hd], q[:, base:base + PEER_HALF])
        s2 = _dot_nt(keys_ref[1, hd], q[:, base + PEER_HALF:base + 2 * PEER_HALF])
        v1 = _top_values(s1, PEER_RANKS)
        v2 = _top_values(s2, PEER_RANKS)
        pad = jnp.full((SUBLANES - 1, t), NEG_INF, F32)
        v2a = jnp.concatenate(v2 + [pad], axis=0)
        blocks = [v1[0] + v2a, v1[1] + v2a[:SUBLANES]]
        for i in range(2, SUBLANES):
            blocks.append(jnp.where(riota < PEER_RANKS // (i + 1), v1[i] + v2a[:SUBLANES], NEG_INF))
        blocks.append(jnp.concatenate(v1[SUBLANES:] + [pad], axis=0) + v2[0])
        cand = jnp.concatenate(blocks, axis=0)
        top = _top_values(cand, PEER_RANKS)
        zsum = sum(jnp.exp(tv - top[0]) for tv in top[:PEER_TOPK])
        th = 0.5 * (top[PEER_TOPK - 1] + top[PEER_TOPK]) - s1
        eth = jnp.exp(th - v2[0])
        c1 = jnp.exp(s1 - v1[0]) / zsum
        cw = eth_ref.shape[-1]
        for ch in range(eth_ref.shape[1]):
            eth_ref[hd, ch] = eth[:, ch * cw:(ch + 1) * cw]
            c1_ref[hd, ch] = c1[:, ch * cw:(ch + 1) * cw]
        e2_ref[hd] = jnp.exp(s2 - v2[0])


def _peer_select(hb, wq, keys, tn):
    n = hb.shape[0]
    big = jax.ShapeDtypeStruct((PEER_HEADS, PEER_N_KEYS, n), F32)
    bspec = pl.BlockSpec((PEER_HEADS, PEER_N_KEYS, tn), lambda i: (0, 0, i))
    cw = min(LANES, tn)
    chunked = jax.ShapeDtypeStruct((PEER_HEADS, n // cw, PEER_N_KEYS, cw), F32)
    cspec = pl.BlockSpec((PEER_HEADS, tn // cw, PEER_N_KEYS, cw), lambda i: (0, i, 0, 0))
    return pl.pallas_call(
        _peer_select_kernel,
        grid=(n // tn,),
        in_specs=[pl.BlockSpec((tn, D_MODEL), lambda i: (i, 0)),
                  pl.BlockSpec(wq.shape, lambda i: (0, 0)),
                  pl.BlockSpec(keys.shape, lambda i: (0, 0, 0, 0))],
        out_specs=[cspec, cspec, bspec],
        out_shape=[chunked, chunked, big],
        compiler_params=_cparams(("parallel",)),
        name="peer_select",
    )(hb, wq, keys)


def _gelu_tanh(x):
    c = math.sqrt(2.0 / math.pi)
    half = 0.5 * x
    return half + half * jnp.tanh(x * (c + (c * 0.044715) * (x * x)))


PEER_EC = 1024
PEER_SUB = 512


def _peer_mix_kernel(h_ref, u_ref, vt_ref, eth_ref, c1_ref, e2_ref, x1_ref, y_ref,
                     acc_ref, act_ref, g_ref, ht_ref):
    j = pl.program_id(1)

    @pl.when(j == 0)
    def _():
        acc_ref[...] = jnp.zeros_like(acc_ref)
        ht_ref[...] = h_ref[...].T

    tn = act_ref.shape[1]
    cw = min(LANES, tn)

    def gate_chunk(sc):
        start = sc * PEER_SUB
        erows = pl.ds(start, PEER_SUB)
        act_ref[...] = _gelu_tanh(_dot(u_ref[erows, :], ht_ref[...]))
        slot = sc
        for ii in range(PEER_SUB // PEER_N_KEYS):
            i1 = j * (PEER_EC // PEER_N_KEYS) + sc * (PEER_SUB // PEER_N_KEYS) + ii
            rows = slice(ii * PEER_N_KEYS, (ii + 1) * PEER_N_KEYS)
            for ch in range(tn // cw):
                cols = slice(ch * cw, (ch + 1) * cw)
                grp = (PEER_N_KEYS // SUBLANES, SUBLANES, cw)
                w = jnp.zeros(grp, F32)
                for hd in range(PEER_HEADS):
                    ethb = jnp.broadcast_to(eth_ref[hd, ch, pl.ds(i1, 1), :], (SUBLANES, cw))
                    c1b = jnp.broadcast_to(c1_ref[hd, ch, pl.ds(i1, 1), :], (SUBLANES, cw))
                    e2t = e2_ref[hd, :, cols].reshape(grp)
                    w = w + jnp.where(e2t >= ethb, e2t * c1b, 0.0)
                g = act_ref[rows, cols].reshape(grp) * w
                g_ref[slot, rows, cols] = g.reshape(PEER_N_KEYS, cw).astype(BF16)

    nsub = PEER_EC // PEER_SUB
    for sc in range(nsub):
        gate_chunk(sc)
    vt = jnp.concatenate([vt_ref[sc] for sc in range(nsub)], axis=1)
    acc_ref[...] += _dot(vt, g_ref[...].reshape(PEER_EC, tn))

    @pl.when(j == pl.num_programs(1) - 1)
    def _():
        y_ref[...] = x1_ref[...] + acc_ref[...].T


def _peer_mix(hb, ub, vtb, sel, x1, tn):
    n = hb.shape[0]
    eth, c1, e2 = sel
    bspec = pl.BlockSpec((PEER_HEADS, PEER_N_KEYS, tn), lambda i, j: (0, 0, i))
    cw = min(LANES, tn)
    cspec = pl.BlockSpec((PEER_HEADS, tn // cw, PEER_N_KEYS, cw), lambda i, j: (0, i, 0, 0))
    return pl.pallas_call(
        _peer_mix_kernel,
        grid=(n // tn, PEER_EXPERTS // PEER_EC),
        in_specs=[pl.BlockSpec((tn, D_MODEL), lambda i, j: (i, 0)),
                  pl.BlockSpec((PEER_EC, D_MODEL), lambda i, j: (j, 0)),
                  pl.BlockSpec((PEER_EC // PEER_SUB, D_MODEL, PEER_SUB), lambda i, j: (j, 0, 0)),
                  cspec, cspec, bspec,
                  pl.BlockSpec((tn, D_MODEL), lambda i, j: (i, 0))],
        out_specs=pl.BlockSpec((tn, D_MODEL), lambda i, j: (i, 0)),
        out_shape=jax.ShapeDtypeStruct((n, D_MODEL), F32),
        scratch_shapes=[pltpu.VMEM((D_MODEL, tn), F32),
                        pltpu.VMEM((PEER_SUB, tn), F32),
                        pltpu.VMEM((PEER_EC // PEER_SUB, PEER_SUB, tn), BF16),
                        pltpu.VMEM((D_MODEL, tn), BF16)],
        compiler_params=_cparams(("parallel", "arbitrary")),
        name="peer_mix",
    )(hb, ub, vtb, eth, c1, e2, x1)


HALO = 8
N_PAIRS = SSD_HEADS // 2
GROUP_W = D_INNER // SSD_GROUPS


def _softplus(x):
    return jnp.maximum(x, 0.0) + jnp.log1p(jnp.exp(-jnp.abs(x)))


def _silu(x):
    return x * jax.nn.sigmoid(x)


def _lane_bcast(col):
    return jnp.broadcast_to(col, (col.shape[0], LANES))


def _ssd_gate_norm(y, xs, z, dvec, gain):
    y = (y + dvec * xs) * _silu(z)
    parts = []
    for g in range(SSD_GROUPS):
        yg = y[:, g * GROUP_W:(g + 1) * GROUP_W]
        ms = jnp.mean(yg * yg, axis=-1, keepdims=True)
        parts.append(yg * lax.rsqrt(ms + EPS))
    return jnp.concatenate(parts, axis=1) * gain


def _ssd_prompt_kernel(xbc_ref, z_ref, sm_ref, cw_ref, cb_ref, dtb_ref, alog_ref, dvec_ref, gain_ref, tri_ref,
                       y_ref, hout_ref, ext_ref, h_ref, yacc_ref):
    c = pl.program_id(1)

    @pl.when(c == 0)
    def _():
        ext_ref[0:HALO, :] = jnp.zeros((HALO, CONV_DIM), F32)
        h_ref[...] = jnp.zeros_like(h_ref)

    x = xbc_ref[...]
    ext_ref[HALO:HALO + SSD_CHUNK, :] = x
    conv = cb_ref[...]
    for j in range(CONV_W):
        conv = conv + cw_ref[j:j + 1, :] * ext_ref[pl.ds(HALO - (CONV_W - 1) + j, SSD_CHUNK), :]
    ext_ref[0:HALO, :] = x[SSD_CHUNK - HALO:, :]
    xc = _silu(conv)
    xs = xc[:, :D_INNER]

    lane = lax.broadcasted_iota(I32, (SSD_CHUNK, LANES), 1)
    row = lax.broadcasted_iota(I32, (SSD_CHUNK, LANES), 0)
    dt = jnp.where(lane < SSD_HEADS, _softplus(sm_ref[...] + dtb_ref[...]), 0.0)
    da = dt * (-jnp.exp(alog_ref[...]))
    hi, mid, lo = _split3(da)
    tri = tri_ref[...]
    acum = _dot(tri, hi) + _dot(tri, mid) + _dot(tri, lo)
    acum_t = acum.T
    dt_t = dt.T
    alast = acum[SSD_CHUNK - 1:SSD_CHUNK, :]
    e_in = jnp.exp(acum)
    e_out = jnp.exp(alast - acum) * dt
    e_all = jnp.exp(alast)
    causal = row >= lane
    lo_half = lane < SSD_HEAD_DIM

    for g in range(SSD_GROUPS):
        bg = xc[:, D_INNER + g * D_STATE:D_INNER + (g + 1) * D_STATE].astype(BF16)
        cg = xc[:, D_INNER + SSD_GROUPS * D_STATE + g * D_STATE:
                D_INNER + SSD_GROUPS * D_STATE + (g + 1) * D_STATE].astype(BF16)
        cbm = _dot_nt(cg, bg)
        for pp in range(2):
            pair = 2 * g + pp
            r0 = 2 * pair
            xp = xs[:, r0 * SSD_HEAD_DIM:(r0 + 2) * SSD_HEAD_DIM]
            yp = jnp.zeros((SSD_CHUNK, LANES), F32)
            for hh in range(2):
                r = r0 + hh
                seg = _lane_bcast(acum[:, r:r + 1]) - acum_t[r:r + 1, :]
                lmat = jnp.exp(jnp.where(causal, seg, NEG_INF))
                m = (cbm * lmat * dt_t[r:r + 1, :]).astype(BF16)
                xm = jnp.where(lo_half == (hh == 0), xp, 0.0).astype(BF16)
                yp = yp + _dot(m, xm)
            hp = h_ref[pair]
            scale_in = jnp.where(lo_half, _lane_bcast(e_in[:, r0:r0 + 1]), _lane_bcast(e_in[:, r0 + 1:r0 + 2]))
            yp = yp + _dot_nt(cg, hp.astype(BF16)) * scale_in
            scale_out = jnp.where(lo_half, _lane_bcast(e_out[:, r0:r0 + 1]), _lane_bcast(e_out[:, r0 + 1:r0 + 2]))
            xd = (xp * scale_out).astype(BF16)
            hdec = jnp.where(row < SSD_HEAD_DIM, e_all[:, r0:r0 + 1], e_all[:, r0 + 1:r0 + 2])
            h_ref[pair] = hp * hdec + _dot_tn(xd, bg)
            yacc_ref[:, r0 * SSD_HEAD_DIM:(r0 + 2) * SSD_HEAD_DIM] = yp

    y_ref[...] = _ssd_gate_norm(yacc_ref[...], xs, z_ref[...], dvec_ref[...], gain_ref[...])

    @pl.when(c == pl.num_programs(1) - 1)
    def _():
        hout_ref[0] = h_ref[...]


def _pad_lanes(v):
    return jnp.zeros((1, LANES), F32).at[0, :v.shape[0]].set(v)


def _ssd_prompt(proj, bsz, conv_w, conv_b, dt_bias, a_log, d_skip, ssd_norm_g):
    n = proj.shape[0]
    nc = n // bsz // SSD_CHUNK
    tri = jnp.asarray(np.tril(np.ones((SSD_CHUNK, SSD_CHUNK), np.float32)), BF16)
    dvec = jnp.repeat(d_skip, SSD_HEAD_DIM).reshape(1, D_INNER)
    small = [conv_w, conv_b.reshape(1, CONV_DIM), _pad_lanes(dt_bias), _pad_lanes(a_log), dvec,
             ssd_norm_g.reshape(1, D_INNER), tri]

    def full(a):
        return pl.BlockSpec(a.shape, lambda b, c: (0,) * a.ndim)

    return pl.pallas_call(
        _ssd_prompt_kernel,
        grid=(bsz, nc),
        in_specs=[pl.BlockSpec((SSD_CHUNK, CONV_DIM), lambda b, c: (b * nc + c, 0)),
                  pl.BlockSpec((SSD_CHUNK, D_INNER), lambda b, c: (b * nc + c, COL_Z // D_INNER)),
                  pl.BlockSpec((SSD_CHUNK, LANES), lambda b, c: (b * nc + c, COL_SMALL // LANES))]
                 + [full(a) for a in small],
        out_specs=[pl.BlockSpec((SSD_CHUNK, D_INNER), lambda b, c: (b * nc + c, 0)),
                   pl.BlockSpec((1, N_PAIRS, LANES, D_STATE), lambda b, c: (b, 0, 0, 0))],
        out_shape=[jax.ShapeDtypeStruct((n, D_INNER), F32),
                   jax.ShapeDtypeStruct((bsz, N_PAIRS, LANES, D_STATE), F32)],
        scratch_shapes=[pltpu.VMEM((HALO + SSD_CHUNK, CONV_DIM), F32),
                        pltpu.VMEM((N_PAIRS, LANES, D_STATE), F32),
                        pltpu.VMEM((SSD_CHUNK, D_INNER), F32)],
        compiler_params=_cparams(("parallel", "arbitrary")),
        name="ssd_prompt",
    )(proj, proj, proj, *small)


def _t5_bucket_np(dist):
    n = np.maximum(dist, 0)
    max_exact = NUM_BUCKETS // 2
    nf = np.maximum(n, 1).astype(np.float32)
    ratio = np.log(nf / np.float32(max_exact)) / np.float32(math.log(MAX_DISTANCE / max_exact))
    large = max_exact + (ratio * np.float32(NUM_BUCKETS - max_exact)).astype(np.int32)
    large = np.minimum(large, NUM_BUCKETS - 1)
    return np.where(n < max_exact, n, large).astype(np.int32)


BIAS_TJ = 2048


def _bias_kernel(rel_t_ref, bucket_ref, o_ref):
    ids = lax.broadcasted_iota(I32, (NUM_BUCKETS, BIAS_TJ), 0)
    onehot = jnp.where(ids == bucket_ref[...], 1.0, 0.0).astype(BF16)
    o_ref[...] = _exact_dot(rel_t_ref[...], onehot)


def _bias_lookup(rel_bias, buckets):
    j = buckets.shape[0]
    return pl.pallas_call(
        _bias_kernel,
        grid=(j // BIAS_TJ,),
        in_specs=[pl.BlockSpec((ATT_HEADS, NUM_BUCKETS), lambda i: (0, 0)),
                  pl.BlockSpec((1, BIAS_TJ), lambda i: (0, i))],
        out_specs=pl.BlockSpec((ATT_HEADS, BIAS_TJ), lambda i: (0, i)),
        out_shape=jax.ShapeDtypeStruct((ATT_HEADS, j), F32),
        compiler_params=_cparams(("parallel",)),
        name="bias_lookup",
    )(rel_bias.T, jnp.asarray(buckets.reshape(1, j)))


def _prompt_bias_buckets():
    i = np.arange(Q_BLOCK)[:, None]
    j = np.arange(Q_BLOCK)[None, :]
    far = np.full((Q_BLOCK, Q_BLOCK), 2 * Q_BLOCK)
    return _t5_bucket_np(np.stack([far, i - j + Q_BLOCK, i - j]))


KEY_TILES = 4
KEY_CHUNK = KEY_TILES * Q_BLOCK


def _fold_tiles(x, op):
    out = x[:, 0:LANES]
    for u in range(1, x.shape[1] // LANES):
        out = op(out, x[:, u * LANES:(u + 1) * LANES])
    return out


def _sortable_key(x):
    b = pltpu.bitcast(x, I32)
    return jnp.where(b < 0, b ^ 0x7FFFFFFF, b)


def _kth_largest_key(count_ge, k, shape):
    def body(it, cand):
        trial = cand + jnp.left_shift(jnp.int32(1), 31 - it)
        return jnp.where(count_ge(trial) >= k, trial, cand)
    return lax.fori_loop(0, 32, body, jnp.full(shape, INT_MIN, I32))


def _chunks_before(qb):
    a, r = qb // KEY_TILES, qb % KEY_TILES
    return KEY_TILES * (a * (a + 1) // 2) + r * (a + 1)


def _attn_select_kernel(n_sel, nqb, iq_ref, iw_ref, ik_ref, triu_ref, mask_ref, key_ref, cand_ref):
    nch = mask_ref.shape[1]
    row = lax.broadcasted_iota(I32, (Q_BLOCK, KEY_CHUNK), 0)
    lane = lax.broadcasted_iota(I32, (Q_BLOCK, KEY_CHUNK), 1)

    def admissible(qb, c):
        return lane + (c * KEY_CHUNK - qb * Q_BLOCK) <= row

    def score_block(qb, _):
        rows = pl.ds(pl.multiple_of(qb * Q_BLOCK, Q_BLOCK), Q_BLOCK)
        iw = iw_ref[rows, :]
        iq = iq_ref[rows, :]
        wcols = [jnp.broadcast_to(iw[:, SMALL_IW + h:SMALL_IW + h + 1], (Q_BLOCK, KEY_CHUNK))
                 for h in range(IDX_HEADS)]
        iqs = [iq[:, h * IDX_DIM:(h + 1) * IDX_DIM] for h in range(IDX_HEADS)]
        base = _chunks_before(qb)

        def score_chunk(c, _):
            ikc = ik_ref[pl.ds(pl.multiple_of(c * KEY_CHUNK, KEY_CHUNK), KEY_CHUNK), :]
            sc = jnp.zeros((Q_BLOCK, KEY_CHUNK), F32)
            for h in range(IDX_HEADS):
                sc = sc + wcols[h] * jnp.maximum(_dot_nt(iqs[h], ikc), 0.0)
            key_ref[base + c] = _sortable_key(jnp.where(admissible(qb, c), sc, NEG_INF))
            return 0

        lax.fori_loop(0, qb // KEY_TILES + 1, score_chunk, 0)
        cand_ref[qb] = jnp.full((Q_BLOCK, LANES), INT_MIN, I32)
        return 0

    lax.fori_loop(0, nqb, score_block, 0)

    def search_bit(it, _):
        bit = jnp.left_shift(jnp.int32(1), 31 - it)
        for qb in range(nqb):
            trial = cand_ref[qb] + bit
            acc = jnp.zeros((Q_BLOCK, LANES), I32)
            for c in range(qb // KEY_TILES + 1):
                k = key_ref[_chunks_before(qb) + c]
                for u in range(KEY_TILES):
                    acc = acc + jnp.where(k[:, u * LANES:(u + 1) * LANES] >= trial, 1, 0)
            enough = jnp.sum(acc, axis=1, keepdims=True) >= n_sel
            cand_ref[qb] = jnp.where(enough, trial, cand_ref[qb])
        return 0

    lax.fori_loop(0, 32, search_bit, 0)

    def mask_block(qb, _):
        tau = cand_ref[qb][:, 0:1]
        base = _chunks_before(qb)
        nchunk = qb // KEY_TILES + 1

        def count_gt(c, acc):
            return acc + _fold_tiles(jnp.where(key_ref[base + c] > tau, 1, 0), jnp.add)
        n_gt = jnp.sum(lax.fori_loop(0, nchunk, count_gt, jnp.zeros((Q_BLOCK, LANES), I32)), axis=1, keepdims=True)
        need = (n_sel - n_gt).astype(F32)

        def mask_chunk(c, seen):
            k = key_ref[base + c]
            eq = k == tau
            eqf = jnp.where(eq, 1.0, 0.0)
            pref = _dot(eqf.astype(BF16), triu_ref[...]) + seen
            sel = jnp.logical_or(k > tau, jnp.logical_and(eq, pref <= need))
            mask_ref[qb, c] = jnp.where(jnp.logical_and(sel, admissible(qb, c)), 0.0, NEG_INF).astype(BF16)
            return seen + jnp.sum(eqf, axis=1, keepdims=True)

        lax.fori_loop(0, nchunk, mask_chunk, jnp.zeros((Q_BLOCK, 1), F32))

        def blank_chunk(c, _):
            mask_ref[qb, c] = jnp.full((Q_BLOCK, KEY_CHUNK), NEG_INF, BF16)
            return 0

        lax.fori_loop(nchunk, nch, blank_chunk, 0)
        return 0

    lax.fori_loop(0, nqb, mask_block, 0)


def _attn_select(prep, bsz):
    ikb, iqb, iw = prep[8], prep[9], prep[10]
    n = iqb.shape[0]
    t = n // bsz
    nqb = t // Q_BLOCK
    assert t % KEY_CHUNK == 0
    nch = t // KEY_CHUNK
    n_sel = min(TOPK_KEYS_MAX, t // 4)
    triu = jnp.asarray(np.triu(np.ones((KEY_CHUNK, KEY_CHUNK), np.float32)), BF16)
    n_key_chunks = int(_chunks_before(nqb))
    return pl.pallas_call(
        functools.partial(_attn_select_kernel, n_sel, nqb),
        grid=(bsz,),
        in_specs=[pl.BlockSpec((t, IDX_HEADS * IDX_DIM), lambda b: (b, 0)),
                  pl.BlockSpec((t, LANES), lambda b: (b, 0)),
                  pl.BlockSpec((t, IDX_DIM), lambda b: (b, 0)),
                  pl.BlockSpec(triu.shape, lambda b: (0, 0))],
        out_specs=pl.BlockSpec((nqb, nch, Q_BLOCK, KEY_CHUNK), lambda b: (b, 0, 0, 0)),
        out_shape=jax.ShapeDtypeStruct((bsz * nqb, nch, Q_BLOCK, KEY_CHUNK), BF16),
        scratch_shapes=[pltpu.VMEM((n_key_chunks, Q_BLOCK, KEY_CHUNK), I32),
                        pltpu.VMEM((nqb, Q_BLOCK, LANES), I32)],
        compiler_params=_cparams(("parallel",)),
        name="attn_select",
    )(iqb, iw, ikb, triu)


HEADS_PER_KV = ATT_HEADS // KV_HEADS


def _attn_prompt_kernel(q_ref, mask_ref, klo_ref, khi_ref, vlo_ref, vhi_ref, bt_ref, y_ref, lbuf_ref):
    qb = pl.program_id(1)
    g = pl.program_id(2)
    nchunk = qb // KEY_TILES + 1

    k_refs = (klo_ref, khi_ref)
    v_refs = (vlo_ref, vhi_ref)
    qps = [q_ref[:, pp * LANES:(pp + 1) * LANES] for pp in range(HEADS_PER_KV // 2)]

    def logits_chunk(c, mxs):
        rows = pl.ds(pl.multiple_of(c * KEY_CHUNK, KEY_CHUNK), KEY_CHUNK)
        kinds = [jnp.clip(c * KEY_TILES + u - qb + 2, 0, 2) for u in range(KEY_TILES)]
        mask = mask_ref[0, c].astype(F32)
        new = []
        for hh in range(HEADS_PER_KV):
            bias = jnp.concatenate([bt_ref[g * HEADS_PER_KV + hh, kd] for kd in kinds], axis=1)
            lg = _dot_nt(qps[hh // 2], k_refs[hh % 2][g, rows, :]) + bias + mask
            lbuf_ref[hh, c] = lg
            new.append(jnp.maximum(mxs[hh], _fold_tiles(lg, jnp.maximum)))
        return tuple(new)

    neg = jnp.full((Q_BLOCK, LANES), NEG_INF, F32)
    mxs = lax.fori_loop(0, nchunk, logits_chunk, (neg,) * HEADS_PER_KV)
    mrows = [jnp.max(m, axis=1, keepdims=True) for m in mxs]

    def pv_chunk(c, carry):
        rows = pl.ds(pl.multiple_of(c * KEY_CHUNK, KEY_CHUNK), KEY_CHUNK)
        new = []
        for hh in range(HEADS_PER_KV):
            lsum, acc = carry[hh]
            p = jnp.exp(lbuf_ref[hh, c] - mrows[hh])
            new.append((lsum + _fold_tiles(p, jnp.add), acc + _dot(p.astype(BF16), v_refs[hh % 2][g, rows, :])))
        return tuple(new)

    zero = jnp.zeros((Q_BLOCK, LANES), F32)
    res = lax.fori_loop(0, nchunk, pv_chunk, ((zero, zero),) * HEADS_PER_KV)
    outs = [acc / jnp.sum(lsum, axis=1, keepdims=True) for lsum, acc in res]
    for pp in range(HEADS_PER_KV // 2):
        y_ref[:, pp * LANES:(pp + 1) * LANES] = outs[2 * pp] + outs[2 * pp + 1]


def _attn_prompt(prep, mask, btile, bsz):
    qn, klo, khi, vlo, vhi = prep[0], prep[4], prep[5], prep[6], prep[7]
    n = qn.shape[0]
    t = n // bsz
    nqb = t // Q_BLOCK
    nch = t // KEY_CHUNK
    kvspec = pl.BlockSpec((KV_HEADS, t, LANES), lambda b, q, g: (0, b, 0))
    qw = HEADS_PER_KV * ATT_HEAD_DIM
    return pl.pallas_call(
        _attn_prompt_kernel,
        grid=(bsz, nqb, KV_HEADS),
        in_specs=[pl.BlockSpec((Q_BLOCK, qw), lambda b, q, g: (b * nqb + q, g)),
                  pl.BlockSpec((1, nch, Q_BLOCK, KEY_CHUNK), lambda b, q, g: (b * nqb + q, 0, 0, 0)),
                  kvspec, kvspec, kvspec, kvspec,
                  pl.BlockSpec(btile.shape, lambda b, q, g: (0, 0, 0, 0))],
        out_specs=pl.BlockSpec((Q_BLOCK, qw), lambda b, q, g: (b * nqb + q, g)),
        out_shape=jax.ShapeDtypeStruct((n, ATT_WIDTH), F32),
        scratch_shapes=[pltpu.VMEM((HEADS_PER_KV, nch, Q_BLOCK, KEY_CHUNK), F32)],
        compiler_params=_cparams(("parallel", "parallel", "arbitrary")),
        name="attn_prompt",
    )(qn, mask, klo, khi, vlo, vhi, btile)


def _head_expand_indicator():
    m = np.zeros((LANES, D_INNER), np.float32)
    m[np.arange(D_INNER) // SSD_HEAD_DIM, np.arange(D_INNER)] = 1.0
    return m


def _ssd_step_kernel(xbc_ref, z_ref, sm_ref, sc_ref, hin_ref, cw_ref, cb_ref, dtb_ref, alog_ref, dvec_ref, gain_ref,
                     exp_ref, y_ref, hout_ref, cs_ref, xs_ref, bm_ref, cm_ref, dtx_t_ref, dec_t_ref, y_t_ref):
    b = pl.program_id(0)
    db = xbc_ref.shape[0]

    @pl.when(b == 0)
    def _():
        x = xbc_ref[...]
        sc = sc_ref[...]
        conv = cb_ref[...] + cw_ref[CONV_W - 1:CONV_W, :] * x
        for j in range(CONV_W - 1):
            conv = conv + cw_ref[j:j + 1, :] * sc[:, j * CONV_DIM:(j + 1) * CONV_DIM]
        cs_ref[:, :(CONV_W - 2) * CONV_DIM] = sc[:, CONV_DIM:]
        cs_ref[:, (CONV_W - 2) * CONV_DIM:] = x
        xc = _silu(conv)
        xs = xc[:, :D_INNER]
        lane = lax.broadcasted_iota(I32, (db, LANES), 1)
        dt = jnp.where(lane < SSD_HEADS, _softplus(sm_ref[...] + dtb_ref[...]), 0.0)
        dec = jnp.exp(dt * (-jnp.exp(alog_ref[...])))
        xs_ref[0:db, :] = xs
        bm_ref[0:db, :] = xc[:, D_INNER:D_INNER + SSD_GROUPS * D_STATE]
        cm_ref[0:db, :] = xc[:, D_INNER + SSD_GROUPS * D_STATE:]
        dtx_t_ref[:, 0:db] = (_exact_dot(dt, exp_ref[...]) * xs).T
        dec_t_ref[:, 0:db] = _exact_dot(dec, exp_ref[...]).T
        y_t_ref[...] = jnp.zeros_like(y_t_ref)

    lane = lax.broadcasted_iota(I32, (LANES, LANES), 1)
    mine = lane == b
    base = pl.multiple_of((b // 8) * 8, 8)
    my_row = lax.broadcasted_iota(I32, (8, SSD_GROUPS * D_STATE), 0) == (b % 8)
    bm_b = jnp.sum(jnp.where(my_row, bm_ref[pl.ds(base, 8), :], 0.0), axis=0, keepdims=True)
    cm_b = jnp.sum(jnp.where(my_row, cm_ref[pl.ds(base, 8), :], 0.0), axis=0, keepdims=True)
    for pair in range(N_PAIRS):
        g = pair // 2
        rows = slice(pair * LANES, (pair + 1) * LANES)
        xcol = jnp.sum(jnp.where(mine, dtx_t_ref[rows, :], 0.0), axis=1, keepdims=True)
        dcol = jnp.sum(jnp.where(mine, dec_t_ref[rows, :], 0.0), axis=1, keepdims=True)
        brow = bm_b[:, g * D_STATE:(g + 1) * D_STATE]
        crow = cm_b[:, g * D_STATE:(g + 1) * D_STATE]
        hnew = hin_ref[0, pair] * dcol + xcol * brow
        hout_ref[0, pair] = hnew
        ycol = jnp.sum(hnew * crow, axis=1, keepdims=True)
        y_t_ref[rows, :] = jnp.where(mine, ycol, y_t_ref[rows, :])

    @pl.when(b == pl.num_programs(0) - 1)
    def _():
        y = y_t_ref[...].T[0:db, :]
        y_ref[...] = _ssd_gate_norm(y, xs_ref[0:db, :], z_ref[...], dvec_ref[...], gain_ref[...])


def _ssd_step(proj, state_conv2d, state_ssm4d, conv_w, conv_b, dt_bias, a_log, d_skip, ssd_norm_g):
    db = proj.shape[0]
    assert db <= LANES and db % 8 == 0
    dvec = jnp.repeat(d_skip, SSD_HEAD_DIM).reshape(1, D_INNER)
    small = [conv_w, conv_b.reshape(1, CONV_DIM), _pad_lanes(dt_bias), _pad_lanes(a_log), dvec,
             ssd_norm_g.reshape(1, D_INNER), jnp.asarray(_head_expand_indicator(), BF16)]

    def full(a):
        return pl.BlockSpec(a.shape, lambda b: (0,) * a.ndim)

    cs_w = (CONV_W - 1) * CONV_DIM
    hspec = pl.BlockSpec((1, N_PAIRS, LANES, D_STATE), lambda b: (b, 0, 0, 0))
    return pl.pallas_call(
        _ssd_step_kernel,
        grid=(db,),
        in_specs=[pl.BlockSpec((db, CONV_DIM), lambda b: (0, 0)),
                  pl.BlockSpec((db, D_INNER), lambda b: (0, COL_Z // D_INNER)),
                  pl.BlockSpec((db, LANES), lambda b: (0, COL_SMALL // LANES)),
                  pl.BlockSpec((db, cs_w), lambda b: (0, 0)),
                  hspec] + [full(a) for a in small],
        out_specs=[pl.BlockSpec((db, D_INNER), lambda b: (0, 0)),
                   hspec,
                   pl.BlockSpec((db, cs_w), lambda b: (0, 0))],
        out_shape=[jax.ShapeDtypeStruct((db, D_INNER), F32),
                   jax.ShapeDtypeStruct(state_ssm4d.shape, F32),
                   jax.ShapeDtypeStruct((db, cs_w), F32)],
        scratch_shapes=[pltpu.VMEM((LANES, D_INNER), F32),
                        pltpu.VMEM((LANES, SSD_GROUPS * D_STATE), F32),
                        pltpu.VMEM((LANES, SSD_GROUPS * D_STATE), F32),
                        pltpu.VMEM((D_INNER, LANES), F32),
                        pltpu.VMEM((D_INNER, LANES), F32),
                        pltpu.VMEM((D_INNER, LANES), F32)],
        compiler_params=_cparams(("arbitrary",)),
        name="ssd_step",
    )(proj, proj, proj, state_conv2d, state_ssm4d, *small)


def _page_copies(pt_ref, seq, cache_ref, buf_ref, slot, sem_ref, n_pages):
    return [pltpu.make_async_copy(cache_ref.at[pt_ref[seq, p]], buf_ref.at[slot, p], sem_ref.at[slot])
            for p in range(n_pages)]


def _paged_fetch(pt_ref, streams, n_pages):
    b = pl.program_id(0)
    slot = b % 2

    @pl.when(b == 0)
    def _():
        for cache_ref, buf_ref, sem_ref in streams:
            for p, cp in enumerate(_page_copies(pt_ref, 0, cache_ref, buf_ref, 0, sem_ref, n_pages)):
                cp.start(priority=p % 2)

    @pl.when(b + 1 < pl.num_programs(0))
    def _():
        for cache_ref, buf_ref, sem_ref in streams:
            for p, cp in enumerate(_page_copies(pt_ref, b + 1, cache_ref, buf_ref, 1 - slot, sem_ref, n_pages)):
                cp.start(priority=p % 2)

    for cache_ref, buf_ref, sem_ref in streams:
        for cp in _page_copies(pt_ref, b, cache_ref, buf_ref, slot, sem_ref, n_pages):
            cp.wait()
    return slot


def _dec_score_kernel(n_pages, pt_ref, iq_ref, w_ref, iknew_ref, cache_ref, o_ref, buf_ref, sem_ref):
    b = pl.program_id(0)
    slot = _paged_fetch(pt_ref, [(cache_ref, buf_ref, sem_ref)], n_pages)
    past = n_pages * PAGE_SIZE
    iq = iq_ref[...].astype(BF16)
    w = w_ref[...]
    for p in range(n_pages):
        s = _dot(iq, buf_ref[slot, p].astype(BF16))
        o_ref[0, :, p * PAGE_SIZE:(p + 1) * PAGE_SIZE] = jnp.sum(w * jnp.maximum(s, 0.0), axis=0, keepdims=True)
    iknew = iknew_ref[pl.ds(b, 1), :].astype(BF16).astype(F32)
    s_new = jnp.sum(iq.astype(F32) * iknew, axis=1, keepdims=True)
    sc_new = jnp.sum(w * jnp.maximum(s_new, 0.0), axis=0, keepdims=True)
    lane = lax.broadcasted_iota(I32, (1, LANES), 1)
    o_ref[0, :, past:past + LANES] = jnp.where(lane == 0, sc_new, NEG_INF)


def _dec_scores(page_table, cache_ik, iq8, w8, ikb_new):
    db, n_pages = page_table.shape
    s_pad = (n_pages + 1) * PAGE_SIZE
    grid_spec = pltpu.PrefetchScalarGridSpec(
        num_scalar_prefetch=1,
        grid=(db,),
        in_specs=[pl.BlockSpec((IDX_HEADS, IDX_DIM), lambda b, pt: (b, 0)),
                  pl.BlockSpec((IDX_HEADS, 1), lambda b, pt: (b, 0)),
                  pl.BlockSpec(ikb_new.shape, lambda b, pt: (0, 0)),
                  pl.BlockSpec(memory_space=pl.ANY)],
        out_specs=pl.BlockSpec((1, 1, s_pad), lambda b, pt: (b, 0, 0)),
        scratch_shapes=[pltpu.VMEM((2, n_pages, IDX_DIM, PAGE_SIZE), F32),
                        pltpu.SemaphoreType.DMA((2,))],
    )
    return pl.pallas_call(
        functools.partial(_dec_score_kernel, n_pages),
        grid_spec=grid_spec,
        out_shape=jax.ShapeDtypeStruct((db, 1, s_pad), F32),
        compiler_params=_cparams(("arbitrary",)),
        name="dec_scores",
    )(page_table, iq8, w8, ikb_new, cache_ik)


def _dec_select_kernel(n_sel, sc_ref, triu_ref, mask_ref, key_ref):
    db, s_pad = sc_ref.shape
    ntile = s_pad // LANES
    key_ref[...] = _sortable_key(sc_ref[...])

    def count_ge(trial):
        acc = jnp.zeros((db, LANES), I32)
        for j in range(ntile):
            acc = acc + jnp.where(key_ref[:, j * LANES:(j + 1) * LANES] >= trial, 1, 0)
        return jnp.sum(acc, axis=1, keepdims=True)

    tau = _kth_largest_key(count_ge, n_sel, (db, 1))
    acc = jnp.zeros((db, LANES), I32)
    for j in range(ntile):
        acc = acc + jnp.where(key_ref[:, j * LANES:(j + 1) * LANES] > tau, 1, 0)
    need = (n_sel - jnp.sum(acc, axis=1, keepdims=True)).astype(F32)
    seen = jnp.zeros((db, 1), F32)
    for j in range(ntile):
        k = key_ref[:, j * LANES:(j + 1) * LANES]
        eq = k == tau
        pref = _dot(jnp.where(eq, 1.0, 0.0).astype(BF16), triu_ref[...]) + seen
        sel = jnp.logical_or(k > tau, jnp.logical_and(eq, pref <= need))
        mask_ref[:, j * LANES:(j + 1) * LANES] = jnp.where(sel, 0.0, NEG_INF)
        seen = pref[:, LANES - 1:LANES]


def _dec_select(scores2d, n_sel):
    db, s_pad = scores2d.shape
    triu = jnp.asarray(np.triu(np.ones((LANES, LANES), np.float32)), BF16)
    return pl.pallas_call(
        functools.partial(_dec_select_kernel, n_sel),
        grid=(1,),
        in_specs=[pl.BlockSpec((db, s_pad), lambda i: (0, 0)),
                  pl.BlockSpec(triu.shape, lambda i: (0, 0))],
        out_specs=pl.BlockSpec((db, s_pad), lambda i: (0, 0)),
        out_shape=jax.ShapeDtypeStruct((db, s_pad), F32),
        scratch_shapes=[pltpu.VMEM((db, s_pad), I32)],
        compiler_params=_cparams(("arbitrary",)),
        name="dec_select",
    )(scores2d, triu)


def _dec_attn_kernel(n_pages, pt_ref, q_ref, mask_ref, bias_ref, knew_ref, vnew_ref, ck_ref, cv_ref,
                     y_ref, kbuf_ref, vbuf_ref, ksem_ref, vsem_ref):
    b = pl.program_id(0)
    slot = _paged_fetch(pt_ref, [(ck_ref, kbuf_ref, ksem_ref), (cv_ref, vbuf_ref, vsem_ref)], n_pages)
    past = n_pages * PAGE_SIZE
    q = q_ref[...]
    hrow = lax.broadcasted_iota(I32, (ATT_HEADS, KV_WIDTH), 0)
    lane = lax.broadcasted_iota(I32, (ATT_HEADS, KV_WIDTH), 1)
    own = (hrow // (ATT_HEADS // KV_HEADS)) == (lane // ATT_HEAD_DIM)
    qbd = jnp.where(own, jnp.concatenate([q] * KV_HEADS, axis=1), jnp.zeros_like(q[:, :1]))
    knew = knew_ref[pl.ds(b, 1), :].astype(BF16).astype(F32)
    l_new = jnp.sum(qbd.astype(F32) * knew, axis=1, keepdims=True)
    vnew = vnew_ref[pl.ds(b, 1), :].astype(BF16).astype(F32)
    mask = mask_ref[0]
    lg = jnp.concatenate([_dot(qbd, kbuf_ref[slot, p].astype(BF16)) for p in range(n_pages)], axis=1)
    lg = lg + bias_ref[:, 0:past] + mask[:, 0:past]
    tl = lax.broadcasted_iota(I32, (ATT_HEADS, LANES), 1)
    lg_new = jnp.where(tl == 0, l_new, 0.0) + bias_ref[:, past:past + LANES] + mask[:, past:past + LANES]
    m = jnp.maximum(jnp.max(lg, axis=1, keepdims=True), jnp.max(lg_new, axis=1, keepdims=True))
    p = jnp.exp(lg - m)
    p_new = jnp.exp(lg_new - m)
    denom = jnp.sum(p, axis=1, keepdims=True) + jnp.sum(p_new, axis=1, keepdims=True)
    pn = p_new[:, 0:1].astype(BF16).astype(F32)
    pb = p.astype(BF16)
    out = pn * vnew
    for pg in range(n_pages):
        out = out + _dot_nt(pb[:, pg * PAGE_SIZE:(pg + 1) * PAGE_SIZE], vbuf_ref[slot, pg].astype(BF16))
    out = out / denom
    r16 = lax.broadcasted_iota(I32, (ATT_HEADS, ATT_HEAD_DIM), 0) // (ATT_HEADS // KV_HEADS)
    y = jnp.zeros((ATT_HEADS, ATT_HEAD_DIM), F32)
    for g in range(KV_HEADS):
        y = jnp.where(r16 == g, out[:, g * ATT_HEAD_DIM:(g + 1) * ATT_HEAD_DIM], y)
    y_ref[...] = y


def _dec_attn(page_table, q16, mask3d, bias_dec, kn_new, v_new, cache_k3d, cache_v3d):
    db, n_pages = page_table.shape
    s_pad = (n_pages + 1) * PAGE_SIZE
    grid_spec = pltpu.PrefetchScalarGridSpec(
        num_scalar_prefetch=1,
        grid=(db,),
        in_specs=[pl.BlockSpec((ATT_HEADS, ATT_HEAD_DIM), lambda b, pt: (b, 0)),
                  pl.BlockSpec((1, 1, s_pad), lambda b, pt: (b, 0, 0)),
                  pl.BlockSpec(bias_dec.shape, lambda b, pt: (0, 0)),
                  pl.BlockSpec(kn_new.shape, lambda b, pt: (0, 0)),
                  pl.BlockSpec(v_new.shape, lambda b, pt: (0, 0)),
                  pl.BlockSpec(memory_space=pl.ANY),
                  pl.BlockSpec(memory_space=pl.ANY)],
        out_specs=pl.BlockSpec((ATT_HEADS, ATT_HEAD_DIM), lambda b, pt: (b, 0)),
        scratch_shapes=[pltpu.VMEM((2, n_pages, KV_WIDTH, PAGE_SIZE), F32),
                        pltpu.VMEM((2, n_pages, KV_WIDTH, PAGE_SIZE), F32),
                        pltpu.SemaphoreType.DMA((2,)),
                        pltpu.SemaphoreType.DMA((2,))],
    )
    return pl.pallas_call(
        functools.partial(_dec_attn_kernel, n_pages),
        grid_spec=grid_spec,
        out_shape=jax.ShapeDtypeStruct((db * ATT_HEADS, ATT_HEAD_DIM), F32),
        compiler_params=_cparams(("arbitrary",)),
        name="dec_attn",
    )(page_table, q16, mask3d, bias_dec, kn_new, v_new, cache_k3d, cache_v3d)


def _channel_mixer(hb, x1, peer_wq, peer_keys, ub, vtb, tn):
    sel = _peer_select(hb, peer_wq, peer_keys, tn)
    return _peer_mix(hb, ub, vtb, sel, x1, tn)


def kernel(x_prompt, x_sample, cache_k, cache_v, cache_idx_k, state_ssm, state_conv, page_table, rel_bias, norm1_g, w_in, conv_w, conv_b, dt_bias, a_log, d_skip, ssd_norm_g, q_norm_g, k_norm_g, w_branch_a, w_branch_b, w_out, norm2_g, peer_wq, peer_keys, peer_u, peer_v):
    depth = w_in.shape[0]
    bsz, t, _ = x_prompt.shape
    db, ds, _ = x_sample.shape
    assert depth == 1 and ds == 1, "single layer, one new token per sequence"
    n = bsz * t
    n_pages = page_table.shape[1]
    past = n_pages * PAGE_SIZE
    l = 0

    w_perm = _permute_w_in(w_in[l])
    wa, wb, wo = w_branch_a[l].astype(BF16), w_branch_b[l].astype(BF16), w_out[l].astype(BF16)
    wq, keys = peer_wq[l].astype(BF16), peer_keys[l].astype(BF16)
    ub = peer_u[l].astype(BF16)
    vtb = jnp.transpose(peer_v[l].reshape(PEER_EXPERTS // PEER_SUB, PEER_SUB, D_MODEL), (0, 2, 1)).astype(BF16)
    ssd_w = (conv_w[l], conv_b[l], dt_bias[l], a_log[l], d_skip[l], ssd_norm_g[l])

    dec_dist = np.concatenate([past - np.arange(past), np.zeros(PAGE_SIZE, np.int64)])
    n_prompt_b = 3 * Q_BLOCK * Q_BLOCK
    buckets = np.concatenate([_prompt_bias_buckets().reshape(-1), _t5_bucket_np(dec_dist)])
    pad = (-buckets.shape[0]) % BIAS_TJ
    bias_all = _bias_lookup(rel_bias, np.concatenate([buckets, np.zeros(pad, np.int32)]))
    btile = bias_all[:, :n_prompt_b].reshape(ATT_HEADS, 3, Q_BLOCK, Q_BLOCK)
    bias_dec = bias_all[:, n_prompt_b:n_prompt_b + past + PAGE_SIZE]

    xp = x_prompt.reshape(n, D_MODEL)
    proj = _in_proj(xp, norm1_g[l], w_perm, min(1024, n))
    y_ssd, h_t = _ssd_prompt(proj, bsz, *ssd_w)
    prep = _attn_prep(proj, q_norm_g[l], k_norm_g[l], min(512, n))
    y_att = _attn_prompt(prep, _attn_select(prep, bsz), btile, bsz)
    x1, hb = _merge(y_ssd, y_att, proj, xp, wa, wb, wo, norm2_g[l], min(512, n))
    y_prompt = _channel_mixer(hb, x1, wq, keys, ub, vtb, min(512, n))

    k_prompt = prep[1].reshape(1, bsz, t, KV_HEADS, ATT_HEAD_DIM)
    v_prompt = prep[2].reshape(1, bsz, t, KV_HEADS, ATT_HEAD_DIM)
    ik_prompt = prep[3].reshape(1, bsz, t, IDX_DIM)
    ssm_prompt = h_t.reshape(1, bsz, SSD_HEADS, SSD_HEAD_DIM, D_STATE)
    conv_prompt = proj.reshape(bsz, t, PROJ_COLS)[:, t - (CONV_W - 1):, :CONV_DIM][None]

    xs_ = x_sample.reshape(db, D_MODEL)
    proj_s = _in_proj(xs_, norm1_g[l], w_perm, db)
    y_ssd_s, h_s, conv_s = _ssd_step(proj_s, state_conv[l].reshape(db, (CONV_W - 1) * CONV_DIM),
                                     state_ssm[l].reshape(db, N_PAIRS, LANES, D_STATE), *ssd_w)
    prep_s = _attn_prep(proj_s, q_norm_g[l], k_norm_g[l], db)
    qn_s, kn_s, v_s, ik_s, _, _, _, _, _, _, iw_s = prep_s
    iq8 = proj_s[:, COL_IQ:COL_IQ + IDX_HEADS * IDX_DIM].reshape(db * IDX_HEADS, IDX_DIM)
    w8 = iw_s[:, SMALL_IW:SMALL_IW + IDX_HEADS].reshape(db * IDX_HEADS, 1)
    ikt_pages = jnp.transpose(cache_idx_k[l], (0, 2, 1))
    kt_pages = jnp.transpose(cache_k[l], (0, 2, 3, 1)).reshape(-1, KV_WIDTH, PAGE_SIZE)
    vt_pages = jnp.transpose(cache_v[l], (0, 2, 3, 1)).reshape(-1, KV_WIDTH, PAGE_SIZE)
    scores = _dec_scores(page_table, ikt_pages, iq8, w8, ik_s)
    n_sel = min(TOPK_KEYS_MAX, (past + ds) // 4)
    mask = _dec_select(scores.reshape(db, past + PAGE_SIZE), n_sel)
    y_att_s = _dec_attn(page_table, qn_s.reshape(db * ATT_HEADS, ATT_HEAD_DIM),
                        mask.reshape(db, 1, past + PAGE_SIZE), bias_dec, kn_s, v_s,
                        kt_pages, vt_pages)
    x1_s, hb_s = _merge(y_ssd_s, y_att_s.reshape(db, ATT_WIDTH), proj_s, xs_, wa, wb, wo, norm2_g[l], db)
    y_sample = _channel_mixer(hb_s, x1_s, wq, keys, ub, vtb, db)

    return (y_prompt.reshape(bsz, t, D_MODEL), y_sample.reshape(db, ds, D_MODEL),
            k_prompt, v_prompt, ik_prompt, ssm_prompt, conv_prompt,
            kn_s.reshape(1, db, ds, KV_HEADS, ATT_HEAD_DIM), v_s.reshape(1, db, ds, KV_HEADS, ATT_HEAD_DIM),
            ik_s.reshape(1, db, ds, IDX_DIM),
            h_s.reshape(1, db, SSD_HEADS, SSD_HEAD_DIM, D_STATE),
            conv_s.reshape(1, db, CONV_W - 1, CONV_DIM))
```
